```python
import math
import jax
import jax.numpy as jnp
from jax import lax
import numpy as np

D_MODEL = 2048
BATCH = 1
SEQ = 8192
DEPTH = 4
DEC_BATCH = 1
DEC_SEQ = 16384
PAST_LEN = 128

N_EVEN = (DEPTH + 1) // 2
N_ODD = DEPTH // 2
EPS = 1e-6
Q_BLOCK = 128
GRID_W = 64
N_MEM = 256
D_FF = 5632

S5_WIDTH = D_MODEL // 2
S5_GROUP = 16
S5_GROUPS = S5_WIDTH // S5_GROUP
S5_STATE = 64
S5_DT_MIN = 1e-3
S5_DT_MAX = 1e-1
S5_LAMBDA_RE_MAX = -1e-4

DIFF_WIDTH = D_MODEL - S5_WIDTH
DIFF_HEAD = 64
DIFF_HEADS = DIFF_WIDTH // (2 * DIFF_HEAD)
DIFF_VHEAD = 2 * DIFF_HEAD
DIFF_SUBLN_EPS = 1e-5
EVEN_IN = S5_WIDTH + 3 * DIFF_WIDTH

GQA_HEAD = 128
GQA_HEADS = D_MODEL // GQA_HEAD
GQA_KV_HEADS = 4
GQA_GROUP = GQA_HEADS // GQA_KV_HEADS
ODD_IN = (GQA_HEADS + 2 * GQA_KV_HEADS) * GQA_HEAD
ROPE_AXIS = GQA_HEAD // 2
ROPE_THETA = 10000.0

X_HEADS = 4
X_HEAD = D_MODEL // X_HEADS

kernel_name = "hybrid_s5_diffattn_axial_gqa_encoder"


def rmsnorm(x, g, eps=EPS):
    xf = x.astype(jnp.float32)
    y = xf * lax.rsqrt(jnp.mean(xf * xf, axis=-1, keepdims=True) + eps)
    return (y * g.astype(jnp.float32)).astype(x.dtype)


def swiglu(h, w_gu, w_down):
    g, u = jnp.split(h @ w_gu, 2, axis=-1)
    return (jax.nn.silu(g) * u) @ w_down


def alibi_slopes(n):
    return jnp.asarray([2.0 ** (-8.0 * (i + 1) / n) for i in range(n)], dtype=jnp.float32)


def s5_combine(e1, e2):
    a1r, a1i, b1r, b1i = e1
    a2r, a2i, b2r, b2i = e2
    ar = a2r * a1r - a2i * a1i
    ai = a2r * a1i + a2i * a1r
    br = a2r * b1r - a2i * b1i + b2r
    bi = a2r * b1i + a2i * b1r + b2i
    return (ar, ai, br, bi)


def s5_mixer(u, lam_re, lam_im, log_dt, b_re, b_im, c_re, c_im, d_skip, glu_w, glu_b):
    bsz, L, _ = u.shape
    uf = u.astype(jnp.float32).reshape(bsz, L, S5_GROUPS, S5_GROUP)
    y = d_skip.astype(jnp.float32).reshape(S5_GROUPS, S5_GROUP) * uf
    for direction in range(2):
        lr = jnp.minimum(lam_re[direction].astype(jnp.float32), S5_LAMBDA_RE_MAX)
        li = lam_im[direction].astype(jnp.float32)
        dt = jnp.exp(log_dt[direction].astype(jnp.float32))[:, None]
        mag = jnp.exp(lr * dt)
        ab_re = mag * jnp.cos(li * dt)
        ab_im = mag * jnp.sin(li * dt)
        nr = ab_re - 1.0
        den = lr * lr + li * li
        f_re = ((nr * lr + ab_im * li) / den)[..., None]
        f_im = ((ab_im * lr - nr * li) / den)[..., None]
        br = b_re[direction].astype(jnp.float32)
        bi = b_im[direction].astype(jnp.float32)
        bb_re = f_re * br - f_im * bi
        bb_im = f_re * bi + f_im * br
        bu_re = jnp.einsum('blgh,gph->blgp', uf, bb_re)
        bu_im = jnp.einsum('blgh,gph->blgp', uf, bb_im)
        a_re = jnp.broadcast_to(ab_re, bu_re.shape)
        a_im = jnp.broadcast_to(ab_im, bu_im.shape)
        _, _, x_re, x_im = lax.associative_scan(
            s5_combine, (a_re, a_im, bu_re, bu_im), reverse=(direction == 1), axis=1)
        cr = c_re[direction].astype(jnp.float32)
        ci = c_im[direction].astype(jnp.float32)
        y = y + jnp.einsum('blgp,ghp->blgh', x_re, cr) - jnp.einsum('blgp,ghp->blgh', x_im, ci)
    y = y.reshape(bsz, L, S5_WIDTH).astype(u.dtype)
    g = jax.nn.gelu(y)
    return g * jax.nn.sigmoid(g @ glu_w + glu_b)


def diff_attention(q, k, v, lam, lambda_init, subln_g):
    bsz, H, _, L, d = q.shape
    nb = L // Q_BLOCK
    slopes = alibi_slopes(H)
    kpos = jnp.arange(L, dtype=jnp.int32)
    scale = d ** -0.5
    qb = q.reshape(bsz, H, 2, nb, Q_BLOCK, d).transpose(3, 0, 1, 2, 4, 5)
    starts = jnp.arange(nb, dtype=jnp.int32) * Q_BLOCK

    def block(args):
        qi, s0 = args
        qpos = s0 + jnp.arange(Q_BLOCK, dtype=jnp.int32)
        dist = jnp.abs(qpos[:, None] - kpos[None, :]).astype(jnp.float32)
        bias = -slopes[:, None, None] * dist
        s = jnp.einsum('bhmqd,bhmkd->bhmqk', qi, k,
                       preferred_element_type=jnp.float32) * scale + bias[None, :, None]
        p = jax.nn.softmax(s, axis=-1)
        a = p[:, :, 0] - lam * p[:, :, 1]
        return jnp.einsum('bhqk,bhke->bhqe', a.astype(v.dtype), v)

    o = lax.map(block, (qb, starts))
    o = o.transpose(1, 2, 0, 3, 4).reshape(bsz, H, L, 2 * d)
    return rmsnorm(o, subln_g, eps=DIFF_SUBLN_EPS) * (1.0 - lambda_init)


def even_mixer(h, w_in, w_out, lam_re, lam_im, log_dt, b_re, b_im, c_re, c_im, d_skip,
               glu_w, glu_b, lq1, lk1, lq2, lk2, subln_g, layer_idx):
    bsz, L, _ = h.shape
    z = h @ w_in
    u, q, k, v = jnp.split(z, [S5_WIDTH, S5_WIDTH + DIFF_WIDTH, S5_WIDTH + 2 * DIFF_WIDTH], axis=-1)
    ya = s5_mixer(u, lam_re, lam_im, log_dt, b_re, b_im, c_re, c_im, d_skip, glu_w, glu_b)
    q = q.reshape(bsz, L, DIFF_HEADS, 2, DIFF_HEAD).transpose(0, 2, 3, 1, 4)
    k = k.reshape(bsz, L, DIFF_HEADS, 2, DIFF_HEAD).transpose(0, 2, 3, 1, 4)
    v = v.reshape(bsz, L, DIFF_HEADS, DIFF_VHEAD).transpose(0, 2, 1, 3)
    lambda_init = 0.8 - 0.6 * math.exp(-0.3 * layer_idx)
    lam = (jnp.exp(jnp.sum(lq1.astype(jnp.float32) * lk1.astype(jnp.float32)))
           - jnp.exp(jnp.sum(lq2.astype(jnp.float32) * lk2.astype(jnp.float32))) + lambda_init)
    yb = diff_attention(q, k, v, lam, lambda_init, subln_g)
    yb = yb.transpose(0, 2, 1, 3).reshape(bsz, L, DIFF_WIDTH)
    return jnp.concatenate([ya, yb.astype(ya.dtype)], axis=-1) @ w_out


def axial_rope_tables(L):
    rows = L // GRID_W
    r = jnp.repeat(jnp.arange(rows, dtype=jnp.float32), GRID_W)
    c = jnp.tile(jnp.arange(GRID_W, dtype=jnp.float32), rows)
    inv = ROPE_THETA ** (-jnp.arange(0, ROPE_AXIS, 2, dtype=jnp.float32) / ROPE_AXIS)
    ang = jnp.stack([r[:, None] * inv, c[:, None] * inv], axis=1)
    return jnp.cos(ang), jnp.sin(ang)


def apply_axial_rope(x, cos, sin):
    xs = x.astype(jnp.float32).reshape(x.shape[:-1] + (2, 2, ROPE_AXIS // 2))
    x1 = xs[..., 0, :]
    x2 = xs[..., 1, :]
    o1 = x1 * cos - x2 * sin
    o2 = x2 * cos + x1 * sin
    return jnp.stack([o1, o2], axis=-2).reshape(x.shape).astype(x.dtype)


def gqa_attention(q, k, v):
    bsz, hk, g, L, d = q.shape
    nb = L // Q_BLOCK
    scale = d ** -0.5
    qb = q.reshape(bsz, hk, g, nb, Q_BLOCK, d).transpose(3, 0, 1, 2, 4, 5)

    def block(qi):
        s = jnp.einsum('bhgqd,bhkd->bhgqk', qi, k, preferred_element_type=jnp.float32) * scale
        p = jax.nn.softmax(s, axis=-1)
        return jnp.einsum('bhgqk,bhkd->bhgqd', p.astype(v.dtype), v)

    o = lax.map(block, qb)
    return o.transpose(1, 2, 3, 0, 4, 5).reshape(bsz, hk, g, L, d)


def odd_mixer(h, w_in, w_out, q_g, k_g, cos, sin):
    bsz, L, _ = h.shape
    z = h @ w_in
    q, k, v = jnp.split(z, [GQA_HEADS * GQA_HEAD, (GQA_HEADS + GQA_KV_HEADS) * GQA_HEAD], axis=-1)
    q = rmsnorm(q.reshape(bsz, L, GQA_HEADS, GQA_HEAD), q_g).transpose(0, 2, 1, 3)
    k = rmsnorm(k.reshape(bsz, L, GQA_KV_HEADS, GQA_HEAD), k_g).transpose(0, 2, 1, 3)
    v = v.reshape(bsz, L, GQA_KV_HEADS, GQA_HEAD).transpose(0, 2, 1, 3)
    q = apply_axial_rope(q, cos, sin).reshape(bsz, GQA_KV_HEADS, GQA_GROUP, L, GQA_HEAD)
    k = apply_axial_rope(k, cos, sin)
    o = gqa_attention(q, k, v)
    o = o.transpose(0, 3, 1, 2, 4).reshape(bsz, L, D_MODEL)
    return o @ w_out


def cross_attention(h, m, w_q, w_kv, w_o):
    bsz, L, _ = h.shape
    n = m.shape[1]
    q = (h @ w_q).reshape(bsz, L, X_HEADS, X_HEAD)
    k, v = jnp.split(m @ w_kv, 2, axis=-1)
    k = k.reshape(bsz, n, X_HEADS, X_HEAD)
    v = v.reshape(bsz, n, X_HEADS, X_HEAD)
    s = jnp.einsum('blhd,bnhd->bhln', q, k, preferred_element_type=jnp.float32) * (X_HEAD ** -0.5)
    p = jax.nn.softmax(s, axis=-1)
    o = jnp.einsum('bhln,bnhd->blhd', p.astype(v.dtype), v).reshape(bsz, L, D_MODEL)
    return o @ w_o


def encoder(x, mem, p):
    L = x.shape[1]
    cos, sin = axial_rope_tables(L)
    for l in range(DEPTH):
        x = x + 0.5 * swiglu(rmsnorm(x, p['ffn1_norm'][l]), p['ffn1_w_gu'][l], p['ffn1_w_down'][l])
        h = rmsnorm(x, p['mix_norm'][l])
        if l % 2 == 0:
            e = l // 2
            x = x + even_mixer(h, p['even_w_in'][e], p['even_w_out'][e],
                               p['s5_lambda_re'][e], p['s5_lambda_im'][e], p['s5_log_dt'][e],
                               p['s5_b_re'][e], p['s5_b_im'][e], p['s5_c_re'][e], p['s5_c_im'][e],
                               p['s5_d'][e], p['s5_glu_w'][e], p['s5_glu_b'][e],
                               p['diff_lambda_q1'][e], p['diff_lambda_k1'][e],
                               p['diff_lambda_q2'][e], p['diff_lambda_k2'][e],
                               p['diff_subln'][e], l)
        else:
            o = l // 2
            x = x + odd_mixer(h, p['odd_w_in'][o], p['odd_w_out'][o],
                              p['gqa_q_norm'][o], p['gqa_k_norm'][o], cos, sin)
        x = x + cross_attention(rmsnorm(x, p['cross_norm'][l]), rmsnorm(mem, p['mem_norm'][l]),
                                p['cross_w_q'][l], p['cross_w_kv'][l], p['cross_w_o'][l])
        x = x + 0.5 * swiglu(rmsnorm(x, p['ffn2_norm'][l]), p['ffn2_w_gu'][l], p['ffn2_w_down'][l])
    return rmsnorm(x, p['final_norm'])


def _dense(k, shape, fan_in):
    return jax.random.normal(k, shape, jnp.float32) * (fan_in ** -0.5)


def _gain(k, shape):
    return 1.0 + 0.02 * jax.random.normal(k, shape, jnp.float32)


def setup_inputs(seed: int = 0) -> dict:
    key = jax.random.key(seed)
    ks = list(jax.random.split(key, 48))
    G, P, H = S5_GROUPS, S5_STATE, S5_GROUP
    n_idx = jnp.arange(P, dtype=jnp.float32)
    d = {}
    d['x_prompt'] = jax.random.normal(ks[0], (BATCH, SEQ, D_MODEL), jnp.float32)
    d['x_sample'] = jax.random.normal(ks[1], (DEC_BATCH, DEC_SEQ, D_MODEL), jnp.float32)
    d['mem_prompt'] = jax.random.normal(ks[2], (BATCH, N_MEM, D_MODEL), jnp.float32)
    d['mem_sample'] = jax.random.normal(ks[3], (DEC_BATCH, N_MEM, D_MODEL), jnp.float32)
    d['ffn1_norm'] = _gain(ks[4], (DEPTH, D_MODEL))
    d['ffn1_w_gu'] = _dense(ks[5], (DEPTH, D_MODEL, 2 * D_FF), D_MODEL)
    d['ffn1_w_down'] = _dense(ks[6], (DEPTH, D_FF, D_MODEL), D_FF)
    d['mix_norm'] = _gain(ks[7], (DEPTH, D_MODEL))
    d['even_w_in'] = _dense(ks[8], (N_EVEN, D_MODEL, EVEN_IN), D_MODEL)
    d['even_w_out'] = _dense(ks[9], (N_EVEN, D_MODEL, D_MODEL), D_MODEL)
    d['s5_lambda_re'] = -0.5 + 0.01 * jax.random.normal(ks[10], (N_EVEN, 2, G, P), jnp.float32)
    d['s5_lambda_im'] = math.pi * n_idx + 0.01 * jax.random.normal(ks[11], (N_EVEN, 2, G, P), jnp.float32)
    d['s5_log_dt'] = jax.random.uniform(ks[12], (N_EVEN, 2, G), jnp.float32,
                                        math.log(S5_DT_MIN), math.log(S5_DT_MAX))
    d['s5_b_re'] = _dense(ks[13], (N_EVEN, 2, G, P, H), 2 * H)
    d['s5_b_im'] = _dense(ks[14], (N_EVEN, 2, G, P, H), 2 * H)
    d['s5_c_re'] = _dense(ks[15], (N_EVEN, 2, G, H, P), 2 * P)
    d['s5_c_im'] = _dense(ks[16], (N_EVEN, 2, G, H, P), 2 * P)
    d['s5_d'] = jax.random.normal(ks[17], (N_EVEN, S5_WIDTH), jnp.float32)
    d['s5_glu_w'] = _dense(ks[18], (N_EVEN, S5_WIDTH, S5_WIDTH), S5_WIDTH)
    d['s5_glu_b'] = 0.01 * jax.random.normal(ks[19], (N_EVEN, S5_WIDTH), jnp.float32)
    d['diff_lambda_q1'] = 0.1 * jax.random.normal(ks[20], (N_EVEN, DIFF_HEAD), jnp.float32)
    d['diff_lambda_k1'] = 0.1 * jax.random.normal(ks[21], (N_EVEN, DIFF_HEAD), jnp.float32)
    d['diff_lambda_q2'] = 0.1 * jax.random.normal(ks[22], (N_EVEN, DIFF_HEAD), jnp.float32)
    d['diff_lambda_k2'] = 0.1 * jax.random.normal(ks[23], (N_EVEN, DIFF_HEAD), jnp.float32)
    d['diff_subln'] = _gain(ks[24], (N_EVEN, DIFF_VHEAD))
    d['odd_w_in'] = _dense(ks[25], (N_ODD, D_MODEL, ODD_IN), D_MODEL)
    d['odd_w_out'] = _dense(ks[26], (N_ODD, D_MODEL, D_MODEL), D_MODEL)
    d['gqa_q_norm'] = _gain(ks[27], (N_ODD, GQA_HEAD))
    d['gqa_k_norm'] = _gain(ks[28], (N_ODD, GQA_HEAD))
    d['cross_norm'] = _gain(ks[29], (DEPTH, D_MODEL))
    d['mem_norm'] = _gain(ks[30], (DEPTH, D_MODEL))
    d['cross_w_q'] = _dense(ks[31], (DEPTH, D_MODEL, D_MODEL), D_MODEL)
    d['cross_w_kv'] = _dense(ks[32], (DEPTH, D_MODEL, 2 * D_MODEL), D_MODEL)
    d['cross_w_o'] = _dense(ks[33], (DEPTH, D_MODEL, D_MODEL), D_MODEL)
    d['ffn2_norm'] = _gain(ks[34], (DEPTH, D_MODEL))
    d['ffn2_w_gu'] = _dense(ks[35], (DEPTH, D_MODEL, 2 * D_FF), D_MODEL)
    d['ffn2_w_down'] = _dense(ks[36], (DEPTH, D_FF, D_MODEL), D_FF)
    d['final_norm'] = _gain(ks[37], (D_MODEL,))
    return d


def reference(x_prompt, x_sample, mem_prompt, mem_sample,
              ffn1_norm, ffn1_w_gu, ffn1_w_down, mix_norm,
              even_w_in, even_w_out, s5_lambda_re, s5_lambda_im, s5_log_dt,
              s5_b_re, s5_b_im, s5_c_re, s5_c_im, s5_d, s5_glu_w, s5_glu_b,
              diff_lambda_q1, diff_lambda_k1, diff_lambda_q2, diff_lambda_k2, diff_subln,
              odd_w_in, odd_w_out, gqa_q_norm, gqa_k_norm,
              cross_norm, mem_norm, cross_w_q, cross_w_kv, cross_w_o,
              ffn2_norm, ffn2_w_gu, ffn2_w_down, final_norm):
    p = dict(ffn1_norm=ffn1_norm, ffn1_w_gu=ffn1_w_gu, ffn1_w_down=ffn1_w_down, mix_norm=mix_norm,
             even_w_in=even_w_in, even_w_out=even_w_out, s5_lambda_re=s5_lambda_re,
             s5_lambda_im=s5_lambda_im, s5_log_dt=s5_log_dt, s5_b_re=s5_b_re, s5_b_im=s5_b_im,
             s5_c_re=s5_c_re, s5_c_im=s5_c_im, s5_d=s5_d, s5_glu_w=s5_glu_w, s5_glu_b=s5_glu_b,
             diff_lambda_q1=diff_lambda_q1, diff_lambda_k1=diff_lambda_k1,
             diff_lambda_q2=diff_lambda_q2, diff_lambda_k2=diff_lambda_k2, diff_subln=diff_subln,
             odd_w_in=odd_w_in, odd_w_out=odd_w_out, gqa_q_norm=gqa_q_norm, gqa_k_norm=gqa_k_norm,
             cross_norm=cross_norm, mem_norm=mem_norm, cross_w_q=cross_w_q, cross_w_kv=cross_w_kv,
             cross_w_o=cross_w_o, ffn2_norm=ffn2_norm, ffn2_w_gu=ffn2_w_gu, ffn2_w_down=ffn2_w_down,
             final_norm=final_norm)
    y_prompt = encoder(x_prompt, mem_prompt, p)
    y_sample = encoder(x_sample, mem_sample, p)
    return (y_prompt, y_sample)
```

```python
import functools
import math

import jax
import jax.numpy as jnp
from jax import lax
from jax.experimental import pallas as pl
from jax.experimental.pallas import tpu as pltpu

F32 = jnp.float32
BF16 = jnp.bfloat16

D_MODEL = 2048
DEPTH = 4
EPS = 1e-6
D_FF = 5632
GRID_W = 64

S5_WIDTH = D_MODEL // 2
S5_GROUP = 16
S5_GROUPS = S5_WIDTH // S5_GROUP
S5_STATE = 64
S5_LAMBDA_RE_MAX = -1e-4

DIFF_WIDTH = D_MODEL - S5_WIDTH
DIFF_HEAD = 64
DIFF_HEADS = DIFF_WIDTH // (2 * DIFF_HEAD)
DIFF_VHEAD = 2 * DIFF_HEAD
DIFF_SUBLN_EPS = 1e-5

GQA_HEAD = 128
GQA_HEADS = D_MODEL // GQA_HEAD
GQA_KV_HEADS = 4
GQA_GROUP = GQA_HEADS // GQA_KV_HEADS
ROPE_AXIS = GQA_HEAD // 2
ROPE_THETA = 10000.0

X_HEADS = 4
X_HEAD = D_MODEL // X_HEADS

LANES = 128
V7X_VMEM_BYTES = 64 * 1024 * 1024
VMEM_LIMIT_BYTES = V7X_VMEM_BYTES - 8 * 1024 * 1024

S5_CHUNK = 16
S5_LANE_GROUPS = LANES // S5_GROUP
S5_LANE_BLOCKS = S5_WIDTH // LANES
S5_BLOCK_COLS = S5_CHUNK * LANES
S5_BLOCK_STATE = S5_LANE_GROUPS * S5_STATE


def _tile(n, pref):
    t = min(pref, n)
    while n % t:
        t //= 2
    return t


def _cparams(*sem):
    return pltpu.CompilerParams(dimension_semantics=sem, vmem_limit_bytes=VMEM_LIMIT_BYTES)


def _rmsnorm(x, g, eps):
    ms = jnp.mean(x * x, axis=-1, keepdims=True)
    return x * lax.rsqrt(ms + eps) * g


def _sigmoid(x):
    return 1.0 / (1.0 + jnp.exp(-x))


def _gelu_tanh(x):
    c = math.sqrt(2.0 / math.pi)
    return x * (0.5 * (1.0 + jnp.tanh(c * (x + 0.044715 * (x * x * x)))))


def _ffn_kernel(x_ref, g_ref, wg_ref, wu_ref, wd_ref, *rest, n_f, final):
    if final:
        fg_ref, o_ref, h_ref, acc_ref = rest
    else:
        o_ref, h_ref, acc_ref = rest
    j = pl.program_id(1)

    @pl.when(j == 0)
    def _():
        h_ref[...] = _rmsnorm(x_ref[...], g_ref[...], EPS).astype(BF16)
        acc_ref[...] = jnp.zeros_like(acc_ref)

    h = h_ref[...]
    gate = jnp.dot(h, wg_ref[...], preferred_element_type=F32)
    up = jnp.dot(h, wu_ref[...], preferred_element_type=F32)
    act = gate * _sigmoid(gate) * up
    acc_ref[...] += jnp.dot(act.astype(BF16), wd_ref[...], preferred_element_type=F32)

    @pl.when(j == n_f - 1)
    def _():
        y = x_ref[...] + 0.5 * acc_ref[...]
        if final:
            y = _rmsnorm(y, fg_ref[...], EPS)
        o_ref[...] = y


def _ffn(x, g, w_gu, w_down, final_g=None):
    L = x.shape[0]
    tm = _tile(L, 512)
    tf = 512
    n_f = D_FF // tf
    final = final_g is not None
    in_specs = [
        pl.BlockSpec((tm, D_MODEL), lambda i, j: (i, 0)),
        pl.BlockSpec((1, D_MODEL), lambda i, j: (0, 0)),
        pl.BlockSpec((D_MODEL, tf), lambda i, j: (0, j)),
        pl.BlockSpec((D_MODEL, tf), lambda i, j: (0, j + n_f)),
        pl.BlockSpec((tf, D_MODEL), lambda i, j: (j, 0)),
    ]
    args = [x, g.reshape(1, D_MODEL), w_gu, w_gu, w_down]
    if final:
        in_specs.append(pl.BlockSpec((1, D_MODEL), lambda i, j: (0, 0)))
        args.append(final_g.reshape(1, D_MODEL))
    return pl.pallas_call(
        functools.partial(_ffn_kernel, n_f=n_f, final=final),
        grid=(L // tm, n_f),
        in_specs=in_specs,
        out_specs=pl.BlockSpec((tm, D_MODEL), lambda i, j: (i, 0)),
        out_shape=jax.ShapeDtypeStruct((L, D_MODEL), F32),
        scratch_shapes=[pltpu.VMEM((tm, D_MODEL), BF16), pltpu.VMEM((tm, D_MODEL), F32)],
        compiler_params=_cparams("parallel", "arbitrary"),
        name="ffn_final" if final else "ffn",
    )(*args)


def _norm_matmul_kernel(x_ref, g_ref, w_ref, o_ref, h_ref):
    @pl.when(pl.program_id(1) == 0)
    def _():
        h_ref[...] = _rmsnorm(x_ref[...], g_ref[...], EPS).astype(BF16)

    o_ref[...] = jnp.dot(h_ref[...], w_ref[...], preferred_element_type=F32).astype(o_ref.dtype)


def _norm_matmul(x, g, w, out_dtype):
    L = x.shape[0]
    N = w.shape[1]
    tm = _tile(L, 512)
    tn = _tile(N, 512)
    return pl.pallas_call(
        _norm_matmul_kernel,
        grid=(L // tm, N // tn),
        in_specs=[
            pl.BlockSpec((tm, D_MODEL), lambda i, j: (i, 0)),
            pl.BlockSpec((1, D_MODEL), lambda i, j: (0, 0)),
            pl.BlockSpec((D_MODEL, tn), lambda i, j: (0, j)),
        ],
        out_specs=pl.BlockSpec((tm, tn), lambda i, j: (i, j)),
        out_shape=jax.ShapeDtypeStruct((L, N), out_dtype),
        scratch_shapes=[pltpu.VMEM((tm, D_MODEL), BF16)],
        compiler_params=_cparams("parallel", "arbitrary"),
        name="norm_matmul",
    )(x, g.reshape(1, D_MODEL), w)


def _odd_proj_kernel(x_ref, g_ref, w_ref, qg_ref, kg_ref, cos_ref, sin_ref, o_ref, h_ref, *, heads_per_tile):
    j = pl.program_id(1)
    n_q_tiles = GQA_HEADS // heads_per_tile
    n_k_tiles = GQA_KV_HEADS // heads_per_tile

    @pl.when(j == 0)
    def _():
        h_ref[...] = _rmsnorm(x_ref[...], g_ref[...], EPS).astype(BF16)

    z = jnp.dot(h_ref[...], w_ref[...], preferred_element_type=F32)

    @pl.when(j < n_q_tiles + n_k_tiles)
    def _():
        gain = jnp.where(j < n_q_tiles, qg_ref[...], kg_ref[...])
        cos = cos_ref[...]
        sin = sin_ref[...]
        lane = lax.broadcasted_iota(jnp.int32, (1, GQA_HEAD), 1)
        first_half = (lane & (ROPE_AXIS - 1)) < (ROPE_AXIS // 2)
        for hh in range(heads_per_tile):
            sl = slice(hh * GQA_HEAD, (hh + 1) * GQA_HEAD)
            y = _rmsnorm(z[:, sl], gain, EPS)
            partner = jnp.where(first_half, pltpu.roll(y, GQA_HEAD - ROPE_AXIS // 2, 1),
                                pltpu.roll(y, ROPE_AXIS // 2, 1))
            o_ref[:, sl] = (y * cos + partner * sin).astype(o_ref.dtype)

    @pl.when(j >= n_q_tiles + n_k_tiles)
    def _():
        o_ref[...] = z.astype(o_ref.dtype)


def _odd_proj(x, g, w, q_g, k_g, cos, sin):
    L = x.shape[0]
    N = w.shape[1]
    tm = _tile(L, 512)
    heads_per_tile = 4
    tn = heads_per_tile * GQA_HEAD
    return pl.pallas_call(
        functools.partial(_odd_proj_kernel, heads_per_tile=heads_per_tile),
        grid=(L // tm, N // tn),
        in_specs=[
            pl.BlockSpec((tm, D_MODEL), lambda i, j: (i, 0)),
            pl.BlockSpec((1, D_MODEL), lambda i, j: (0, 0)),
            pl.BlockSpec((D_MODEL, tn), lambda i, j: (0, j)),
            pl.BlockSpec((1, GQA_HEAD), lambda i, j: (0, 0)),
            pl.BlockSpec((1, GQA_HEAD), lambda i, j: (0, 0)),
            pl.BlockSpec((tm, GQA_HEAD), lambda i, j: (i, 0)),
            pl.BlockSpec((tm, GQA_HEAD), lambda i, j: (i, 0)),
        ],
        out_specs=pl.BlockSpec((tm, tn), lambda i, j: (i, j)),
        out_shape=jax.ShapeDtypeStruct((L, N), BF16),
        scratch_shapes=[pltpu.VMEM((tm, D_MODEL), BF16)],
        compiler_params=_cparams("parallel", "arbitrary"),
        name="odd_proj",
    )(x, g.reshape(1, D_MODEL), w, q_g.reshape(1, GQA_HEAD), k_g.reshape(1, GQA_HEAD), cos, sin)


def _matmul_residual_kernel(a_ref, w_ref, r_ref, o_ref):
    o_ref[...] = r_ref[...] + jnp.dot(a_ref[...], w_ref[...], preferred_element_type=F32)


def _matmul_residual(a, w, res):
    L, K = a.shape
    N = w.shape[1]
    tm = _tile(L, 512)
    return pl.pallas_call(
        _matmul_residual_kernel,
        grid=(L // tm,),
        in_specs=[
            pl.BlockSpec((tm, K), lambda i: (i, 0)),
            pl.BlockSpec((K, N), lambda i: (0, 0)),
            pl.BlockSpec((tm, N), lambda i: (i, 0)),
        ],
        out_specs=pl.BlockSpec((tm, N), lambda i: (i, 0)),
        out_shape=jax.ShapeDtypeStruct((L, N), F32),
        compiler_params=_cparams("parallel"),
        name="matmul_residual",
    )(a, w, res)


def _online_softmax_step(s, v, m_ref, l_ref, acc_ref):
    m_prev = m_ref[...]
    m_new = jnp.maximum(m_prev, jnp.max(s, axis=-1, keepdims=True))
    alpha = jnp.exp(m_prev - m_new)
    p = jnp.exp(s - m_new)
    l_ref[...] = alpha * l_ref[...] + jnp.sum(p, axis=-1, keepdims=True)
    acc_ref[...] = alpha * acc_ref[...] + jnp.dot(p.astype(v.dtype), v, preferred_element_type=F32)
    m_ref[...] = m_new


def _softmax_init(m_ref, l_ref, acc_ref):
    m_ref[...] = jnp.full_like(m_ref, -jnp.inf)
    l_ref[...] = jnp.zeros_like(l_ref)
    acc_ref[...] = jnp.zeros_like(acc_ref)


def _gqa_kernel(q_ref, k_ref, v_ref, o_ref, m_ref, l_ref, acc_ref, *, n_k, tq, scale):
    ki = pl.program_id(2)

    @pl.when(ki == 0)
    def _():
        _softmax_init(m_ref, l_ref, acc_ref)

    q = q_ref[...]
    qs = jnp.concatenate([q[:, g * GQA_HEAD:(g + 1) * GQA_HEAD] for g in range(GQA_GROUP)], axis=0)
    s = lax.dot_general(qs, k_ref[...], (((1,), (1,)), ((), ())), preferred_element_type=F32) * scale
    _online_softmax_step(s, v_ref[...], m_ref, l_ref, acc_ref)

    @pl.when(ki == n_k - 1)
    def _():
        o = acc_ref[...] / l_ref[...]
        for g in range(GQA_GROUP):
            o_ref[:, g * GQA_HEAD:(g + 1) * GQA_HEAD] = o[g * tq:(g + 1) * tq].astype(o_ref.dtype)


def _gqa_attention(qkv):
    L = qkv.shape[0]
    tq = _tile(L, 256)
    tk = _tile(L, 512)
    n_k = L // tk
    group_cols = GQA_GROUP * GQA_HEAD
    return pl.pallas_call(
        functools.partial(_gqa_kernel, n_k=n_k, tq=tq, scale=GQA_HEAD ** -0.5),
        grid=(GQA_KV_HEADS, L // tq, n_k),
        in_specs=[
            pl.BlockSpec((tq, group_cols), lambda h, i, j: (i, h)),
            pl.BlockSpec((tk, GQA_HEAD), lambda h, i, j: (j, GQA_HEADS + h)),
            pl.BlockSpec((tk, GQA_HEAD), lambda h, i, j: (j, GQA_HEADS + GQA_KV_HEADS + h)),
        ],
        out_specs=pl.BlockSpec((tq, group_cols), lambda h, i, j: (i, h)),
        out_shape=jax.ShapeDtypeStruct((L, D_MODEL), BF16),
        scratch_shapes=[
            pltpu.VMEM((GQA_GROUP * tq, 1), F32),
            pltpu.VMEM((GQA_GROUP * tq, 1), F32),
            pltpu.VMEM((GQA_GROUP * tq, GQA_HEAD), F32),
        ],
        compiler_params=_cparams("parallel", "parallel", "arbitrary"),
        name="gqa_attention",
    )(qkv, qkv, qkv)


def _diff_kernel(slope_ref, q_ref, k_ref, v_ref, lq1_ref, lk1_ref, lq2_ref, lk2_ref, sg_ref, o_ref,
                 m_ref, l_ref, acc_ref, *, n_k, tq, tk, lambda_init):
    h = pl.program_id(0)
    qi = pl.program_id(1)
    ki = pl.program_id(2)

    @pl.when(ki == 0)
    def _():
        _softmax_init(m_ref, l_ref, acc_ref)

    q = q_ref[...]
    lane = lax.broadcasted_iota(jnp.int32, (1, 2 * DIFF_HEAD), 1)
    zero = jnp.zeros_like(q)
    qs = jnp.concatenate([jnp.where(lane < DIFF_HEAD, q, zero), jnp.where(lane >= DIFF_HEAD, q, zero)], axis=0)
    s = lax.dot_general(qs, k_ref[...], (((1,), (1,)), ((), ())), preferred_element_type=F32)
    rel = (qi * tq - ki * tk) + (lax.broadcasted_iota(jnp.int32, (tq, tk), 0)
                                 - lax.broadcasted_iota(jnp.int32, (tq, tk), 1))
    bias = (-slope_ref[h]) * jnp.abs(rel).astype(F32)
    s = s + jnp.concatenate([bias, bias], axis=0)
    _online_softmax_step(s, v_ref[...], m_ref, l_ref, acc_ref)

    @pl.when(ki == n_k - 1)
    def _():
        o = acc_ref[...] / l_ref[...]
        lam = (jnp.exp(jnp.sum(lq1_ref[...] * lk1_ref[...], axis=-1, keepdims=True))
               - jnp.exp(jnp.sum(lq2_ref[...] * lk2_ref[...], axis=-1, keepdims=True)) + lambda_init)
        d = o[:tq] - lam * o[tq:]
        o_ref[...] = (_rmsnorm(d, sg_ref[...], DIFF_SUBLN_EPS) * (1.0 - lambda_init)).astype(o_ref.dtype)


def _diff_attention(qkv, lq1, lk1, lq2, lk2, subln_g, layer_idx):
    L = qkv.shape[0]
    tq = _tile(L, 256)
    tk = _tile(L, 512)
    n_k = L // tk
    lambda_init = 0.8 - 0.6 * math.exp(-0.3 * layer_idx)
    slopes = jnp.asarray([2.0 ** (-8.0 * (i + 1) / DIFF_HEADS) for i in range(DIFF_HEADS)], dtype=F32)
    vec = lambda: pl.BlockSpec((1, DIFF_HEAD), lambda h, i, j: (0, 0))
    return pl.pallas_call(
        functools.partial(_diff_kernel, n_k=n_k, tq=tq, tk=tk, lambda_init=lambda_init),
        grid=(DIFF_HEADS, L // tq, n_k),
        in_specs=[
            pl.BlockSpec(memory_space=pltpu.SMEM),
            pl.BlockSpec((tq, DIFF_VHEAD), lambda h, i, j: (i, h)),
            pl.BlockSpec((tk, DIFF_VHEAD), lambda h, i, j: (j, DIFF_HEADS + h)),
            pl.BlockSpec((tk, DIFF_VHEAD), lambda h, i, j: (j, 2 * DIFF_HEADS + h)),
            vec(), vec(), vec(), vec(),
            pl.BlockSpec((1, DIFF_VHEAD), lambda h, i, j: (0, 0)),
        ],
        out_specs=pl.BlockSpec((tq, DIFF_VHEAD), lambda h, i, j: (i, h)),
        out_shape=jax.ShapeDtypeStruct((L, DIFF_WIDTH), BF16),
        scratch_shapes=[
            pltpu.VMEM((2 * tq, 1), F32),
            pltpu.VMEM((2 * tq, 1), F32),
            pltpu.VMEM((2 * tq, DIFF_VHEAD), F32),
        ],
        compiler_params=_cparams("parallel", "parallel", "arbitrary"),
        name="diff_attention",
    )(slopes, qkv, qkv, qkv, lq1.reshape(1, -1), lk1.reshape(1, -1), lq2.reshape(1, -1), lk2.reshape(1, -1),
      subln_g.reshape(1, -1))


def _cross_kernel(q_ref, kv_ref, o_ref, *, scale):
    for h in range(X_HEADS):
        sl = slice(h * X_HEAD, (h + 1) * X_HEAD)
        vsl = slice(D_MODEL + h * X_HEAD, D_MODEL + (h + 1) * X_HEAD)
        s = lax.dot_general(q_ref[:, sl], kv_ref[:, sl], (((1,), (1,)), ((), ())),
                            preferred_element_type=F32) * scale
        e = jnp.exp(s - jnp.max(s, axis=-1, keepdims=True))
        p = e / jnp.sum(e, axis=-1, keepdims=True)
        o_ref[:, sl] = jnp.dot(p.astype(BF16), kv_ref[:, vsl], preferred_element_type=F32).astype(o_ref.dtype)


def _cross_attention(q, kv):
    L = q.shape[0]
    n_mem = kv.shape[0]
    tq = _tile(L, 512)
    return pl.pallas_call(
        functools.partial(_cross_kernel, scale=X_HEAD ** -0.5),
        grid=(L // tq,),
        in_specs=[
            pl.BlockSpec((tq, D_MODEL), lambda i: (i, 0)),
            pl.BlockSpec((n_mem, 2 * D_MODEL), lambda i: (0, 0)),
        ],
        out_specs=pl.BlockSpec((tq, D_MODEL), lambda i: (i, 0)),
        out_shape=jax.ShapeDtypeStruct((L, D_MODEL), BF16),
        compiler_params=_cparams("parallel"),
        name="cross_attention",
    )(q, kv)


def _s5_weights(lam_re, lam_im, log_dt, b_re, b_im, c_re, c_im):
    T = S5_CHUNK
    hi = lax.Precision.HIGHEST
    lr = jnp.minimum(lam_re, S5_LAMBDA_RE_MAX)
    li = lam_im
    dt = jnp.exp(log_dt)[..., None]
    mag = jnp.exp(lr * dt)
    ab_re = mag * jnp.cos(li * dt)
    ab_im = mag * jnp.sin(li * dt)
    nr = ab_re - 1.0
    den = lr * lr + li * li
    f_re = ((nr * lr + ab_im * li) / den)[..., None]
    f_im = ((ab_im * lr - nr * li) / den)[..., None]
    bb_re = f_re * b_re - f_im * b_im
    bb_im = f_re * b_im + f_im * b_re
    k = jnp.arange(T + 1, dtype=F32)[:, None, None, None]
    pmag = jnp.exp(lr * dt * k)
    pw_re = pmag * jnp.cos(li * dt * k)
    pw_im = pmag * jnp.sin(li * dt * k)
    ca_re = c_re * pw_re[:, :, :, None, :] - c_im * pw_im[:, :, :, None, :]
    ca_im = c_re * pw_im[:, :, :, None, :] + c_im * pw_re[:, :, :, None, :]
    eye = jnp.eye(S5_LANE_GROUPS, dtype=F32)
    nb, ng = S5_LANE_BLOCKS, S5_LANE_GROUPS

    kern = (jnp.einsum('kdghp,dgpi->kdghi', ca_re[:T], bb_re, precision=hi)
            - jnp.einsum('kdghp,dgpi->kdghi', ca_im[:T], bb_im, precision=hi))
    ktab = jnp.concatenate([kern[1:, 1][::-1], (kern[0, 0] + kern[0, 1])[None], kern[1:, 0]], axis=0)
    idx = jnp.arange(T)[None, :] - jnp.arange(T)[:, None] + (T - 1)
    a6 = ktab[idx].reshape(T, T, nb, ng, S5_GROUP, S5_GROUP)
    toep = jnp.einsum('stcgoi,gk->csgitko', a6, eye).reshape(nb, S5_BLOCK_COLS, S5_BLOCK_COLS)

    e_re = jnp.stack([pw_re[:T, 0][::-1], pw_re[:T, 1]], axis=1)[..., None]
    e_im = jnp.stack([pw_im[:T, 0][::-1], pw_im[:T, 1]], axis=1)[..., None]
    zb = jnp.stack([e_re * bb_re - e_im * bb_im, e_re * bb_im + e_im * bb_re], axis=2)
    z7 = zb.reshape(T, 2, 2, nb, ng, S5_STATE, S5_GROUP)
    w_in = jnp.einsum('sdrcgpi,gk->csgidrkp', z7, eye).reshape(nb, S5_BLOCK_COLS, 4 * S5_BLOCK_STATE)

    cr = jnp.stack([ca_re[1:, 0], ca_re[1:, 1][::-1]], axis=1)
    ci = jnp.stack([ca_im[1:, 0], ca_im[1:, 1][::-1]], axis=1)
    c7 = jnp.stack([cr, -ci], axis=2).reshape(T, 2, 2, nb, ng, S5_GROUP, S5_STATE)
    w_out = jnp.einsum('tdrcgop,gk->cdrgptko', c7, eye).reshape(nb, 4 * S5_BLOCK_STATE, S5_BLOCK_COLS)

    a4 = jnp.stack([pw_re[T], pw_im[T]], axis=1).reshape(2, 2, nb, ng, S5_STATE)
    a_t = a4.transpose(2, 0, 1, 3, 4).reshape(nb, 1, 4 * S5_BLOCK_STATE)
    return toep.astype(BF16), w_in.astype(BF16), w_out.astype(BF16), a_t


def _s5_in_kernel(u_ref, w_ref, z_ref):
    z_ref[0] = jnp.dot(u_ref[0], w_ref[0], preferred_element_type=F32)


def _s5_out_kernel(u_ref, toep_ref, x_ref, w_ref, y_ref):
    y_ref[0] = (jnp.dot(u_ref[0], toep_ref[0], preferred_element_type=F32)
                + jnp.dot(x_ref[0].astype(BF16), w_ref[0], preferred_element_type=F32))


def _s5_scan_kernel(z_ref, a_ref, x_ref, *, n_rows):
    n_s = S5_BLOCK_STATE
    a = a_ref[0]
    af_re, af_im, ab_re, ab_im = (a[:, i * n_s:(i + 1) * n_s] for i in range(4))

    def body(n, carry):
        f_re, f_im, b_re, b_im = carry
        rf = pl.ds(n, 1)
        rb = pl.ds(n_rows - 1 - n, 1)
        x_ref[0, rf, 0 * n_s:1 * n_s] = f_re
        x_ref[0, rf, 1 * n_s:2 * n_s] = f_im
        x_ref[0, rb, 2 * n_s:3 * n_s] = b_re
        x_ref[0, rb, 3 * n_s:4 * n_s] = b_im
        zf_re = z_ref[0, rf, 0 * n_s:1 * n_s]
        zf_im = z_ref[0, rf, 1 * n_s:2 * n_s]
        zb_re = z_ref[0, rb, 2 * n_s:3 * n_s]
        zb_im = z_ref[0, rb, 3 * n_s:4 * n_s]
        return (af_re * f_re - af_im * f_im + zf_re, af_re * f_im + af_im * f_re + zf_im,
                ab_re * b_re - ab_im * b_im + zb_re, ab_re * b_im + ab_im * b_re + zb_im)

    zero = jnp.zeros((1, n_s), F32)
    lax.fori_loop(0, n_rows, body, (zero, zero, zero, zero))


def _s5_glu_kernel(y_ref, u_ref, d_ref, w_ref, b_ref, o_ref):
    y = d_ref[...] * u_ref[...] + y_ref[...]
    g = _gelu_tanh(y)
    gate = jnp.dot(g.astype(BF16), w_ref[...], preferred_element_type=F32) + b_ref[...]
    o_ref[...] = (g * _sigmoid(gate)).astype(o_ref.dtype)


def _s5_mixer(u, s5w, d_skip, glu_w, glu_b):
    L = u.shape[0]
    toep, w_in, w_out, a_t = s5w
    nb = S5_LANE_BLOCKS
    n_rows = L // S5_CHUNK
    n_st = 4 * S5_BLOCK_STATE
    tr = _tile(n_rows, 512)
    u_blk = u.reshape(n_rows, S5_CHUNK, nb, LANES).transpose(2, 0, 1, 3).reshape(nb, n_rows, S5_BLOCK_COLS)
    u_blk = u_blk.astype(BF16)
    blk = lambda cols: pl.BlockSpec((1, tr, cols), lambda c, i: (c, i, 0))
    wgt = lambda rows, cols: pl.BlockSpec((1, rows, cols), lambda c, i: (c, 0, 0))
    z = pl.pallas_call(
        _s5_in_kernel,
        grid=(nb, n_rows // tr),
        in_specs=[blk(S5_BLOCK_COLS), wgt(S5_BLOCK_COLS, n_st)],
        out_specs=blk(n_st),
        out_shape=jax.ShapeDtypeStruct((nb, n_rows, n_st), F32),
        compiler_params=_cparams("parallel", "parallel"),
        name="s5_chunk_input",
    )(u_blk, w_in)
    x = pl.pallas_call(
        functools.partial(_s5_scan_kernel, n_rows=n_rows),
        grid=(nb,),
        in_specs=[pl.BlockSpec((1, n_rows, n_st), lambda c: (c, 0, 0)),
                  pl.BlockSpec((1, 1, n_st), lambda c: (c, 0, 0))],
        out_specs=pl.BlockSpec((1, n_rows, n_st), lambda c: (c, 0, 0)),
        out_shape=jax.ShapeDtypeStruct((nb, n_rows, n_st), F32),
        compiler_params=_cparams("parallel"),
        name="s5_chunk_scan",
    )(z, a_t)
    y_blk = pl.pallas_call(
        _s5_out_kernel,
        grid=(nb, n_rows // tr),
        in_specs=[blk(S5_BLOCK_COLS), wgt(S5_BLOCK_COLS, S5_BLOCK_COLS), blk(n_st), wgt(n_st, S5_BLOCK_COLS)],
        out_specs=blk(S5_BLOCK_COLS),
        out_shape=jax.ShapeDtypeStruct((nb, n_rows, S5_BLOCK_COLS), F32),
        compiler_params=_cparams("parallel", "parallel"),
        name="s5_chunk_output",
    )(u_blk, toep, x, w_out)
    y = y_blk.reshape(nb, n_rows, S5_CHUNK, LANES).transpose(1, 2, 0, 3).reshape(L, S5_WIDTH)
    tm = _tile(L, 512)
    row = lambda: pl.BlockSpec((tm, S5_WIDTH), lambda i: (i, 0))
    vec = lambda: pl.BlockSpec((1, S5_WIDTH), lambda i: (0, 0))
    return pl.pallas_call(
        _s5_glu_kernel,
        grid=(L // tm,),
        in_specs=[row(), row(), vec(), pl.BlockSpec((S5_WIDTH, S5_WIDTH), lambda i: (0, 0)), vec()],
        out_specs=row(),
        out_shape=jax.ShapeDtypeStruct((L, S5_WIDTH), BF16),
        compiler_params=_cparams("parallel"),
        name="s5_glu",
    )(y, u, d_skip.reshape(1, -1), glu_w, glu_b.reshape(1, -1))


def _rope_tables(L):
    rows = L // GRID_W
    r = jnp.repeat(jnp.arange(rows, dtype=F32), GRID_W)
    c = jnp.tile(jnp.arange(GRID_W, dtype=F32), rows)
    inv = ROPE_THETA ** (-jnp.arange(0, ROPE_AXIS, 2, dtype=F32) / ROPE_AXIS)
    ar = r[:, None] * inv
    ac = c[:, None] * inv
    cos = jnp.concatenate([jnp.cos(ar), jnp.cos(ar), jnp.cos(ac), jnp.cos(ac)], axis=-1)
    sin = jnp.concatenate([-jnp.sin(ar), jnp.sin(ar), -jnp.sin(ac), jnp.sin(ac)], axis=-1)
    return cos, sin


def _encoder(x, mem, p):
    L = x.shape[0]
    cos, sin = _rope_tables(L)
    for l in range(DEPTH):
        x = _ffn(x, p['ffn1_norm'][l], p['ffn1_w_gu'][l], p['ffn1_w_down'][l])
        if l % 2 == 0:
            e = l // 2
            u = _norm_matmul(x, p['mix_norm'][l], p['even_w_in_u'][e], F32)
            qkv = _norm_matmul(x, p['mix_norm'][l], p['even_w_in_qkv'][e], BF16)
            ya = _s5_mixer(u, p['s5'][e], p['s5_d'][e], p['s5_glu_w'][e], p['s5_glu_b'][e])
            yb = _diff_attention(qkv, p['diff_lambda_q1'][e], p['diff_lambda_k1'][e], p['diff_lambda_q2'][e],
                                 p['diff_lambda_k2'][e], p['diff_subln'][e], l)
            x = _matmul_residual(jnp.concatenate([ya, yb], axis=-1), p['even_w_out'][e], x)
        else:
            o = l // 2
            qkv = _odd_proj(x, p['mix_norm'][l], p['odd_w_in'][o], p['gqa_q_norm'][o], p['gqa_k_norm'][o], cos, sin)
            x = _matmul_residual(_gqa_attention(qkv), p['odd_w_out'][o], x)
        q = _norm_matmul(x, p['cross_norm'][l], p['cross_w_q'][l], BF16)
        kv = _norm_matmul(mem, p['mem_norm'][l], p['cross_w_kv'][l], BF16)
        x = _matmul_residual(_cross_attention(q, kv), p['cross_w_o'][l], x)
        x = _ffn(x, p['ffn2_norm'][l], p['ffn2_w_gu'][l], p['ffn2_w_down'][l],
                 final_g=p['final_norm'] if l == DEPTH - 1 else None)
    return x


def kernel(x_prompt, x_sample, mem_prompt, mem_sample, ffn1_norm, ffn1_w_gu, ffn1_w_down, mix_norm, even_w_in, even_w_out, s5_lambda_re, s5_lambda_im, s5_log_dt, s5_b_re, s5_b_im, s5_c_re, s5_c_im, s5_d, s5_glu_w, s5_glu_b, diff_lambda_q1, diff_lambda_k1, diff_lambda_q2, diff_lambda_k2, diff_subln, odd_w_in, odd_w_out, gqa_q_norm, gqa_k_norm, cross_norm, mem_norm, cross_w_q, cross_w_kv, cross_w_o, ffn2_norm, ffn2_w_gu, ffn2_w_down, final_norm):
    bf = lambda w: w.astype(BF16)
    q_scale = jnp.concatenate([jnp.full((DIFF_WIDTH,), DIFF_HEAD ** -0.5, F32), jnp.ones((2 * DIFF_WIDTH,), F32)])
    p = dict(
        ffn1_norm=ffn1_norm, ffn1_w_gu=bf(ffn1_w_gu), ffn1_w_down=bf(ffn1_w_down), mix_norm=mix_norm,
        even_w_in_u=bf(even_w_in[:, :, :S5_WIDTH]), even_w_in_qkv=bf(even_w_in[:, :, S5_WIDTH:] * q_scale),
        even_w_out=bf(even_w_out),
        s5=[_s5_weights(s5_lambda_re[e], s5_lambda_im[e], s5_log_dt[e], s5_b_re[e], s5_b_im[e], s5_c_re[e], s5_c_im[e])
            for e in range(s5_lambda_re.shape[0])],
        s5_d=s5_d, s5_glu_w=bf(s5_glu_w), s5_glu_b=s5_glu_b,
        diff_lambda_q1=diff_lambda_q1, diff_lambda_k1=diff_lambda_k1, diff_lambda_q2=diff_lambda_q2,
        diff_lambda_k2=diff_lambda_k2, diff_subln=diff_subln,
        odd_w_in=bf(odd_w_in), odd_w_out=bf(odd_w_out), gqa_q_norm=gqa_q_norm, gqa_k_norm=gqa_k_norm,
        cross_norm=cross_norm, mem_norm=mem_norm, cross_w_q=bf(cross_w_q), cross_w_kv=bf(cross_w_kv),
        cross_w_o=bf(cross_w_o), ffn2_norm=ffn2_norm, ffn2_w_gu=bf(ffn2_w_gu), ffn2_w_down=bf(ffn2_w_down),
        final_norm=final_norm)
    outs = []
    for x, mem in ((x_prompt, mem_prompt), (x_sample, mem_sample)):
        outs.append(jnp.stack([_encoder(x[b], mem[b], p) for b in range(x.shape[0])]))
    return tuple(outs)
```

```python
import functools
import math

import jax
import jax.numpy as jnp
from jax import lax
from jax.experimental import pallas as pl
from jax.experimental.pallas import tpu as pltpu

F32 = jnp.float32
BF16 = jnp.bfloat16

D_MODEL = 2048
DEPTH = 4
EPS = 1e-6
D_FF = 5632
GRID_W = 64

S5_WIDTH = D_MODEL // 2
S5_GROUP = 16
S5_GROUPS = S5_WIDTH // S5_GROUP
S5_STATE = 64
S5_LAMBDA_RE_MAX = -1e-4

DIFF_WIDTH = D_MODEL - S5_WIDTH
DIFF_HEAD = 64
DIFF_HEADS = DIFF_WIDTH // (2 * DIFF_HEAD)
DIFF_VHEAD = 2 * DIFF_HEAD
DIFF_SUBLN_EPS = 1e-5

GQA_HEAD = 128
GQA_HEADS = D_MODEL // GQA_HEAD
GQA_KV_HEADS = 4
GQA_GROUP = GQA_HEADS // GQA_KV_HEADS
ROPE_AXIS = GQA_HEAD // 2
ROPE_THETA = 10000.0

X_HEADS = 4
X_HEAD = D_MODEL // X_HEADS

LOG2E = math.log2(math.e)

GQA_TQ = 512
DIFF_TQ = 1024
ATTN_TK = 1024
ATTN_ROWS = 256
ATTN_KEYS = 256

LANES = 128
V7X_VMEM_BYTES = 64 * 1024 * 1024
VMEM_LIMIT_BYTES = V7X_VMEM_BYTES - 8 * 1024 * 1024

S5_CHUNK = 16
S5_LANE_GROUPS = LANES // S5_GROUP
S5_LANE_BLOCKS = S5_WIDTH // LANES
S5_BLOCK_COLS = S5_CHUNK * LANES
S5_BLOCK_STATE = S5_LANE_GROUPS * S5_STATE


def _tile(n, pref):
    t = min(pref, n)
    while n % t:
        t //= 2
    return t


def _cparams(*sem):
    return pltpu.CompilerParams(dimension_semantics=sem, vmem_limit_bytes=VMEM_LIMIT_BYTES)


def _rmsnorm(x, g, eps):
    ms = jnp.mean(x * x, axis=-1, keepdims=True)
    return x * lax.rsqrt(ms + eps) * g


def _sigmoid(x):
    return 1.0 / (1.0 + jnp.exp(-x))


def _gelu_tanh(x):
    c = math.sqrt(2.0 / math.pi)
    return x * (0.5 * (1.0 + jnp.tanh(c * (x + 0.044715 * (x * x * x)))))


def _ffn_kernel(x_ref, g_ref, wg_ref, wu_ref, wd_ref, *rest, n_f, final):
    if final:
        fg_ref, o_ref, h_ref, acc_ref = rest
    else:
        o_ref, h_ref, acc_ref = rest
    j = pl.program_id(1)

    @pl.when(j == 0)
    def _():
        h_ref[...] = _rmsnorm(x_ref[...], g_ref[...], EPS).astype(BF16)
        acc_ref[...] = jnp.zeros_like(acc_ref)

    h = h_ref[...]
    gate = jnp.dot(h, wg_ref[...], preferred_element_type=F32)
    up = jnp.dot(h, wu_ref[...], preferred_element_type=F32)
    act = gate * _sigmoid(gate) * up
    acc_ref[...] += jnp.dot(act.astype(BF16), wd_ref[...], preferred_element_type=F32)

    @pl.when(j == n_f - 1)
    def _():
        y = x_ref[...] + 0.5 * acc_ref[...]
        if final:
            y = _rmsnorm(y, fg_ref[...], EPS)
        o_ref[...] = y


def _ffn(x, g, w_gu, w_down, final_g=None):
    L = x.shape[0]
    tm = _tile(L, 512)
    tf = 512
    n_f = D_FF // tf
    final = final_g is not None
    in_specs = [
        pl.BlockSpec((tm, D_MODEL), lambda i, j: (i, 0)),
        pl.BlockSpec((1, D_MODEL), lambda i, j: (0, 0)),
        pl.BlockSpec((D_MODEL, tf), lambda i, j: (0, j)),
        pl.BlockSpec((D_MODEL, tf), lambda i, j: (0, j + n_f)),
        pl.BlockSpec((tf, D_MODEL), lambda i, j: (j, 0)),
    ]
    args = [x, g.reshape(1, D_MODEL), w_gu, w_gu, w_down]
    if final:
        in_specs.append(pl.BlockSpec((1, D_MODEL), lambda i, j: (0, 0)))
        args.append(final_g.reshape(1, D_MODEL))
    return pl.pallas_call(
        functools.partial(_ffn_kernel, n_f=n_f, final=final),
        grid=(L // tm, n_f),
        in_specs=in_specs,
        out_specs=pl.BlockSpec((tm, D_MODEL), lambda i, j: (i, 0)),
        out_shape=jax.ShapeDtypeStruct((L, D_MODEL), F32),
        scratch_shapes=[pltpu.VMEM((tm, D_MODEL), BF16), pltpu.VMEM((tm, D_MODEL), F32)],
        compiler_params=_cparams("parallel", "arbitrary"),
        name="ffn_final" if final else "ffn",
    )(*args)


def _norm_matmul_kernel(x_ref, g_ref, w_ref, *rest, scaled):
    if scaled:
        c_ref, o_ref, h_ref = rest
    else:
        o_ref, h_ref = rest

    @pl.when(pl.program_id(1) == 0)
    def _():
        h_ref[...] = _rmsnorm(x_ref[...], g_ref[...], EPS).astype(BF16)

    z = jnp.dot(h_ref[...], w_ref[...], preferred_element_type=F32)
    if scaled:
        z = z * c_ref[...]
    o_ref[...] = z.astype(o_ref.dtype)


def _norm_matmul(x, g, w, out_dtype, col_scale=None):
    L = x.shape[0]
    N = w.shape[1]
    tm = _tile(L, 512)
    tn = _tile(N, 512)
    scaled = col_scale is not None
    in_specs = [
        pl.BlockSpec((tm, D_MODEL), lambda i, j: (i, 0)),
        pl.BlockSpec((1, D_MODEL), lambda i, j: (0, 0)),
        pl.BlockSpec((D_MODEL, tn), lambda i, j: (0, j)),
    ]
    args = [x, g.reshape(1, D_MODEL), w]
    if scaled:
        in_specs.append(pl.BlockSpec((1, tn), lambda i, j: (0, j)))
        args.append(col_scale.reshape(1, N))
    return pl.pallas_call(
        functools.partial(_norm_matmul_kernel, scaled=scaled),
        grid=(L // tm, N // tn),
        in_specs=in_specs,
        out_specs=pl.BlockSpec((tm, tn), lambda i, j: (i, j)),
        out_shape=jax.ShapeDtypeStruct((L, N), out_dtype),
        scratch_shapes=[pltpu.VMEM((tm, D_MODEL), BF16)],
        compiler_params=_cparams("parallel", "arbitrary"),
        name="norm_matmul",
    )(*args)


def _odd_proj_kernel(x_ref, g_ref, w_ref, qg_ref, kg_ref, cos_ref, sin_ref, o_ref, h_ref, *, heads_per_tile):
    j = pl.program_id(1)
    n_q_tiles = GQA_HEADS // heads_per_tile
    n_k_tiles = GQA_KV_HEADS // heads_per_tile

    @pl.when(j == 0)
    def _():
        h_ref[...] = _rmsnorm(x_ref[...], g_ref[...], EPS).astype(BF16)

    z = jnp.dot(h_ref[...], w_ref[...], preferred_element_type=F32)

    @pl.when(j < n_q_tiles + n_k_tiles)
    def _():
        gain = jnp.where(j < n_q_tiles, qg_ref[...], kg_ref[...])
        out_scale = jnp.where(j < n_q_tiles, GQA_HEAD ** -0.5 * LOG2E, 1.0)
        cos = cos_ref[...] * out_scale
        sin = sin_ref[...] * out_scale
        lane = lax.broadcasted_iota(jnp.int32, (1, GQA_HEAD), 1)
        first_half = (lane & (ROPE_AXIS - 1)) < (ROPE_AXIS // 2)
        for hh in range(heads_per_tile):
            sl = slice(hh * GQA_HEAD, (hh + 1) * GQA_HEAD)
            y = _rmsnorm(z[:, sl], gain, EPS)
            partner = jnp.where(first_half, pltpu.roll(y, GQA_HEAD - ROPE_AXIS // 2, 1),
                                pltpu.roll(y, ROPE_AXIS // 2, 1))
            o_ref[:, sl] = (y * cos + partner * sin).astype(o_ref.dtype)

    @pl.when(j >= n_q_tiles + n_k_tiles)
    def _():
        o_ref[...] = z.astype(o_ref.dtype)


def _odd_proj(x, g, w, q_g, k_g, cos, sin):
    L = x.shape[0]
    N = w.shape[1]
    tm = _tile(L, 512)
    heads_per_tile = 4
    tn = heads_per_tile * GQA_HEAD
    return pl.pallas_call(
        functools.partial(_odd_proj_kernel, heads_per_tile=heads_per_tile),
        grid=(L // tm, N // tn),
        in_specs=[
            pl.BlockSpec((tm, D_MODEL), lambda i, j: (i, 0)),
            pl.BlockSpec((1, D_MODEL), lambda i, j: (0, 0)),
            pl.BlockSpec((D_MODEL, tn), lambda i, j: (0, j)),
            pl.BlockSpec((1, GQA_HEAD), lambda i, j: (0, 0)),
            pl.BlockSpec((1, GQA_HEAD), lambda i, j: (0, 0)),
            pl.BlockSpec((tm, GQA_HEAD), lambda i, j: (i, 0)),
            pl.BlockSpec((tm, GQA_HEAD), lambda i, j: (i, 0)),
        ],
        out_specs=pl.BlockSpec((tm, tn), lambda i, j: (i, j)),
        out_shape=jax.ShapeDtypeStruct((L, N), BF16),
        scratch_shapes=[pltpu.VMEM((tm, D_MODEL), BF16)],
        compiler_params=_cparams("parallel", "arbitrary"),
        name="odd_proj",
    )(x, g.reshape(1, D_MODEL), w, q_g.reshape(1, GQA_HEAD), k_g.reshape(1, GQA_HEAD), cos, sin)


def _matmul_residual_kernel(a_ref, w_ref, r_ref, o_ref):
    o_ref[...] = r_ref[...] + jnp.dot(a_ref[...], w_ref[...], preferred_element_type=F32)


def _matmul_residual(a, w, res):
    L, K = a.shape
    N = w.shape[1]
    tm = _tile(L, 512)
    return pl.pallas_call(
        _matmul_residual_kernel,
        grid=(L // tm,),
        in_specs=[
            pl.BlockSpec((tm, K), lambda i: (i, 0)),
            pl.BlockSpec((K, N), lambda i: (0, 0)),
            pl.BlockSpec((tm, N), lambda i: (i, 0)),
        ],
        out_specs=pl.BlockSpec((tm, N), lambda i: (i, 0)),
        out_shape=jax.ShapeDtypeStruct((L, N), F32),
        compiler_params=_cparams("parallel"),
        name="matmul_residual",
    )(a, w, res)


def _lane_tile(x, n):
    return jnp.concatenate([x] * n, axis=1) if n > 1 else x


def _softmax_block(t, shift, v, m, l, acc):
    row_max = jnp.max(t, axis=1, keepdims=True)
    m_new = jnp.maximum(m, row_max if shift is None else row_max + shift)
    alpha = jnp.exp2(m - m_new)
    p = jnp.exp2(t - _lane_tile(m_new if shift is None else m_new - shift, t.shape[1] // LANES))
    l = alpha * l + jnp.sum(p, axis=1, keepdims=True)
    acc = alpha * acc + jnp.dot(p.astype(v.dtype), v, preferred_element_type=F32)
    return m_new, l, acc


def _softmax_init(m_ref, l_ref, acc_ref):
    m_ref[...] = jnp.full_like(m_ref, -jnp.inf)
    l_ref[...] = jnp.zeros_like(l_ref)
    acc_ref[...] = jnp.zeros_like(acc_ref)


_NT = (((1,), (1,)), ((), ()))


def _gqa_kernel(q_ref, k_ref, v_ref, o_ref, m_ref, l_ref, acc_ref, *, n_k, tq, tk, rows, keys):
    ki = pl.program_id(2)

    @pl.when(ki == 0)
    def _():
        _softmax_init(m_ref, l_ref, acc_ref)

    for g in range(GQA_GROUP):
        for c in range(tq // rows):
            sl = slice(g * tq + c * rows, g * tq + (c + 1) * rows)
            q = q_ref[c * rows:(c + 1) * rows, g * GQA_HEAD:(g + 1) * GQA_HEAD]
            state = (m_ref[sl, :], l_ref[sl, :], acc_ref[sl, :])
            for kk in range(tk // keys):
                ksl = slice(kk * keys, (kk + 1) * keys)
                s = lax.dot_general(q, k_ref[ksl, :], _NT, preferred_element_type=F32)
                state = _softmax_block(s, None, v_ref[ksl, :], *state)
            m_ref[sl, :], l_ref[sl, :], acc_ref[sl, :] = state

    @pl.when(ki == n_k - 1)
    def _():
        for g in range(GQA_GROUP):
            sl = slice(g * tq, (g + 1) * tq)
            o_ref[:, g * GQA_HEAD:(g + 1) * GQA_HEAD] = (acc_ref[sl, :] / l_ref[sl, :]).astype(o_ref.dtype)


def _gqa_attention(qkv):
    L = qkv.shape[0]
    tq = _tile(L, GQA_TQ)
    tk = _tile(L, ATTN_TK)
    n_k = L // tk
    group_cols = GQA_GROUP * GQA_HEAD
    stat = lambda: pltpu.VMEM((GQA_GROUP * tq, GQA_HEAD), F32)
    return pl.pallas_call(
        functools.partial(_gqa_kernel, n_k=n_k, tq=tq, tk=tk, rows=_tile(tq, ATTN_ROWS), keys=_tile(tk, ATTN_KEYS)),
        grid=(GQA_KV_HEADS, L // tq, n_k),
        in_specs=[
            pl.BlockSpec((tq, group_cols), lambda h, i, j: (i, h)),
            pl.BlockSpec((tk, GQA_HEAD), lambda h, i, j: (j, GQA_HEADS + h)),
            pl.BlockSpec((tk, GQA_HEAD), lambda h, i, j: (j, GQA_HEADS + GQA_KV_HEADS + h)),
        ],
        out_specs=pl.BlockSpec((tq, group_cols), lambda h, i, j: (i, h)),
        out_shape=jax.ShapeDtypeStruct((L, D_MODEL), BF16),
        scratch_shapes=[stat(), stat(), stat()],
        compiler_params=_cparams("parallel", "parallel", "arbitrary"),
        name="gqa_attention",
    )(qkv, qkv, qkv)


def _diff_kernel(slope_ref, q_ref, k_ref, v_ref, lq1_ref, lk1_ref, lq2_ref, lk2_ref, sg_ref, o_ref,
                 qs_ref, m_ref, l_ref, acc_ref, *, n_k, tq, tk, rows, keys, lambda_init):
    h = pl.program_id(0)
    q0 = pl.program_id(1) * tq
    ki = pl.program_id(2)
    k0 = ki * tk
    n_c = tq // rows

    @pl.when(ki == 0)
    def _():
        _softmax_init(m_ref, l_ref, acc_ref)
        q = q_ref[...]
        lane = lax.broadcasted_iota(jnp.int32, (1, 2 * DIFF_HEAD), 1)
        zero = jnp.zeros_like(q)
        q1 = jnp.where(lane < DIFF_HEAD, q, zero)
        q2 = jnp.where(lane >= DIFF_HEAD, q, zero)
        for c in range(n_c):
            qs_ref[2 * c * rows:(2 * c + 1) * rows, :] = q1[c * rows:(c + 1) * rows]
            qs_ref[(2 * c + 1) * rows:(2 * c + 2) * rows, :] = q2[c * rows:(c + 1) * rows]

    slope2 = slope_ref[h] * LOG2E
    d0 = lax.broadcasted_iota(jnp.int32, (rows, keys), 0) - lax.broadcasted_iota(jnp.int32, (rows, keys), 1)

    def step(bias_fn):
        for c in range(n_c):
            sl = slice(2 * c * rows, 2 * (c + 1) * rows)
            state = (m_ref[sl, :], l_ref[sl, :], acc_ref[sl, :])
            for kk in range(tk // keys):
                ksl = slice(kk * keys, (kk + 1) * keys)
                bias, shift = bias_fn(q0 + c * rows - (k0 + kk * keys))
                t = lax.dot_general(qs_ref[sl, :], k_ref[ksl, :], _NT, preferred_element_type=F32)
                t = t + jnp.concatenate([bias, bias], axis=0)
                state = _softmax_block(t, shift, v_ref[ksl, :], *state)
            m_ref[sl, :], l_ref[sl, :], acc_ref[sl, :] = state

    crosses_diagonal = jnp.logical_and(k0 < q0 + tq, q0 < k0 + tk)

    @pl.when(crosses_diagonal)
    def _():
        step(lambda off: ((-slope2) * jnp.abs(off + d0).astype(F32), None))

    @pl.when(jnp.logical_not(crosses_diagonal))
    def _():
        coef = jnp.where(q0 >= k0, -slope2, slope2)
        tile = coef * d0.astype(F32)
        step(lambda off: (tile, coef * off.astype(F32)))

    @pl.when(ki == n_k - 1)
    def _():
        lam = (jnp.exp(jnp.sum(lq1_ref[...] * lk1_ref[...], axis=-1, keepdims=True))
               - jnp.exp(jnp.sum(lq2_ref[...] * lk2_ref[...], axis=-1, keepdims=True)) + lambda_init)
        for c in range(n_c):
            s1 = slice(2 * c * rows, (2 * c + 1) * rows)
            s2 = slice((2 * c + 1) * rows, (2 * c + 2) * rows)
            d = acc_ref[s1, :] / l_ref[s1, :] - lam * (acc_ref[s2, :] / l_ref[s2, :])
            o_ref[c * rows:(c + 1) * rows, :] = (
                _rmsnorm(d, sg_ref[...], DIFF_SUBLN_EPS) * (1.0 - lambda_init)).astype(o_ref.dtype)


def _diff_attention(qkv, lq1, lk1, lq2, lk2, subln_g, layer_idx):
    L = qkv.shape[0]
    tq = _tile(L, DIFF_TQ)
    tk = _tile(L, ATTN_TK)
    n_k = L // tk
    lambda_init = 0.8 - 0.6 * math.exp(-0.3 * layer_idx)
    slopes = jnp.asarray([2.0 ** (-8.0 * (i + 1) / DIFF_HEADS) for i in range(DIFF_HEADS)], dtype=F32)
    vec = lambda: pl.BlockSpec((1, DIFF_HEAD), lambda h, i, j: (0, 0))
    stat = lambda: pltpu.VMEM((2 * tq, DIFF_VHEAD), F32)
    return pl.pallas_call(
        functools.partial(_diff_kernel, n_k=n_k, tq=tq, tk=tk, rows=_tile(tq, ATTN_ROWS), keys=_tile(tk, ATTN_KEYS),
                          lambda_init=lambda_init),
        grid=(DIFF_HEADS, L // tq, n_k),
        in_specs=[
            pl.BlockSpec(memory_space=pltpu.SMEM),
            pl.BlockSpec((tq, DIFF_VHEAD), lambda h, i, j: (i, h)),
            pl.BlockSpec((tk, DIFF_VHEAD), lambda h, i, j: (j, DIFF_HEADS + h)),
            pl.BlockSpec((tk, DIFF_VHEAD), lambda h, i, j: (j, 2 * DIFF_HEADS + h)),
            vec(), vec(), vec(), vec(),
            pl.BlockSpec((1, DIFF_VHEAD), lambda h, i, j: (0, 0)),
        ],
        out_specs=pl.BlockSpec((tq, DIFF_VHEAD), lambda h, i, j: (i, h)),
        out_shape=jax.ShapeDtypeStruct((L, DIFF_WIDTH), BF16),
        scratch_shapes=[pltpu.VMEM((2 * tq, DIFF_VHEAD), BF16), stat(), stat(), stat()],
        compiler_params=_cparams("parallel", "parallel", "arbitrary"),
        name="diff_attention",
    )(slopes, qkv, qkv, qkv, lq1.reshape(1, -1), lk1.reshape(1, -1), lq2.reshape(1, -1), lk2.reshape(1, -1),
      subln_g.reshape(1, -1))


def _cross_kernel(q_ref, kv_ref, o_ref, *, scale):
    for h in range(X_HEADS):
        sl = slice(h * X_HEAD, (h + 1) * X_HEAD)
        vsl = slice(D_MODEL + h * X_HEAD, D_MODEL + (h + 1) * X_HEAD)
        s = lax.dot_general(q_ref[:, sl], kv_ref[:, sl], (((1,), (1,)), ((), ())),
                            preferred_element_type=F32) * scale
        e = jnp.exp(s - jnp.max(s, axis=-1, keepdims=True))
        p = e / jnp.sum(e, axis=-1, keepdims=True)
        o_ref[:, sl] = jnp.dot(p.astype(BF16), kv_ref[:, vsl], preferred_element_type=F32).astype(o_ref.dtype)


def _cross_attention(q, kv):
    L = q.shape[0]
    n_mem = kv.shape[0]
    tq = _tile(L, 512)
    return pl.pallas_call(
        functools.partial(_cross_kernel, scale=X_HEAD ** -0.5),
        grid=(L // tq,),
        in_specs=[
            pl.BlockSpec((tq, D_MODEL), lambda i: (i, 0)),
            pl.BlockSpec((n_mem, 2 * D_MODEL), lambda i: (0, 0)),
        ],
        out_specs=pl.BlockSpec((tq, D_MODEL), lambda i: (i, 0)),
        out_shape=jax.ShapeDtypeStruct((L, D_MODEL), BF16),
        compiler_params=_cparams("parallel"),
        name="cross_attention",
    )(q, kv)


def _s5_weights(lam_re, lam_im, log_dt, b_re, b_im, c_re, c_im):
    T = S5_CHUNK
    hi = lax.Precision.HIGHEST
    lr = jnp.minimum(lam_re, S5_LAMBDA_RE_MAX)
    li = lam_im
    dt = jnp.exp(log_dt)[..., None]
    mag = jnp.exp(lr * dt)
    ab_re = mag * jnp.cos(li * dt)
    ab_im = mag * jnp.sin(li * dt)
    nr = ab_re - 1.0
    den = lr * lr + li * li
    f_re = ((nr * lr + ab_im * li) / den)[..., None]
    f_im = ((ab_im * lr - nr * li) / den)[..., None]
    bb_re = f_re * b_re - f_im * b_im
    bb_im = f_re * b_im + f_im * b_re
    k = jnp.arange(T + 1, dtype=F32)[:, None, None, None]
    pmag = jnp.exp(lr * dt * k)
    pw_re = pmag * jnp.cos(li * dt * k)
    pw_im = pmag * jnp.sin(li * dt * k)
    ca_re = c_re * pw_re[:, :, :, None, :] - c_im * pw_im[:, :, :, None, :]
    ca_im = c_re * pw_im[:, :, :, None, :] + c_im * pw_re[:, :, :, None, :]
    eye = jnp.eye(S5_LANE_GROUPS, dtype=F32)
    nb, ng = S5_LANE_BLOCKS, S5_LANE_GROUPS

    kern = (jnp.einsum('kdghp,dgpi->kdghi', ca_re[:T], bb_re, precision=hi)
            - jnp.einsum('kdghp,dgpi->kdghi', ca_im[:T], bb_im, precision=hi))
    ktab = jnp.concatenate([kern[1:, 1][::-1], (kern[0, 0] + kern[0, 1])[None], kern[1:, 0]], axis=0)
    idx = jnp.arange(T)[None, :] - jnp.arange(T)[:, None] + (T - 1)
    a6 = ktab[idx].reshape(T, T, nb, ng, S5_GROUP, S5_GROUP)
    toep = jnp.einsum('stcgoi,gk->csgitko', a6, eye).reshape(nb, S5_BLOCK_COLS, S5_BLOCK_COLS)

    e_re = jnp.stack([pw_re[:T, 0][::-1], pw_re[:T, 1]], axis=1)[..., None]
    e_im = jnp.stack([pw_im[:T, 0][::-1], pw_im[:T, 1]], axis=1)[..., None]
    zb = jnp.stack([e_re * bb_re - e_im * bb_im, e_re * bb_im + e_im * bb_re], axis=2)
    z7 = zb.reshape(T, 2, 2, nb, ng, S5_STATE, S5_GROUP)
    w_in = jnp.einsum('sdrcgpi,gk->csgidrkp', z7, eye).reshape(nb, S5_BLOCK_COLS, 4 * S5_BLOCK_STATE)

    cr = jnp.stack([ca_re[1:, 0], ca_re[1:, 1][::-1]], axis=1)
    ci = jnp.stack([ca_im[1:, 0], ca_im[1:, 1][::-1]], axis=1)
    c7 = jnp.stack([cr, -ci], axis=2).reshape(T, 2, 2, nb, ng, S5_GROUP, S5_STATE)
    w_out = jnp.einsum('tdrcgop,gk->cdrgptko', c7, eye).reshape(nb, 4 * S5_BLOCK_STATE, S5_BLOCK_COLS)

    a4 = jnp.stack([pw_re[T], pw_im[T]], axis=1).reshape(2, 2, nb, ng, S5_STATE)
    a_t = a4.transpose(2, 0, 1, 3, 4).reshape(nb, 1, 4 * S5_BLOCK_STATE)
    return toep.astype(BF16), w_in.astype(BF16), w_out.astype(BF16), a_t


def _s5_in_kernel(u_ref, w_ref, z_ref):
    z_ref[0] = jnp.dot(u_ref[0], w_ref[0], preferred_element_type=F32)


def _s5_out_kernel(u_ref, toep_ref, x_ref, w_ref, y_ref):
    y_ref[0] = (jnp.dot(u_ref[0], toep_ref[0], preferred_element_type=F32)
                + jnp.dot(x_ref[0].astype(BF16), w_ref[0], preferred_element_type=F32))


def _s5_scan_kernel(z_ref, a_ref, x_ref, *, n_rows):
    n_s = S5_BLOCK_STATE
    a = a_ref[0]
    af_re, af_im, ab_re, ab_im = (a[:, i * n_s:(i + 1) * n_s] for i in range(4))

    def body(n, carry):
        f_re, f_im, b_re, b_im = carry
        rf = pl.ds(n, 1)
        rb = pl.ds(n_rows - 1 - n, 1)
        x_ref[0, rf, 0 * n_s:1 * n_s] = f_re
        x_ref[0, rf, 1 * n_s:2 * n_s] = f_im
        x_ref[0, rb, 2 * n_s:3 * n_s] = b_re
        x_ref[0, rb, 3 * n_s:4 * n_s] = b_im
        zf_re = z_ref[0, rf, 0 * n_s:1 * n_s]
        zf_im = z_ref[0, rf, 1 * n_s:2 * n_s]
        zb_re = z_ref[0, rb, 2 * n_s:3 * n_s]
        zb_im = z_ref[0, rb, 3 * n_s:4 * n_s]
        return (af_re * f_re - af_im * f_im + zf_re, af_re * f_im + af_im * f_re + zf_im,
                ab_re * b_re - ab_im * b_im + zb_re, ab_re * b_im + ab_im * b_re + zb_im)

    zero = jnp.zeros((1, n_s), F32)
    lax.fori_loop(0, n_rows, body, (zero, zero, zero, zero))


def _s5_glu_kernel(y_ref, u_ref, d_ref, w_ref, b_ref, o_ref):
    y = d_ref[...] * u_ref[...] + y_ref[...]
    g = _gelu_tanh(y)
    gate = jnp.dot(g.astype(BF16), w_ref[...], preferred_element_type=F32) + b_ref[...]
    o_ref[...] = (g * _sigmoid(gate)).astype(o_ref.dtype)


def _s5_mixer(u, s5w, d_skip, glu_w, glu_b):
    L = u.shape[0]
    toep, w_in, w_out, a_t = s5w
    nb = S5_LANE_BLOCKS
    n_rows = L // S5_CHUNK
    n_st = 4 * S5_BLOCK_STATE
    tr = _tile(n_rows, 512)
    u_blk = u.reshape(n_rows, S5_CHUNK, nb, LANES).transpose(2, 0, 1, 3).reshape(nb, n_rows, S5_BLOCK_COLS)
    u_blk = u_blk.astype(BF16)
    blk = lambda cols: pl.BlockSpec((1, tr, cols), lambda c, i: (c, i, 0))
    wgt = lambda rows, cols: pl.BlockSpec((1, rows, cols), lambda c, i: (c, 0, 0))
    z = pl.pallas_call(
        _s5_in_kernel,
        grid=(nb, n_rows // tr),
        in_specs=[blk(S5_BLOCK_COLS), wgt(S5_BLOCK_COLS, n_st)],
        out_specs=blk(n_st),
        out_shape=jax.ShapeDtypeStruct((nb, n_rows, n_st), F32),
        compiler_params=_cparams("parallel", "parallel"),
        name="s5_chunk_input",
    )(u_blk, w_in)
    x = pl.pallas_call(
        functools.partial(_s5_scan_kernel, n_rows=n_rows),
        grid=(nb,),
        in_specs=[pl.BlockSpec((1, n_rows, n_st), lambda c: (c, 0, 0)),
                  pl.BlockSpec((1, 1, n_st), lambda c: (c, 0, 0))],
        out_specs=pl.BlockSpec((1, n_rows, n_st), lambda c: (c, 0, 0)),
        out_shape=jax.ShapeDtypeStruct((nb, n_rows, n_st), F32),
        compiler_params=_cparams("parallel"),
        name="s5_chunk_scan",
    )(z, a_t)
    y_blk = pl.pallas_call(
        _s5_out_kernel,
        grid=(nb, n_rows // tr),
        in_specs=[blk(S5_BLOCK_COLS), wgt(S5_BLOCK_COLS, S5_BLOCK_COLS), blk(n_st), wgt(n_st, S5_BLOCK_COLS)],
        out_specs=blk(S5_BLOCK_COLS),
        out_shape=jax.ShapeDtypeStruct((nb, n_rows, S5_BLOCK_COLS), F32),
        compiler_params=_cparams("parallel", "parallel"),
        name="s5_chunk_output",
    )(u_blk, toep, x, w_out)
    y = y_blk.reshape(nb, n_rows, S5_CHUNK, LANES).transpose(1, 2, 0, 3).reshape(L, S5_WIDTH)
    tm = _tile(L, 512)
    row = lambda: pl.BlockSpec((tm, S5_WIDTH), lambda i: (i, 0))
    vec = lambda: pl.BlockSpec((1, S5_WIDTH), lambda i: (0, 0))
    return pl.pallas_call(
        _s5_glu_kernel,
        grid=(L // tm,),
        in_specs=[row(), row(), vec(), pl.BlockSpec((S5_WIDTH, S5_WIDTH), lambda i: (0, 0)), vec()],
        out_specs=row(),
        out_shape=jax.ShapeDtypeStruct((L, S5_WIDTH), BF16),
        compiler_params=_cparams("parallel"),
        name="s5_glu",
    )(y, u, d_skip.reshape(1, -1), glu_w, glu_b.reshape(1, -1))


def _rope_tables(L):
    rows = L // GRID_W
    r = jnp.repeat(jnp.arange(rows, dtype=F32), GRID_W)
    c = jnp.tile(jnp.arange(GRID_W, dtype=F32), rows)
    inv = ROPE_THETA ** (-jnp.arange(0, ROPE_AXIS, 2, dtype=F32) / ROPE_AXIS)
    ar = r[:, None] * inv
    ac = c[:, None] * inv
    cos = jnp.concatenate([jnp.cos(ar), jnp.cos(ar), jnp.cos(ac), jnp.cos(ac)], axis=-1)
    sin = jnp.concatenate([-jnp.sin(ar), jnp.sin(ar), -jnp.sin(ac), jnp.sin(ac)], axis=-1)
    return cos, sin


def _encoder(x, mem, p):
    L = x.shape[0]
    cos, sin = _rope_tables(L)
    for l in range(DEPTH):
        x = _ffn(x, p['ffn1_norm'][l], p['ffn1_w_gu'][l], p['ffn1_w_down'][l])
        if l % 2 == 0:
            e = l // 2
            u = _norm_matmul(x, p['mix_norm'][l], p['even_w_in_u'][e], F32)
            qkv = _norm_matmul(x, p['mix_norm'][l], p['even_w_in_qkv'][e], BF16, col_scale=p['diff_q_scale'])
            ya = _s5_mixer(u, p['s5'][e], p['s5_d'][e], p['s5_glu_w'][e], p['s5_glu_b'][e])
            yb = _diff_attention(qkv, p['diff_lambda_q1'][e], p['diff_lambda_k1'][e], p['diff_lambda_q2'][e],
                                 p['diff_lambda_k2'][e], p['diff_subln'][e], l)
            x = _matmul_residual(jnp.concatenate([ya, yb], axis=-1), p['even_w_out'][e], x)
        else:
            o = l // 2
            qkv = _odd_proj(x, p['mix_norm'][l], p['odd_w_in'][o], p['gqa_q_norm'][o], p['gqa_k_norm'][o], cos, sin)
            x = _matmul_residual(_gqa_attention(qkv), p['odd_w_out'][o], x)
        q = _norm_matmul(x, p['cross_norm'][l], p['cross_w_q'][l], BF16)
        kv = _norm_matmul(mem, p['mem_norm'][l], p['cross_w_kv'][l], BF16)
        x = _matmul_residual(_cross_attention(q, kv), p['cross_w_o'][l], x)
        x = _ffn(x, p['ffn2_norm'][l], p['ffn2_w_gu'][l], p['ffn2_w_down'][l],
                 final_g=p['final_norm'] if l == DEPTH - 1 else None)
    return x


def kernel(x_prompt, x_sample, mem_prompt, mem_sample, ffn1_norm, ffn1_w_gu, ffn1_w_down, mix_norm, even_w_in, even_w_out, s5_lambda_re, s5_lambda_im, s5_log_dt, s5_b_re, s5_b_im, s5_c_re, s5_c_im, s5_d, s5_glu_w, s5_glu_b, diff_lambda_q1, diff_lambda_k1, diff_lambda_q2, diff_lambda_k2, diff_subln, odd_w_in, odd_w_out, gqa_q_norm, gqa_k_norm, cross_norm, mem_norm, cross_w_q, cross_w_kv, cross_w_o, ffn2_norm, ffn2_w_gu, ffn2_w_down, final_norm):
    bf = lambda w: w.astype(BF16)
    diff_q_scale = jnp.concatenate([jnp.full((DIFF_WIDTH,), DIFF_HEAD ** -0.5 * LOG2E, F32),
                                    jnp.ones((2 * DIFF_WIDTH,), F32)])
    p = dict(
        ffn1_norm=ffn1_norm, ffn1_w_gu=bf(ffn1_w_gu), ffn1_w_down=bf(ffn1_w_down), mix_norm=mix_norm,
        even_w_in_u=bf(even_w_in[:, :, :S5_WIDTH]), even_w_in_qkv=bf(even_w_in[:, :, S5_WIDTH:]),
        diff_q_scale=diff_q_scale,
        even_w_out=bf(even_w_out),
        s5=[_s5_weights(s5_lambda_re[e], s5_lambda_im[e], s5_log_dt[e], s5_b_re[e], s5_b_im[e], s5_c_re[e], s5_c_im[e])
            for e in range(s5_lambda_re.shape[0])],
        s5_d=s5_d, s5_glu_w=bf(s5_glu_w), s5_glu_b=s5_glu_b,
        diff_lambda_q1=diff_lambda_q1, diff_lambda_k1=diff_lambda_k1, diff_lambda_q2=diff_lambda_q2,
        diff_lambda_k2=diff_lambda_k2, diff_subln=diff_subln,
        odd_w_in=bf(odd_w_in), odd_w_out=bf(odd_w_out), gqa_q_norm=gqa_q_norm, gqa_k_norm=gqa_k_norm,
        cross_norm=cross_norm, mem_norm=mem_norm, cross_w_q=bf(cross_w_q), cross_w_kv=bf(cross_w_kv),
        cross_w_o=bf(cross_w_o), ffn2_norm=ffn2_norm, ffn2_w_gu=bf(ffn2_w_gu), ffn2_w_down=bf(ffn2_w_down),
        final_norm=final_norm)
    outs = []
    for x, mem in ((x_prompt, mem_prompt), (x_sample, mem_sample)):
        outs.append(jnp.stack([_encoder(x[b], mem[b], p) for b in range(x.shape[0])]))
    return tuple(outs)
```

```python
import functools
import math

import jax
import jax.numpy as jnp
from jax import lax
from jax.experimental import pallas as pl
from jax.experimental.pallas import tpu as pltpu

F32 = jnp.float32
BF16 = jnp.bfloat16

D_MODEL = 2048
DEPTH = 4
EPS = 1e-6
D_FF = 5632
GRID_W = 64

S5_WIDTH = D_MODEL // 2
S5_GROUP = 16
S5_GROUPS = S5_WIDTH // S5_GROUP
S5_STATE = 64
S5_LAMBDA_RE_MAX = -1e-4

DIFF_WIDTH = D_MODEL - S5_WIDTH
DIFF_HEAD = 64
DIFF_HEADS = DIFF_WIDTH // (2 * DIFF_HEAD)
DIFF_VHEAD = 2 * DIFF_HEAD
DIFF_SUBLN_EPS = 1e-5

GQA_HEAD = 128
GQA_HEADS = D_MODEL // GQA_HEAD
GQA_KV_HEADS = 4
GQA_GROUP = GQA_HEADS // GQA_KV_HEADS
ROPE_AXIS = GQA_HEAD // 2
ROPE_THETA = 10000.0

X_HEADS = 4
X_HEAD = D_MODEL // X_HEADS

LOG2E = math.log2(math.e)

GQA_TQ = 512
DIFF_TQ = 1024
ATTN_TK = 2048
ATTN_ROWS = 256
ATTN_KEYS = 256

LANES = 128
SOFTMAX_UNDERFLOW_LOG2 = 150.0
NORM_BOUND_MARGIN = 1.01
V7X_VMEM_BYTES = 64 * 1024 * 1024
VMEM_LIMIT_BYTES = V7X_VMEM_BYTES - 8 * 1024 * 1024

S5_CHUNK = 16
S5_LANE_GROUPS = LANES // S5_GROUP
S5_LANE_BLOCKS = S5_WIDTH // LANES
S5_BLOCK_COLS = S5_CHUNK * LANES
S5_BLOCK_STATE = S5_LANE_GROUPS * S5_STATE


def _tile(n, pref):
    t = min(pref, n)
    while n % t:
        t //= 2
    return t


def _cparams(*sem):
    return pltpu.CompilerParams(dimension_semantics=sem, vmem_limit_bytes=VMEM_LIMIT_BYTES)


def _rmsnorm(x, g, eps):
    ms = jnp.mean(x * x, axis=-1, keepdims=True)
    return x * lax.rsqrt(ms + eps) * g


def _sigmoid(x):
    return 1.0 / (1.0 + jnp.exp(-x))


def _gelu_tanh(x):
    c = math.sqrt(2.0 / math.pi)
    return x * (0.5 * (1.0 + jnp.tanh(c * (x + 0.044715 * (x * x * x)))))


def _ffn_kernel(x_ref, g_ref, wg_ref, wu_ref, wd_ref, *rest, n_f, final):
    if final:
        fg_ref, o_ref, h_ref, acc_ref = rest
    else:
        o_ref, h_ref, acc_ref = rest
    j = pl.program_id(1)

    @pl.when(j == 0)
    def _():
        h_ref[...] = _rmsnorm(x_ref[...], g_ref[...], EPS).astype(BF16)
        acc_ref[...] = jnp.zeros_like(acc_ref)

    h = h_ref[...]
    gate = jnp.dot(h, wg_ref[...], preferred_element_type=F32)
    up = jnp.dot(h, wu_ref[...], preferred_element_type=F32)
    act = gate * _sigmoid(gate) * up
    acc_ref[...] += jnp.dot(act.astype(BF16), wd_ref[...], preferred_element_type=F32)

    @pl.when(j == n_f - 1)
    def _():
        y = x_ref[...] + 0.5 * acc_ref[...]
        if final:
            y = _rmsnorm(y, fg_ref[...], EPS)
        o_ref[...] = y


def _ffn(x, g, w_gu, w_down, final_g=None):
    L = x.shape[0]
    tm = _tile(L, 512)
    tf = 512
    n_f = D_FF // tf
    final = final_g is not None
    in_specs = [
        pl.BlockSpec((tm, D_MODEL), lambda i, j: (i, 0)),
        pl.BlockSpec((1, D_MODEL), lambda i, j: (0, 0)),
        pl.BlockSpec((D_MODEL, tf), lambda i, j: (0, j)),
        pl.BlockSpec((D_MODEL, tf), lambda i, j: (0, j + n_f)),
        pl.BlockSpec((tf, D_MODEL), lambda i, j: (j, 0)),
    ]
    args = [x, g.reshape(1, D_MODEL), w_gu, w_gu, w_down]
    if final:
        in_specs.append(pl.BlockSpec((1, D_MODEL), lambda i, j: (0, 0)))
        args.append(final_g.reshape(1, D_MODEL))
    return pl.pallas_call(
        functools.partial(_ffn_kernel, n_f=n_f, final=final),
        grid=(L // tm, n_f),
        in_specs=in_specs,
        out_specs=pl.BlockSpec((tm, D_MODEL), lambda i, j: (i, 0)),
        out_shape=jax.ShapeDtypeStruct((L, D_MODEL), F32),
        scratch_shapes=[pltpu.VMEM((tm, D_MODEL), BF16), pltpu.VMEM((tm, D_MODEL), F32)],
        compiler_params=_cparams("parallel", "arbitrary"),
        name="ffn_final" if final else "ffn",
    )(*args)


def _norm_matmul_kernel(x_ref, g_ref, w_ref, *rest, scaled):
    if scaled:
        c_ref, o_ref, h_ref = rest
    else:
        o_ref, h_ref = rest

    @pl.when(pl.program_id(1) == 0)
    def _():
        h_ref[...] = _rmsnorm(x_ref[...], g_ref[...], EPS).astype(BF16)

    z = jnp.dot(h_ref[...], w_ref[...], preferred_element_type=F32)
    if scaled:
        z = z * c_ref[...]
    o_ref[...] = z.astype(o_ref.dtype)


def _norm_matmul(x, g, w, out_dtype, col_scale=None):
    L = x.shape[0]
    N = w.shape[1]
    tm = _tile(L, 512)
    tn = _tile(N, 512)
    scaled = col_scale is not None
    in_specs = [
        pl.BlockSpec((tm, D_MODEL), lambda i, j: (i, 0)),
        pl.BlockSpec((1, D_MODEL), lambda i, j: (0, 0)),
        pl.BlockSpec((D_MODEL, tn), lambda i, j: (0, j)),
    ]
    args = [x, g.reshape(1, D_MODEL), w]
    if scaled:
        in_specs.append(pl.BlockSpec((1, tn), lambda i, j: (0, j)))
        args.append(col_scale.reshape(1, N))
    return pl.pallas_call(
        functools.partial(_norm_matmul_kernel, scaled=scaled),
        grid=(L // tm, N // tn),
        in_specs=in_specs,
        out_specs=pl.BlockSpec((tm, tn), lambda i, j: (i, j)),
        out_shape=jax.ShapeDtypeStruct((L, N), out_dtype),
        scratch_shapes=[pltpu.VMEM((tm, D_MODEL), BF16)],
        compiler_params=_cparams("parallel", "arbitrary"),
        name="norm_matmul",
    )(*args)


def _odd_proj_kernel(x_ref, g_ref, w_ref, qg_ref, kg_ref, cos_ref, sin_ref, o_ref, h_ref, *, heads_per_tile):
    j = pl.program_id(1)
    n_q_tiles = GQA_HEADS // heads_per_tile
    n_k_tiles = GQA_KV_HEADS // heads_per_tile

    @pl.when(j == 0)
    def _():
        h_ref[...] = _rmsnorm(x_ref[...], g_ref[...], EPS).astype(BF16)

    z = jnp.dot(h_ref[...], w_ref[...], preferred_element_type=F32)

    @pl.when(j < n_q_tiles + n_k_tiles)
    def _():
        gain = jnp.where(j < n_q_tiles, qg_ref[...], kg_ref[...])
        out_scale = jnp.where(j < n_q_tiles, GQA_HEAD ** -0.5 * LOG2E, 1.0)
        cos = cos_ref[...] * out_scale
        sin = sin_ref[...] * out_scale
        lane = lax.broadcasted_iota(jnp.int32, (1, GQA_HEAD), 1)
        first_half = (lane & (ROPE_AXIS - 1)) < (ROPE_AXIS // 2)
        for hh in range(heads_per_tile):
            sl = slice(hh * GQA_HEAD, (hh + 1) * GQA_HEAD)
            y = _rmsnorm(z[:, sl], gain, EPS)
            partner = jnp.where(first_half, pltpu.roll(y, GQA_HEAD - ROPE_AXIS // 2, 1),
                                pltpu.roll(y, ROPE_AXIS // 2, 1))
            o_ref[:, sl] = (y * cos + partner * sin).astype(o_ref.dtype)

    @pl.when(j >= n_q_tiles + n_k_tiles)
    def _():
        o_ref[...] = z.astype(o_ref.dtype)


def _odd_proj(x, g, w, q_g, k_g, cos, sin):
    L = x.shape[0]
    N = w.shape[1]
    tm = _tile(L, 512)
    heads_per_tile = 4
    tn = heads_per_tile * GQA_HEAD
    return pl.pallas_call(
        functools.partial(_odd_proj_kernel, heads_per_tile=heads_per_tile),
        grid=(L // tm, N // tn),
        in_specs=[
            pl.BlockSpec((tm, D_MODEL), lambda i, j: (i, 0)),
            pl.BlockSpec((1, D_MODEL), lambda i, j: (0, 0)),
            pl.BlockSpec((D_MODEL, tn), lambda i, j: (0, j)),
            pl.BlockSpec((1, GQA_HEAD), lambda i, j: (0, 0)),
            pl.BlockSpec((1, GQA_HEAD), lambda i, j: (0, 0)),
            pl.BlockSpec((tm, GQA_HEAD), lambda i, j: (i, 0)),
            pl.BlockSpec((tm, GQA_HEAD), lambda i, j: (i, 0)),
        ],
        out_specs=pl.BlockSpec((tm, tn), lambda i, j: (i, j)),
        out_shape=jax.ShapeDtypeStruct((L, N), BF16),
        scratch_shapes=[pltpu.VMEM((tm, D_MODEL), BF16)],
        compiler_params=_cparams("parallel", "arbitrary"),
        name="odd_proj",
    )(x, g.reshape(1, D_MODEL), w, q_g.reshape(1, GQA_HEAD), k_g.reshape(1, GQA_HEAD), cos, sin)


def _matmul_residual_kernel(a_ref, w_ref, r_ref, o_ref):
    o_ref[...] = r_ref[...] + jnp.dot(a_ref[...], w_ref[...], preferred_element_type=F32)


def _matmul_residual(a, w, res):
    L, K = a.shape
    N = w.shape[1]
    tm = _tile(L, 512)
    return pl.pallas_call(
        _matmul_residual_kernel,
        grid=(L // tm,),
        in_specs=[
            pl.BlockSpec((tm, K), lambda i: (i, 0)),
            pl.BlockSpec((K, N), lambda i: (0, 0)),
            pl.BlockSpec((tm, N), lambda i: (i, 0)),
        ],
        out_specs=pl.BlockSpec((tm, N), lambda i: (i, 0)),
        out_shape=jax.ShapeDtypeStruct((L, N), F32),
        compiler_params=_cparams("parallel"),
        name="matmul_residual",
    )(a, w, res)


def _lane_tile(x, n):
    return jnp.concatenate([x] * n, axis=1) if n > 1 else x


def _softmax_block(t, shift, v_ones, m, acc):
    row_max = jnp.max(t, axis=1, keepdims=True)
    m_new = jnp.maximum(m, row_max if shift is None else row_max + shift)
    alpha = jnp.exp2(m - m_new)
    p = jnp.exp2((t - _lane_tile(m_new if shift is None else m_new - shift, t.shape[1] // LANES)).astype(BF16))
    acc = _lane_tile(alpha, 2) * acc + jnp.dot(p, v_ones, preferred_element_type=F32)
    return m_new, acc


def _softmax_init(m_ref, acc_ref, vo_ref):
    m_ref[...] = jnp.full_like(m_ref, -jnp.inf)
    acc_ref[...] = jnp.zeros_like(acc_ref)
    vo_ref[:, LANES:] = jnp.ones((vo_ref.shape[0], LANES), vo_ref.dtype)


def _softmax_result(acc):
    return acc[:, :LANES] / acc[:, LANES:]


_NT = (((1,), (1,)), ((), ()))


def _gqa_kernel(q_ref, k_ref, v_ref, o_ref, m_ref, acc_ref, vo_ref, *, n_k, tq, tk, rows, keys):
    ki = pl.program_id(2)

    @pl.when(ki == 0)
    def _():
        _softmax_init(m_ref, acc_ref, vo_ref)

    vo_ref[:, :LANES] = v_ref[...]
    for g in range(GQA_GROUP):
        for c in range(tq // rows):
            sl = slice(g * tq + c * rows, g * tq + (c + 1) * rows)
            q = q_ref[c * rows:(c + 1) * rows, g * GQA_HEAD:(g + 1) * GQA_HEAD]
            state = (m_ref[sl, :], acc_ref[sl, :])
            for kk in range(tk // keys):
                ksl = slice(kk * keys, (kk + 1) * keys)
                s = lax.dot_general(q, k_ref[ksl, :], _NT, preferred_element_type=F32)
                state = _softmax_block(s, None, vo_ref[ksl, :], *state)
            m_ref[sl, :], acc_ref[sl, :] = state

    @pl.when(ki == n_k - 1)
    def _():
        for g in range(GQA_GROUP):
            sl = slice(g * tq, (g + 1) * tq)
            o_ref[:, g * GQA_HEAD:(g + 1) * GQA_HEAD] = _softmax_result(acc_ref[sl, :]).astype(o_ref.dtype)


def _gqa_attention(qkv):
    L = qkv.shape[0]
    tq = _tile(L, GQA_TQ)
    tk = _tile(L, ATTN_TK)
    n_k = L // tk
    group_cols = GQA_GROUP * GQA_HEAD
    return pl.pallas_call(
        functools.partial(_gqa_kernel, n_k=n_k, tq=tq, tk=tk, rows=_tile(tq, ATTN_ROWS), keys=_tile(tk, ATTN_KEYS)),
        grid=(GQA_KV_HEADS, L // tq, n_k),
        in_specs=[
            pl.BlockSpec((tq, group_cols), lambda h, i, j: (i, h)),
            pl.BlockSpec((tk, GQA_HEAD), lambda h, i, j: (j, GQA_HEADS + h)),
            pl.BlockSpec((tk, GQA_HEAD), lambda h, i, j: (j, GQA_HEADS + GQA_KV_HEADS + h)),
        ],
        out_specs=pl.BlockSpec((tq, group_cols), lambda h, i, j: (i, h)),
        out_shape=jax.ShapeDtypeStruct((L, D_MODEL), BF16),
        scratch_shapes=[
            pltpu.VMEM((GQA_GROUP * tq, LANES), F32),
            pltpu.VMEM((GQA_GROUP * tq, 2 * LANES), F32),
            pltpu.VMEM((tk, 2 * LANES), BF16),
        ],
        compiler_params=_cparams("parallel", "parallel", "arbitrary"),
        name="gqa_attention",
    )(qkv, qkv, qkv)


def _sq_norm_max_kernel(x_ref, o_ref, *, half):
    x = x_ref[...].astype(F32)
    x2 = x * x
    first = lax.broadcasted_iota(jnp.int32, (1, LANES), 1) < half
    blocks = []
    for b in range(x2.shape[1] // LANES):
        xb = x2[:, b * LANES:(b + 1) * LANES]
        n1 = jnp.max(jnp.sum(jnp.where(first, xb, 0.0), axis=1, keepdims=True), axis=0, keepdims=True)
        n2 = jnp.max(jnp.sum(jnp.where(first, 0.0, xb), axis=1, keepdims=True), axis=0, keepdims=True)
        blocks.append(jnp.where(first, n1, n2))
    cur = jnp.concatenate(blocks, axis=1)

    @pl.when(pl.program_id(0) == 0)
    def _():
        o_ref[...] = cur

    @pl.when(pl.program_id(0) > 0)
    def _():
        o_ref[...] = jnp.maximum(o_ref[...], cur)


def _sq_norm_max(x, n_cols, half):
    L = x.shape[0]
    tm = _tile(L, 512)
    return pl.pallas_call(
        functools.partial(_sq_norm_max_kernel, half=half),
        grid=(L // tm,),
        in_specs=[pl.BlockSpec((tm, n_cols), lambda i: (i, 0))],
        out_specs=pl.BlockSpec((1, n_cols), lambda i: (0, 0)),
        out_shape=jax.ShapeDtypeStruct((1, n_cols), F32),
        compiler_params=_cparams("arbitrary"),
        name="sq_norm_max",
    )(x)


def _diff_kernel(klo_ref, khi_ref, slope_ref, q_ref, k_ref, v_ref, lq1_ref, lk1_ref, lq2_ref, lk2_ref, sg_ref,
                 o_ref, qs_ref, m_ref, acc_ref, vo_ref, *, n_k, tq, tk, rows, keys, lambda_init):
    h = pl.program_id(0)
    qi = pl.program_id(1)
    j = pl.program_id(2)
    q0 = qi * tq
    kt = klo_ref[h, qi] + j
    active = kt <= khi_ref[h, qi]
    k0 = kt * tk
    n_c = tq // rows

    @pl.when(j == 0)
    def _():
        _softmax_init(m_ref, acc_ref, vo_ref)
        q = q_ref[...]
        lane = lax.broadcasted_iota(jnp.int32, (1, 2 * DIFF_HEAD), 1)
        zero = jnp.zeros_like(q)
        q1 = jnp.where(lane < DIFF_HEAD, q, zero)
        q2 = jnp.where(lane >= DIFF_HEAD, q, zero)
        for c in range(n_c):
            qs_ref[2 * c * rows:(2 * c + 1) * rows, :] = q1[c * rows:(c + 1) * rows]
            qs_ref[(2 * c + 1) * rows:(2 * c + 2) * rows, :] = q2[c * rows:(c + 1) * rows]

    slope2 = slope_ref[h] * LOG2E
    d0 = lax.broadcasted_iota(jnp.int32, (rows, keys), 0) - lax.broadcasted_iota(jnp.int32, (rows, keys), 1)

    def step(bias_fn):
        vo_ref[:, :LANES] = v_ref[...]
        for c in range(n_c):
            sl = slice(2 * c * rows, 2 * (c + 1) * rows)
            state = (m_ref[sl, :], acc_ref[sl, :])
            for kk in range(tk // keys):
                ksl = slice(kk * keys, (kk + 1) * keys)
                bias, shift = bias_fn(q0 + c * rows - (k0 + kk * keys))
                t = lax.dot_general(qs_ref[sl, :], k_ref[ksl, :], _NT, preferred_element_type=F32)
                t = t + jnp.concatenate([bias, bias], axis=0)
                state = _softmax_block(t, shift, vo_ref[ksl, :], *state)
            m_ref[sl, :], acc_ref[sl, :] = state

    crosses_diagonal = jnp.logical_and(k0 < q0 + tq, q0 < k0 + tk)

    @pl.when(jnp.logical_and(active, crosses_diagonal))
    def _():
        step(lambda off: ((-slope2) * jnp.abs(off + d0).astype(F32), None))

    @pl.when(jnp.logical_and(active, jnp.logical_not(crosses_diagonal)))
    def _():
        coef = jnp.where(q0 >= k0, -slope2, slope2)
        tile = coef * d0.astype(F32)
        step(lambda off: (tile, coef * off.astype(F32)))

    @pl.when(j == n_k - 1)
    def _():
        lam = (jnp.exp(jnp.sum(lq1_ref[...] * lk1_ref[...], axis=-1, keepdims=True))
               - jnp.exp(jnp.sum(lq2_ref[...] * lk2_ref[...], axis=-1, keepdims=True)) + lambda_init)
        for c in range(n_c):
            o1 = _softmax_result(acc_ref[2 * c * rows:(2 * c + 1) * rows, :])
            o2 = _softmax_result(acc_ref[(2 * c + 1) * rows:(2 * c + 2) * rows, :])
            o_ref[c * rows:(c + 1) * rows, :] = (
                _rmsnorm(o1 - lam * o2, sg_ref[...], DIFF_SUBLN_EPS) * (1.0 - lambda_init)).astype(o_ref.dtype)


def _diff_key_ranges(qkv, slopes, tq, tk):
    L = qkv.shape[0]
    n_q, n_k = L // tq, L // tk
    sq = _sq_norm_max(qkv, 2 * DIFF_WIDTH, DIFF_HEAD).reshape(2, DIFF_HEADS, 2, DIFF_HEAD)[..., 0]
    norms = jnp.sqrt(sq)
    bound = jnp.max(norms[0] * norms[1], axis=1) * NORM_BOUND_MARGIN
    radius = (SOFTMAX_UNDERFLOW_LOG2 + 2.0 * bound) / (slopes * LOG2E)
    radius = jnp.where(jnp.isfinite(radius), radius, float(L))[:, None]
    q_first = (jnp.arange(n_q, dtype=F32) * tq)[None, :]
    klo = jnp.clip(jnp.floor((q_first - radius) / tk), 0, n_k - 1).astype(jnp.int32)
    khi = jnp.clip(jnp.floor((q_first + (tq - 1) + radius) / tk), 0, n_k - 1).astype(jnp.int32)
    return klo, khi


def _diff_attention(qkv, lq1, lk1, lq2, lk2, subln_g, layer_idx):
    L = qkv.shape[0]
    tq = _tile(L, DIFF_TQ)
    tk = _tile(L, ATTN_TK)
    n_k = L // tk
    lambda_init = 0.8 - 0.6 * math.exp(-0.3 * layer_idx)
    slopes = jnp.asarray([2.0 ** (-8.0 * (i + 1) / DIFF_HEADS) for i in range(DIFF_HEADS)], dtype=F32)
    klo, khi = _diff_key_ranges(qkv, slopes, tq, tk)
    key_tile = lambda h, i, j, klo, khi: jnp.minimum(klo[h, i] + j, khi[h, i])
    vec = lambda: pl.BlockSpec((1, DIFF_HEAD), lambda h, i, j, klo, khi: (0, 0))
    return pl.pallas_call(
        functools.partial(_diff_kernel, n_k=n_k, tq=tq, tk=tk, rows=_tile(tq, ATTN_ROWS), keys=_tile(tk, ATTN_KEYS),
                          lambda_init=lambda_init),
        grid_spec=pltpu.PrefetchScalarGridSpec(
            num_scalar_prefetch=2,
            grid=(DIFF_HEADS, L // tq, n_k),
            in_specs=[
                pl.BlockSpec(memory_space=pltpu.SMEM),
                pl.BlockSpec((tq, DIFF_VHEAD), lambda h, i, j, klo, khi: (i, h)),
                pl.BlockSpec((tk, DIFF_VHEAD), lambda h, i, j, klo, khi: (key_tile(h, i, j, klo, khi), DIFF_HEADS + h)),
                pl.BlockSpec((tk, DIFF_VHEAD),
                             lambda h, i, j, klo, khi: (key_tile(h, i, j, klo, khi), 2 * DIFF_HEADS + h)),
                vec(), vec(), vec(), vec(),
                pl.BlockSpec((1, DIFF_VHEAD), lambda h, i, j, klo, khi: (0, 0)),
            ],
            out_specs=pl.BlockSpec((tq, DIFF_VHEAD), lambda h, i, j, klo, khi: (i, h)),
            scratch_shapes=[
                pltpu.VMEM((2 * tq, DIFF_VHEAD), BF16),
                pltpu.VMEM((2 * tq, LANES), F32),
                pltpu.VMEM((2 * tq, 2 * LANES), F32),
                pltpu.VMEM((tk, 2 * LANES), BF16),
            ],
        ),
        out_shape=jax.ShapeDtypeStruct((L, DIFF_WIDTH), BF16),
        compiler_params=_cparams("parallel", "parallel", "arbitrary"),
        name="diff_attention",
    )(klo, khi, slopes, qkv, qkv, qkv, lq1.reshape(1, -1), lk1.reshape(1, -1), lq2.reshape(1, -1),
      lk2.reshape(1, -1), subln_g.reshape(1, -1))


def _cross_kernel(q_ref, kv_ref, o_ref, *, scale):
    for h in range(X_HEADS):
        sl = slice(h * X_HEAD, (h + 1) * X_HEAD)
        vsl = slice(D_MODEL + h * X_HEAD, D_MODEL + (h + 1) * X_HEAD)
        s = lax.dot_general(q_ref[:, sl], kv_ref[:, sl], (((1,), (1,)), ((), ())),
                            preferred_element_type=F32) * scale
        e = jnp.exp(s - jnp.max(s, axis=-1, keepdims=True))
        p = e / jnp.sum(e, axis=-1, keepdims=True)
        o_ref[:, sl] = jnp.dot(p.astype(BF16), kv_ref[:, vsl], preferred_element_type=F32).astype(o_ref.dtype)


def _cross_attention(q, kv):
    L = q.shape[0]
    n_mem = kv.shape[0]
    tq = _tile(L, 512)
    return pl.pallas_call(
        functools.partial(_cross_kernel, scale=X_HEAD ** -0.5),
        grid=(L // tq,),
        in_specs=[
            pl.BlockSpec((tq, D_MODEL), lambda i: (i, 0)),
            pl.BlockSpec((n_mem, 2 * D_MODEL), lambda i: (0, 0)),
        ],
        out_specs=pl.BlockSpec((tq, D_MODEL), lambda i: (i, 0)),
        out_shape=jax.ShapeDtypeStruct((L, D_MODEL), BF16),
        compiler_params=_cparams("parallel"),
        name="cross_attention",
    )(q, kv)


def _s5_weights(lam_re, lam_im, log_dt, b_re, b_im, c_re, c_im):
    T = S5_CHUNK
    hi = lax.Precision.HIGHEST
    lr = jnp.minimum(lam_re, S5_LAMBDA_RE_MAX)
    li = lam_im
    dt = jnp.exp(log_dt)[..., None]
    mag = jnp.exp(lr * dt)
    ab_re = mag * jnp.cos(li * dt)
    ab_im = mag * jnp.sin(li * dt)
    nr = ab_re - 1.0
    den = lr * lr + li * li
    f_re = ((nr * lr + ab_im * li) / den)[..., None]
    f_im = ((ab_im * lr - nr * li) / den)[..., None]
    bb_re = f_re * b_re - f_im * b_im
    bb_im = f_re * b_im + f_im * b_re
    k = jnp.arange(T + 1, dtype=F32)[:, None, None, None]
    pmag = jnp.exp(lr * dt * k)
    pw_re = pmag * jnp.cos(li * dt * k)
    pw_im = pmag * jnp.sin(li * dt * k)
    ca_re = c_re * pw_re[:, :, :, None, :] - c_im * pw_im[:, :, :, None, :]
    ca_im = c_re * pw_im[:, :, :, None, :] + c_im * pw_re[:, :, :, None, :]
    eye = jnp.eye(S5_LANE_GROUPS, dtype=F32)
    nb, ng = S5_LANE_BLOCKS, S5_LANE_GROUPS

    kern = (jnp.einsum('kdghp,dgpi->kdghi', ca_re[:T], bb_re, precision=hi)
            - jnp.einsum('kdghp,dgpi->kdghi', ca_im[:T], bb_im, precision=hi))
    ktab = jnp.concatenate([kern[1:, 1][::-1], (kern[0, 0] + kern[0, 1])[None], kern[1:, 0]], axis=0)
    idx = jnp.arange(T)[None, :] - jnp.arange(T)[:, None] + (T - 1)
    a6 = ktab[idx].reshape(T, T, nb, ng, S5_GROUP, S5_GROUP)
    toep = jnp.einsum('stcgoi,gk->csgitko', a6, eye).reshape(nb, S5_BLOCK_COLS, S5_BLOCK_COLS)

    e_re = jnp.stack([pw_re[:T, 0][::-1], pw_re[:T, 1]], axis=1)[..., None]
    e_im = jnp.stack([pw_im[:T, 0][::-1], pw_im[:T, 1]], axis=1)[..., None]
    zb = jnp.stack([e_re * bb_re - e_im * bb_im, e_re * bb_im + e_im * bb_re], axis=2)
    z7 = zb.reshape(T, 2, 2, nb, ng, S5_STATE, S5_GROUP)
    w_in = jnp.einsum('sdrcgpi,gk->csgidrkp', z7, eye).reshape(nb, S5_BLOCK_COLS, 4 * S5_BLOCK_STATE)

    cr = jnp.stack([ca_re[1:, 0], ca_re[1:, 1][::-1]], axis=1)
    ci = jnp.stack([ca_im[1:, 0], ca_im[1:, 1][::-1]], axis=1)
    c7 = jnp.stack([cr, -ci], axis=2).reshape(T, 2, 2, nb, ng, S5_GROUP, S5_STATE)
    w_out = jnp.einsum('tdrcgop,gk->cdrgptko', c7, eye).reshape(nb, 4 * S5_BLOCK_STATE, S5_BLOCK_COLS)

    a4 = jnp.stack([pw_re[T], pw_im[T]], axis=1).reshape(2, 2, nb, ng, S5_STATE)
    a_t = a4.transpose(2, 0, 1, 3, 4).reshape(nb, 1, 4 * S5_BLOCK_STATE)
    return toep.astype(BF16), w_in.astype(BF16), w_out.astype(BF16), a_t


def _s5_in_kernel(u_ref, w_ref, z_ref):
    z_ref[0] = jnp.dot(u_ref[0], w_ref[0], preferred_element_type=F32)


def _s5_out_kernel(u_ref, toep_ref, x_ref, w_ref, y_ref):
    y_ref[0] = (jnp.dot(u_ref[0], toep_ref[0], preferred_element_type=F32)
                + jnp.dot(x_ref[0].astype(BF16), w_ref[0], preferred_element_type=F32))


def _s5_scan_kernel(z_ref, a_ref, x_ref, *, n_rows):
    n_s = S5_BLOCK_STATE
    a = a_ref[0]
    af_re, af_im, ab_re, ab_im = (a[:, i * n_s:(i + 1) * n_s] for i in range(4))

    def body(n, carry):
        f_re, f_im, b_re, b_im = carry
        rf = pl.ds(n, 1)
        rb = pl.ds(n_rows - 1 - n, 1)
        x_ref[0, rf, 0 * n_s:1 * n_s] = f_re
        x_ref[0, rf, 1 * n_s:2 * n_s] = f_im
        x_ref[0, rb, 2 * n_s:3 * n_s] = b_re
        x_ref[0, rb, 3 * n_s:4 * n_s] = b_im
        zf_re = z_ref[0, rf, 0 * n_s:1 * n_s]
        zf_im = z_ref[0, rf, 1 * n_s:2 * n_s]
        zb_re = z_ref[0, rb, 2 * n_s:3 * n_s]
        zb_im = z_ref[0, rb, 3 * n_s:4 * n_s]
        return (af_re * f_re - af_im * f_im + zf_re, af_re * f_im + af_im * f_re + zf_im,
                ab_re * b_re - ab_im * b_im + zb_re, ab_re * b_im + ab_im * b_re + zb_im)

    zero = jnp.zeros((1, n_s), F32)
    lax.fori_loop(0, n_rows, body, (zero, zero, zero, zero))


def _s5_glu_kernel(y_ref, u_ref, d_ref, w_ref, b_ref, o_ref):
    y = d_ref[...] * u_ref[...] + y_ref[...]
    g = _gelu_tanh(y)
    gate = jnp.dot(g.astype(BF16), w_ref[...], preferred_element_type=F32) + b_ref[...]
    o_ref[...] = (g * _sigmoid(gate)).astype(o_ref.dtype)


def _s5_mixer(u, s5w, d_skip, glu_w, glu_b):
    L = u.shape[0]
    toep, w_in, w_out, a_t = s5w
    nb = S5_LANE_BLOCKS
    n_rows = L // S5_CHUNK
    n_st = 4 * S5_BLOCK_STATE
    tr = _tile(n_rows, 512)
    u_blk = u.reshape(n_rows, S5_CHUNK, nb, LANES).transpose(2, 0, 1, 3).reshape(nb, n_rows, S5_BLOCK_COLS)
    u_blk = u_blk.astype(BF16)
    blk = lambda cols: pl.BlockSpec((1, tr, cols), lambda c, i: (c, i, 0))
    wgt = lambda rows, cols: pl.BlockSpec((1, rows, cols), lambda c, i: (c, 0, 0))
    z = pl.pallas_call(
        _s5_in_kernel,
        grid=(nb, n_rows // tr),
        in_specs=[blk(S5_BLOCK_COLS), wgt(S5_BLOCK_COLS, n_st)],
        out_specs=blk(n_st),
        out_shape=jax.ShapeDtypeStruct((nb, n_rows, n_st), F32),
        compiler_params=_cparams("parallel", "parallel"),
        name="s5_chunk_input",
    )(u_blk, w_in)
    x = pl.pallas_call(
        functools.partial(_s5_scan_kernel, n_rows=n_rows),
        grid=(nb,),
        in_specs=[pl.BlockSpec((1, n_rows, n_st), lambda c: (c, 0, 0)),
                  pl.BlockSpec((1, 1, n_st), lambda c: (c, 0, 0))],
        out_specs=pl.BlockSpec((1, n_rows, n_st), lambda c: (c, 0, 0)),
        out_shape=jax.ShapeDtypeStruct((nb, n_rows, n_st), F32),
        compiler_params=_cparams("parallel"),
        name="s5_chunk_scan",
    )(z, a_t)
    y_blk = pl.pallas_call(
        _s5_out_kernel,
        grid=(nb, n_rows // tr),
        in_specs=[blk(S5_BLOCK_COLS), wgt(S5_BLOCK_COLS, S5_BLOCK_COLS), blk(n_st), wgt(n_st, S5_BLOCK_COLS)],
        out_specs=blk(S5_BLOCK_COLS),
        out_shape=jax.ShapeDtypeStruct((nb, n_rows, S5_BLOCK_COLS), F32),
        compiler_params=_cparams("parallel", "parallel"),
        name="s5_chunk_output",
    )(u_blk, toep, x, w_out)
    y = y_blk.reshape(nb, n_rows, S5_CHUNK, LANES).transpose(1, 2, 0, 3).reshape(L, S5_WIDTH)
    tm = _tile(L, 512)
    row = lambda: pl.BlockSpec((tm, S5_WIDTH), lambda i: (i, 0))
    vec = lambda: pl.BlockSpec((1, S5_WIDTH), lambda i: (0, 0))
    return pl.pallas_call(
        _s5_glu_kernel,
        grid=(L // tm,),
        in_specs=[row(), row(), vec(), pl.BlockSpec((S5_WIDTH, S5_WIDTH), lambda i: (0, 0)), vec()],
        out_specs=row(),
        out_shape=jax.ShapeDtypeStruct((L, S5_WIDTH), BF16),
        compiler_params=_cparams("parallel"),
        name="s5_glu",
    )(y, u, d_skip.reshape(1, -1), glu_w, glu_b.reshape(1, -1))


def _rope_tables(L):
    rows = L // GRID_W
    r = jnp.repeat(jnp.arange(rows, dtype=F32), GRID_W)
    c = jnp.tile(jnp.arange(GRID_W, dtype=F32), rows)
    inv = ROPE_THETA ** (-jnp.arange(0, ROPE_AXIS, 2, dtype=F32) / ROPE_AXIS)
    ar = r[:, None] * inv
    ac = c[:, None] * inv
    cos = jnp.concatenate([jnp.cos(ar), jnp.cos(ar), jnp.cos(ac), jnp.cos(ac)], axis=-1)
    sin = jnp.concatenate([-jnp.sin(ar), jnp.sin(ar), -jnp.sin(ac), jnp.sin(ac)], axis=-1)
    return cos, sin


def _encoder(x, mem, p):
    L = x.shape[0]
    cos, sin = _rope_tables(L)
    for l in range(DEPTH):
        x = _ffn(x, p['ffn1_norm'][l], p['ffn1_w_gu'][l], p['ffn1_w_down'][l])
        if l % 2 == 0:
            e = l // 2
            u = _norm_matmul(x, p['mix_norm'][l], p['even_w_in_u'][e], F32)
            qkv = _norm_matmul(x, p['mix_norm'][l], p['even_w_in_qkv'][e], BF16, col_scale=p['diff_q_scale'])
            ya = _s5_mixer(u, p['s5'][e], p['s5_d'][e], p['s5_glu_w'][e], p['s5_glu_b'][e])
            yb = _diff_attention(qkv, p['diff_lambda_q1'][e], p['diff_lambda_k1'][e], p['diff_lambda_q2'][e],
                                 p['diff_lambda_k2'][e], p['diff_subln'][e], l)
            x = _matmul_residual(jnp.concatenate([ya, yb], axis=-1), p['even_w_out'][e], x)
        else:
            o = l // 2
            qkv = _odd_proj(x, p['mix_norm'][l], p['odd_w_in'][o], p['gqa_q_norm'][o], p['gqa_k_norm'][o], cos, sin)
            x = _matmul_residual(_gqa_attention(qkv), p['odd_w_out'][o], x)
        q = _norm_matmul(x, p['cross_norm'][l], p['cross_w_q'][l], BF16)
        kv = _norm_matmul(mem, p['mem_norm'][l], p['cross_w_kv'][l], BF16)
        x = _matmul_residual(_cross_attention(q, kv), p['cross_w_o'][l], x)
        x = _ffn(x, p['ffn2_norm'][l], p['ffn2_w_gu'][l], p['ffn2_w_down'][l],
                 final_g=p['final_norm'] if l == DEPTH - 1 else None)
    return x


def kernel(x_prompt, x_sample, mem_prompt, mem_sample, ffn1_norm, ffn1_w_gu, ffn1_w_down, mix_norm, even_w_in, even_w_out, s5_lambda_re, s5_lambda_im, s5_log_dt, s5_b_re, s5_b_im, s5_c_re, s5_c_im, s5_d, s5_glu_w, s5_glu_b, diff_lambda_q1, diff_lambda_k1, diff_lambda_q2, diff_lambda_k2, diff_subln, odd_w_in, odd_w_out, gqa_q_norm, gqa_k_norm, cross_norm, mem_norm, cross_w_q, cross_w_kv, cross_w_o, ffn2_norm, ffn2_w_gu, ffn2_w_down, final_norm):
    bf = lambda w: w.astype(BF16)
    diff_q_scale = jnp.concatenate([jnp.full((DIFF_WIDTH,), DIFF_HEAD ** -0.5 * LOG2E, F32),
                                    jnp.ones((2 * DIFF_WIDTH,), F32)])
    p = dict(
        ffn1_norm=ffn1_norm, ffn1_w_gu=bf(ffn1_w_gu), ffn1_w_down=bf(ffn1_w_down), mix_norm=mix_norm,
        even_w_in_u=bf(even_w_in[:, :, :S5_WIDTH]), even_w_in_qkv=bf(even_w_in[:, :, S5_WIDTH:]),
        diff_q_scale=diff_q_scale,
        even_w_out=bf(even_w_out),
        s5=[_s5_weights(s5_lambda_re[e], s5_lambda_im[e], s5_log_dt[e], s5_b_re[e], s5_b_im[e], s5_c_re[e], s5_c_im[e])
            for e in range(s5_lambda_re.shape[0])],
        s5_d=s5_d, s5_glu_w=bf(s5_glu_w), s5_glu_b=s5_glu_b,
        diff_lambda_q1=diff_lambda_q1, diff_lambda_k1=diff_lambda_k1, diff_lambda_q2=diff_lambda_q2,
        diff_lambda_k2=diff_lambda_k2, diff_subln=diff_subln,
        odd_w_in=bf(odd_w_in), odd_w_out=bf(odd_w_out), gqa_q_norm=gqa_q_norm, gqa_k_norm=gqa_k_norm,
        cross_norm=cross_norm, mem_norm=mem_norm, cross_w_q=bf(cross_w_q), cross_w_kv=bf(cross_w_kv),
        cross_w_o=bf(cross_w_o), ffn2_norm=ffn2_norm, ffn2_w_gu=bf(ffn2_w_gu), ffn2_w_down=bf(ffn2_w_down),
        final_norm=final_norm)
    outs = []
    for x, mem in ((x_prompt, mem_prompt), (x_sample, mem_sample)):
        outs.append(jnp.stack([_encoder(x[b], mem[b], p) for b in range(x.shape[0])]))
    return tuple(outs)
```

```python
import functools
import math

import jax
import jax.numpy as jnp
from jax import lax
from jax.experimental import pallas as pl
from jax.experimental.pallas import tpu as pltpu

F32 = jnp.float32
BF16 = jnp.bfloat16

D_MODEL = 2048
DEPTH = 4
EPS = 1e-6
D_FF = 5632
GRID_W = 64

S5_WIDTH = D_MODEL // 2
S5_GROUP = 16
S5_GROUPS = S5_WIDTH // S5_GROUP
S5_STATE = 64
S5_LAMBDA_RE_MAX = -1e-4

DIFF_WIDTH = D_MODEL - S5_WIDTH
DIFF_HEAD = 64
DIFF_HEADS = DIFF_WIDTH // (2 * DIFF_HEAD)
DIFF_VHEAD = 2 * DIFF_HEAD
DIFF_SUBLN_EPS = 1e-5

GQA_HEAD = 128
GQA_HEADS = D_MODEL // GQA_HEAD
GQA_KV_HEADS = 4
GQA_GROUP = GQA_HEADS // GQA_KV_HEADS
ROPE_AXIS = GQA_HEAD // 2
ROPE_THETA = 10000.0

X_HEADS = 4
X_HEAD = D_MODEL // X_HEADS

LOG2E = math.log2(math.e)

GQA_TQ = 512
GQA_TK = 4096
DIFF_TQ = 1024
DIFF_TK = 2048
ATTN_ROWS = 256
ATTN_KEYS = 256

LANES = 128
SOFTMAX_UNDERFLOW_LOG2 = 150.0
NORM_BOUND_MARGIN = 1.01
V7X_VMEM_BYTES = 64 * 1024 * 1024
VMEM_LIMIT_BYTES = V7X_VMEM_BYTES - 8 * 1024 * 1024

S5_CHUNK = 16
S5_LANE_GROUPS = LANES // S5_GROUP
S5_LANE_BLOCKS = S5_WIDTH // LANES
S5_BLOCK_COLS = S5_CHUNK * LANES
S5_BLOCK_STATE = S5_LANE_GROUPS * S5_STATE


def _tile(n, pref):
    t = min(pref, n)
    while n % t:
        t //= 2
    return t


def _cparams(*sem):
    return pltpu.CompilerParams(dimension_semantics=sem, vmem_limit_bytes=VMEM_LIMIT_BYTES)


def _rmsnorm(x, g, eps):
    ms = jnp.mean(x * x, axis=-1, keepdims=True)
    return x * lax.rsqrt(ms + eps) * g


def _sigmoid(x):
    return 1.0 / (1.0 + jnp.exp(-x))


def _gelu_tanh(x):
    c = math.sqrt(2.0 / math.pi)
    return x * (0.5 * (1.0 + jnp.tanh(c * (x + 0.044715 * (x * x * x)))))


def _ffn_kernel(x_ref, g_ref, wg_ref, wu_ref, wd_ref, *rest, n_f, final):
    if final:
        fg_ref, o_ref, h_ref, acc_ref = rest
    else:
        o_ref, h_ref, acc_ref = rest
    j = pl.program_id(1)

    @pl.when(j == 0)
    def _():
        h_ref[...] = _rmsnorm(x_ref[...], g_ref[...], EPS).astype(BF16)
        acc_ref[...] = jnp.zeros_like(acc_ref)

    h = h_ref[...]
    gate = jnp.dot(h, wg_ref[...], preferred_element_type=F32)
    up = jnp.dot(h, wu_ref[...], preferred_element_type=F32)
    act = gate * _sigmoid(gate) * up
    acc_ref[...] += jnp.dot(act.astype(BF16), wd_ref[...], preferred_element_type=F32)

    @pl.when(j == n_f - 1)
    def _():
        y = x_ref[...] + 0.5 * acc_ref[...]
        if final:
            y = _rmsnorm(y, fg_ref[...], EPS)
        o_ref[...] = y


def _ffn(x, g, w_gu, w_down, final_g=None):
    L = x.shape[0]
    tm = _tile(L, 512)
    tf = 512
    n_f = D_FF // tf
    final = final_g is not None
    in_specs = [
        pl.BlockSpec((tm, D_MODEL), lambda i, j: (i, 0)),
        pl.BlockSpec((1, D_MODEL), lambda i, j: (0, 0)),
        pl.BlockSpec((D_MODEL, tf), lambda i, j: (0, j)),
        pl.BlockSpec((D_MODEL, tf), lambda i, j: (0, j + n_f)),
        pl.BlockSpec((tf, D_MODEL), lambda i, j: (j, 0)),
    ]
    args = [x, g.reshape(1, D_MODEL), w_gu, w_gu, w_down]
    if final:
        in_specs.append(pl.BlockSpec((1, D_MODEL), lambda i, j: (0, 0)))
        args.append(final_g.reshape(1, D_MODEL))
    return pl.pallas_call(
        functools.partial(_ffn_kernel, n_f=n_f, final=final),
        grid=(L // tm, n_f),
        in_specs=in_specs,
        out_specs=pl.BlockSpec((tm, D_MODEL), lambda i, j: (i, 0)),
        out_shape=jax.ShapeDtypeStruct((L, D_MODEL), F32),
        scratch_shapes=[pltpu.VMEM((tm, D_MODEL), BF16), pltpu.VMEM((tm, D_MODEL), F32)],
        compiler_params=_cparams("parallel", "arbitrary"),
        name="ffn_final" if final else "ffn",
    )(*args)


def _norm_matmul_kernel(x_ref, g_ref, w_ref, *rest, scaled, tm, sub):
    if scaled:
        c_ref, o_ref = rest
    else:
        (o_ref,) = rest
    for c in range(tm // sub):
        rows = slice(c * sub, (c + 1) * sub)
        h = _rmsnorm(x_ref[rows, :], g_ref[...], EPS).astype(BF16)
        z = jnp.dot(h, w_ref[...], preferred_element_type=F32)
        if scaled:
            z = z * c_ref[...]
        o_ref[rows, :] = z.astype(o_ref.dtype)


def _norm_matmul(x, g, w, out_dtype, col_scale=None):
    L = x.shape[0]
    N = w.shape[1]
    tm = _tile(L, 512)
    scaled = col_scale is not None
    in_specs = [
        pl.BlockSpec((tm, D_MODEL), lambda i: (i, 0)),
        pl.BlockSpec((1, D_MODEL), lambda i: (0, 0)),
        pl.BlockSpec((D_MODEL, N), lambda i: (0, 0)),
    ]
    args = [x, g.reshape(1, D_MODEL), w]
    if scaled:
        in_specs.append(pl.BlockSpec((1, N), lambda i: (0, 0)))
        args.append(col_scale.reshape(1, N))
    return pl.pallas_call(
        functools.partial(_norm_matmul_kernel, scaled=scaled, tm=tm, sub=_tile(tm, 256)),
        grid=(L // tm,),
        in_specs=in_specs,
        out_specs=pl.BlockSpec((tm, N), lambda i: (i, 0)),
        out_shape=jax.ShapeDtypeStruct((L, N), out_dtype),
        compiler_params=_cparams("parallel"),
        name="norm_matmul",
    )(*args)


def _odd_proj_kernel(x_ref, g_ref, w_ref, qg_ref, kg_ref, cos_ref, sin_ref, o_ref, h_ref, *, heads_per_tile):
    j = pl.program_id(1)
    n_q_tiles = GQA_HEADS // heads_per_tile

    @pl.when(j == 0)
    def _():
        h_ref[...] = _rmsnorm(x_ref[...], g_ref[...], EPS).astype(BF16)

    z = jnp.dot(h_ref[...], w_ref[...], preferred_element_type=F32)
    lane = lax.broadcasted_iota(jnp.int32, (1, GQA_HEAD), 1)
    first_half = (lane & (ROPE_AXIS - 1)) < (ROPE_AXIS // 2)

    def normed_rotated(hh, gain, cos, sin):
        sl = slice(hh * GQA_HEAD, (hh + 1) * GQA_HEAD)
        y = _rmsnorm(z[:, sl], gain, EPS)
        partner = jnp.where(first_half, pltpu.roll(y, GQA_HEAD - ROPE_AXIS // 2, 1),
                            pltpu.roll(y, ROPE_AXIS // 2, 1))
        o_ref[:, sl] = (y * cos + partner * sin).astype(o_ref.dtype)

    @pl.when(j < n_q_tiles)
    def _():
        out_scale = GQA_HEAD ** -0.5 * LOG2E
        cos = cos_ref[...] * out_scale
        sin = sin_ref[...] * out_scale
        for hh in range(heads_per_tile):
            normed_rotated(hh, qg_ref[...], cos, sin)

    @pl.when(j >= n_q_tiles)
    def _():
        for hh in range(GQA_KV_HEADS):
            normed_rotated(hh, kg_ref[...], cos_ref[...], sin_ref[...])
        v_cols = slice(GQA_KV_HEADS * GQA_HEAD, 2 * GQA_KV_HEADS * GQA_HEAD)
        o_ref[:, v_cols] = z[:, v_cols].astype(o_ref.dtype)


def _odd_proj(x, g, w, q_g, k_g, cos, sin):
    L = x.shape[0]
    N = w.shape[1]
    tm = _tile(L, 512)
    heads_per_tile = 2 * GQA_KV_HEADS
    tn = heads_per_tile * GQA_HEAD
    return pl.pallas_call(
        functools.partial(_odd_proj_kernel, heads_per_tile=heads_per_tile),
        grid=(L // tm, N // tn),
        in_specs=[
            pl.BlockSpec((tm, D_MODEL), lambda i, j: (i, 0)),
            pl.BlockSpec((1, D_MODEL), lambda i, j: (0, 0)),
            pl.BlockSpec((D_MODEL, tn), lambda i, j: (0, j)),
            pl.BlockSpec((1, GQA_HEAD), lambda i, j: (0, 0)),
            pl.BlockSpec((1, GQA_HEAD), lambda i, j: (0, 0)),
            pl.BlockSpec((tm, GQA_HEAD), lambda i, j: (i, 0)),
            pl.BlockSpec((tm, GQA_HEAD), lambda i, j: (i, 0)),
        ],
        out_specs=pl.BlockSpec((tm, tn), lambda i, j: (i, j)),
        out_shape=jax.ShapeDtypeStruct((L, N), BF16),
        scratch_shapes=[pltpu.VMEM((tm, D_MODEL), BF16)],
        compiler_params=_cparams("parallel", "arbitrary"),
        name="odd_proj",
    )(x, g.reshape(1, D_MODEL), w, q_g.reshape(1, GQA_HEAD), k_g.reshape(1, GQA_HEAD), cos, sin)


def _matmul_residual_kernel(a_ref, w_ref, r_ref, o_ref):
    o_ref[...] = r_ref[...] + jnp.dot(a_ref[...], w_ref[...], preferred_element_type=F32)


def _matmul_residual(a, w, res):
    L, K = a.shape
    N = w.shape[1]
    tm = _tile(L, 512)
    return pl.pallas_call(
        _matmul_residual_kernel,
        grid=(L // tm,),
        in_specs=[
            pl.BlockSpec((tm, K), lambda i: (i, 0)),
            pl.BlockSpec((K, N), lambda i: (0, 0)),
            pl.BlockSpec((tm, N), lambda i: (i, 0)),
        ],
        out_specs=pl.BlockSpec((tm, N), lambda i: (i, 0)),
        out_shape=jax.ShapeDtypeStruct((L, N), F32),
        compiler_params=_cparams("parallel"),
        name="matmul_residual",
    )(a, w, res)


def _lane_tile(x, n):
    return jnp.concatenate([x] * n, axis=1) if n > 1 else x


def _softmax_block(t, shift, v_ones, m, acc):
    row_max = jnp.max(t, axis=1, keepdims=True)
    m_new = jnp.maximum(m, row_max if shift is None else row_max + shift)
    alpha = jnp.exp2(m - m_new)
    p = jnp.exp2((t - _lane_tile(m_new if shift is None else m_new - shift, t.shape[1] // LANES)).astype(BF16))
    acc = _lane_tile(alpha, 2) * acc + jnp.dot(p, v_ones, preferred_element_type=F32)
    return m_new, acc


def _softmax_init(m_ref, acc_ref, vo_ref):
    m_ref[...] = jnp.full_like(m_ref, -jnp.inf)
    acc_ref[...] = jnp.zeros_like(acc_ref)
    vo_ref[:, LANES:] = jnp.ones((vo_ref.shape[0], LANES), vo_ref.dtype)


def _softmax_result(acc):
    return acc[:, :LANES] / acc[:, LANES:]


_NT = (((1,), (1,)), ((), ()))


def _gqa_kernel(q_ref, k_ref, v_ref, o_ref, m_ref, acc_ref, vo_ref, *, n_k, tq, tk, rows, keys):
    ki = pl.program_id(2)

    @pl.when(ki == 0)
    def _():
        _softmax_init(m_ref, acc_ref, vo_ref)

    vo_ref[:, :LANES] = v_ref[...]
    for g in range(GQA_GROUP):
        for c in range(tq // rows):
            sl = slice(g * tq + c * rows, g * tq + (c + 1) * rows)
            q = q_ref[c * rows:(c + 1) * rows, g * GQA_HEAD:(g + 1) * GQA_HEAD]
            state = (m_ref[sl, :], acc_ref[sl, :])
            for kk in range(tk // keys):
                ksl = slice(kk * keys, (kk + 1) * keys)
                s = lax.dot_general(q, k_ref[ksl, :], _NT, preferred_element_type=F32)
                state = _softmax_block(s, None, vo_ref[ksl, :], *state)
            m_ref[sl, :], acc_ref[sl, :] = state

    @pl.when(ki == n_k - 1)
    def _():
        for g in range(GQA_GROUP):
            sl = slice(g * tq, (g + 1) * tq)
            o_ref[:, g * GQA_HEAD:(g + 1) * GQA_HEAD] = _softmax_result(acc_ref[sl, :]).astype(o_ref.dtype)


def _gqa_attention(qkv):
    L = qkv.shape[0]
    tq = _tile(L, GQA_TQ)
    tk = _tile(L, GQA_TK)
    n_k = L // tk
    group_cols = GQA_GROUP * GQA_HEAD
    return pl.pallas_call(
        functools.partial(_gqa_kernel, n_k=n_k, tq=tq, tk=tk, rows=_tile(tq, ATTN_ROWS), keys=_tile(tk, ATTN_KEYS)),
        grid=(GQA_KV_HEADS, L // tq, n_k),
        in_specs=[
            pl.BlockSpec((tq, group_cols), lambda h, i, j: (i, h)),
            pl.BlockSpec((tk, GQA_HEAD), lambda h, i, j: (j, GQA_HEADS + h)),
            pl.BlockSpec((tk, GQA_HEAD), lambda h, i, j: (j, GQA_HEADS + GQA_KV_HEADS + h)),
        ],
        out_specs=pl.BlockSpec((tq, group_cols), lambda h, i, j: (i, h)),
        out_shape=jax.ShapeDtypeStruct((L, D_MODEL), BF16),
        scratch_shapes=[
            pltpu.VMEM((GQA_GROUP * tq, LANES), F32),
            pltpu.VMEM((GQA_GROUP * tq, 2 * LANES), F32),
            pltpu.VMEM((tk, 2 * LANES), BF16),
        ],
        compiler_params=_cparams("parallel", "parallel", "arbitrary"),
        name="gqa_attention",
    )(qkv, qkv, qkv)


def _sq_norm_max_kernel(x_ref, o_ref, *, half):
    x = x_ref[...].astype(F32)
    x2 = x * x
    first = lax.broadcasted_iota(jnp.int32, (1, LANES), 1) < half
    blocks = []
    for b in range(x2.shape[1] // LANES):
        xb = x2[:, b * LANES:(b + 1) * LANES]
        n1 = jnp.max(jnp.sum(jnp.where(first, xb, 0.0), axis=1, keepdims=True), axis=0, keepdims=True)
        n2 = jnp.max(jnp.sum(jnp.where(first, 0.0, xb), axis=1, keepdims=True), axis=0, keepdims=True)
        blocks.append(jnp.where(first, n1, n2))
    cur = jnp.concatenate(blocks, axis=1)

    @pl.when(pl.program_id(0) == 0)
    def _():
        o_ref[...] = cur

    @pl.when(pl.program_id(0) > 0)
    def _():
        o_ref[...] = jnp.maximum(o_ref[...], cur)


def _sq_norm_max(x, n_cols, half):
    L = x.shape[0]
    tm = _tile(L, 512)
    return pl.pallas_call(
        functools.partial(_sq_norm_max_kernel, half=half),
        grid=(L // tm,),
        in_specs=[pl.BlockSpec((tm, n_cols), lambda i: (i, 0))],
        out_specs=pl.BlockSpec((1, n_cols), lambda i: (0, 0)),
        out_shape=jax.ShapeDtypeStruct((1, n_cols), F32),
        compiler_params=_cparams("arbitrary"),
        name="sq_norm_max",
    )(x)


def _diff_kernel(klo_ref, khi_ref, slope_ref, q_ref, k_ref, v_ref, lq1_ref, lk1_ref, lq2_ref, lk2_ref, sg_ref,
                 o_ref, qs_ref, m_ref, acc_ref, vo_ref, *, n_k, tq, tk, rows, keys, lambda_init):
    h = pl.program_id(0)
    qi = pl.program_id(1)
    j = pl.program_id(2)
    q0 = qi * tq
    kt = klo_ref[h, qi] + j
    active = kt <= khi_ref[h, qi]
    k0 = kt * tk
    n_c = tq // rows

    @pl.when(j == 0)
    def _():
        _softmax_init(m_ref, acc_ref, vo_ref)
        q = q_ref[...]
        lane = lax.broadcasted_iota(jnp.int32, (1, 2 * DIFF_HEAD), 1)
        zero = jnp.zeros_like(q)
        q1 = jnp.where(lane < DIFF_HEAD, q, zero)
        q2 = jnp.where(lane >= DIFF_HEAD, q, zero)
        for c in range(n_c):
            qs_ref[2 * c * rows:(2 * c + 1) * rows, :] = q1[c * rows:(c + 1) * rows]
            qs_ref[(2 * c + 1) * rows:(2 * c + 2) * rows, :] = q2[c * rows:(c + 1) * rows]

    slope2 = slope_ref[h] * LOG2E
    d0 = lax.broadcasted_iota(jnp.int32, (rows, keys), 0) - lax.broadcasted_iota(jnp.int32, (rows, keys), 1)

    def step(bias_fn):
        vo_ref[:, :LANES] = v_ref[...]
        for c in range(n_c):
            sl = slice(2 * c * rows, 2 * (c + 1) * rows)
            state = (m_ref[sl, :], acc_ref[sl, :])
            for kk in range(tk // keys):
                ksl = slice(kk * keys, (kk + 1) * keys)
                bias, shift = bias_fn(q0 + c * rows - (k0 + kk * keys))
                t = lax.dot_general(qs_ref[sl, :], k_ref[ksl, :], _NT, preferred_element_type=F32)
                t = t + jnp.concatenate([bias, bias], axis=0)
                state = _softmax_block(t, shift, vo_ref[ksl, :], *state)
            m_ref[sl, :], acc_ref[sl, :] = state

    crosses_diagonal = jnp.logical_and(k0 < q0 + tq, q0 < k0 + tk)

    @pl.when(jnp.logical_and(active, crosses_diagonal))
    def _():
        step(lambda off: ((-slope2) * jnp.abs(off + d0).astype(F32), None))

    @pl.when(jnp.logical_and(active, jnp.logical_not(crosses_diagonal)))
    def _():
        coef = jnp.where(q0 >= k0, -slope2, slope2)
        tile = coef * d0.astype(F32)
        step(lambda off: (tile, coef * off.astype(F32)))

    @pl.when(j == n_k - 1)
    def _():
        lam = (jnp.exp(jnp.sum(lq1_ref[...] * lk1_ref[...], axis=-1, keepdims=True))
               - jnp.exp(jnp.sum(lq2_ref[...] * lk2_ref[...], axis=-1, keepdims=True)) + lambda_init)
        for c in range(n_c):
            o1 = _softmax_result(acc_ref[2 * c * rows:(2 * c + 1) * rows, :])
            o2 = _softmax_result(acc_ref[(2 * c + 1) * rows:(2 * c + 2) * rows, :])
            o_ref[c * rows:(c + 1) * rows, :] = (
                _rmsnorm(o1 - lam * o2, sg_ref[...], DIFF_SUBLN_EPS) * (1.0 - lambda_init)).astype(o_ref.dtype)


def _diff_key_ranges(qkv, slopes, tq, tk):
    L = qkv.shape[0]
    n_q, n_k = L // tq, L // tk
    sq = _sq_norm_max(qkv, 2 * DIFF_WIDTH, DIFF_HEAD).reshape(2, DIFF_HEADS, 2, DIFF_HEAD)[..., 0]
    norms = jnp.sqrt(sq)
    bound = jnp.max(norms[0] * norms[1], axis=1) * NORM_BOUND_MARGIN
    radius = (SOFTMAX_UNDERFLOW_LOG2 + 2.0 * bound) / (slopes * LOG2E)
    radius = jnp.where(jnp.isfinite(radius), radius, float(L))[:, None]
    q_first = (jnp.arange(n_q, dtype=F32) * tq)[None, :]
    klo = jnp.clip(jnp.floor((q_first - radius) / tk), 0, n_k - 1).astype(jnp.int32)
    khi = jnp.clip(jnp.floor((q_first + (tq - 1) + radius) / tk), 0, n_k - 1).astype(jnp.int32)
    return klo, khi


def _diff_attention(qkv, lq1, lk1, lq2, lk2, subln_g, layer_idx):
    L = qkv.shape[0]
    tq = _tile(L, DIFF_TQ)
    tk = _tile(L, DIFF_TK)
    n_k = L // tk
    lambda_init = 0.8 - 0.6 * math.exp(-0.3 * layer_idx)
    slopes = jnp.asarray([2.0 ** (-8.0 * (i + 1) / DIFF_HEADS) for i in range(DIFF_HEADS)], dtype=F32)
    klo, khi = _diff_key_ranges(qkv, slopes, tq, tk)
    key_tile = lambda h, i, j, klo, khi: jnp.minimum(klo[h, i] + j, khi[h, i])
    vec = lambda: pl.BlockSpec((1, DIFF_HEAD), lambda h, i, j, klo, khi: (0, 0))
    return pl.pallas_call(
        functools.partial(_diff_kernel, n_k=n_k, tq=tq, tk=tk, rows=_tile(tq, ATTN_ROWS), keys=_tile(tk, ATTN_KEYS),
                          lambda_init=lambda_init),
        grid_spec=pltpu.PrefetchScalarGridSpec(
            num_scalar_prefetch=2,
            grid=(DIFF_HEADS, L // tq, n_k),
            in_specs=[
                pl.BlockSpec(memory_space=pltpu.SMEM),
                pl.BlockSpec((tq, DIFF_VHEAD), lambda h, i, j, klo, khi: (i, h)),
                pl.BlockSpec((tk, DIFF_VHEAD), lambda h, i, j, klo, khi: (key_tile(h, i, j, klo, khi), DIFF_HEADS + h)),
                pl.BlockSpec((tk, DIFF_VHEAD),
                             lambda h, i, j, klo, khi: (key_tile(h, i, j, klo, khi), 2 * DIFF_HEADS + h)),
                vec(), vec(), vec(), vec(),
                pl.BlockSpec((1, DIFF_VHEAD), lambda h, i, j, klo, khi: (0, 0)),
            ],
            out_specs=pl.BlockSpec((tq, DIFF_VHEAD), lambda h, i, j, klo, khi: (i, h)),
            scratch_shapes=[
                pltpu.VMEM((2 * tq, DIFF_VHEAD), BF16),
                pltpu.VMEM((2 * tq, LANES), F32),
                pltpu.VMEM((2 * tq, 2 * LANES), F32),
                pltpu.VMEM((tk, 2 * LANES), BF16),
            ],
        ),
        out_shape=jax.ShapeDtypeStruct((L, DIFF_WIDTH), BF16),
        compiler_params=_cparams("parallel", "parallel", "arbitrary"),
        name="diff_attention",
    )(klo, khi, slopes, qkv, qkv, qkv, lq1.reshape(1, -1), lk1.reshape(1, -1), lq2.reshape(1, -1),
      lk2.reshape(1, -1), subln_g.reshape(1, -1))


def _cross_kernel(q_ref, kv_ref, o_ref, *, scale):
    for h in range(X_HEADS):
        sl = slice(h * X_HEAD, (h + 1) * X_HEAD)
        vsl = slice(D_MODEL + h * X_HEAD, D_MODEL + (h + 1) * X_HEAD)
        s = lax.dot_general(q_ref[:, sl], kv_ref[:, sl], (((1,), (1,)), ((), ())),
                            preferred_element_type=F32) * scale
        e = jnp.exp(s - jnp.max(s, axis=-1, keepdims=True))
        p = e / jnp.sum(e, axis=-1, keepdims=True)
        o_ref[:, sl] = jnp.dot(p.astype(BF16), kv_ref[:, vsl], preferred_element_type=F32).astype(o_ref.dtype)


def _cross_attention(q, kv):
    L = q.shape[0]
    n_mem = kv.shape[0]
    tq = _tile(L, 512)
    return pl.pallas_call(
        functools.partial(_cross_kernel, scale=X_HEAD ** -0.5),
        grid=(L // tq,),
        in_specs=[
            pl.BlockSpec((tq, D_MODEL), lambda i: (i, 0)),
            pl.BlockSpec((n_mem, 2 * D_MODEL), lambda i: (0, 0)),
        ],
        out_specs=pl.BlockSpec((tq, D_MODEL), lambda i: (i, 0)),
        out_shape=jax.ShapeDtypeStruct((L, D_MODEL), BF16),
        compiler_params=_cparams("parallel"),
        name="cross_attention",
    )(q, kv)


def _s5_weights(lam_re, lam_im, log_dt, b_re, b_im, c_re, c_im):
    T = S5_CHUNK
    hi = lax.Precision.HIGHEST
    lr = jnp.minimum(lam_re, S5_LAMBDA_RE_MAX)
    li = lam_im
    dt = jnp.exp(log_dt)[..., None]
    mag = jnp.exp(lr * dt)
    ab_re = mag * jnp.cos(li * dt)
    ab_im = mag * jnp.sin(li * dt)
    nr = ab_re - 1.0
    den = lr * lr + li * li
    f_re = ((nr * lr + ab_im * li) / den)[..., None]
    f_im = ((ab_im * lr - nr * li) / den)[..., None]
    bb_re = f_re * b_re - f_im * b_im
    bb_im = f_re * b_im + f_im * b_re
    k = jnp.arange(T + 1, dtype=F32)[:, None, None, None]
    pmag = jnp.exp(lr * dt * k)
    pw_re = pmag * jnp.cos(li * dt * k)
    pw_im = pmag * jnp.sin(li * dt * k)
    ca_re = c_re * pw_re[:, :, :, None, :] - c_im * pw_im[:, :, :, None, :]
    ca_im = c_re * pw_im[:, :, :, None, :] + c_im * pw_re[:, :, :, None, :]
    eye = jnp.eye(S5_LANE_GROUPS, dtype=BF16)
    nb, ng = S5_LANE_BLOCKS, S5_LANE_GROUPS

    kern = (jnp.einsum('kdghp,dgpi->kdghi', ca_re[:T], bb_re, precision=hi)
            - jnp.einsum('kdghp,dgpi->kdghi', ca_im[:T], bb_im, precision=hi))
    ktab = jnp.concatenate([kern[1:, 1][::-1], (kern[0, 0] + kern[0, 1])[None], kern[1:, 0]], axis=0)
    idx = jnp.arange(T)[None, :] - jnp.arange(T)[:, None] + (T - 1)
    a6 = ktab[idx].reshape(T, T, nb, ng, S5_GROUP, S5_GROUP)
    toep = jnp.einsum('stcgoi,gk->csgitko', a6.astype(BF16), eye).reshape(nb, S5_BLOCK_COLS, S5_BLOCK_COLS)

    e_re = jnp.stack([pw_re[:T, 0][::-1], pw_re[:T, 1]], axis=1)[..., None]
    e_im = jnp.stack([pw_im[:T, 0][::-1], pw_im[:T, 1]], axis=1)[..., None]
    zb = jnp.stack([e_re * bb_re - e_im * bb_im, e_re * bb_im + e_im * bb_re], axis=2)
    z7 = zb.reshape(T, 2, 2, nb, ng, S5_STATE, S5_GROUP)
    w_in = jnp.einsum('sdrcgpi,gk->csgidrkp', z7.astype(BF16), eye).reshape(nb, S5_BLOCK_COLS, 4 * S5_BLOCK_STATE)

    cr = jnp.stack([ca_re[1:, 0], ca_re[1:, 1][::-1]], axis=1)
    ci = jnp.stack([ca_im[1:, 0], ca_im[1:, 1][::-1]], axis=1)
    c7 = jnp.stack([cr, -ci], axis=2).reshape(T, 2, 2, nb, ng, S5_GROUP, S5_STATE)
    w_out = jnp.einsum('tdrcgop,gk->cdrgptko', c7.astype(BF16), eye).reshape(nb, 4 * S5_BLOCK_STATE, S5_BLOCK_COLS)

    a4 = jnp.stack([pw_re[T], pw_im[T]], axis=1).reshape(2, 2, nb, ng, S5_STATE)
    a_t = a4.transpose(2, 0, 1, 3, 4).reshape(nb, 1, 4 * S5_BLOCK_STATE)
    return toep, w_in, w_out, a_t


def _s5_gather_chunks(u_ref, lhs_ref, tr):
    for t in range(S5_CHUNK):
        lhs_ref[:, t * LANES:(t + 1) * LANES] = u_ref[pl.ds(t, tr, stride=S5_CHUNK), :].astype(BF16)


def _s5_in_kernel(u_ref, w_ref, z_ref, lhs_ref, *, tr):
    _s5_gather_chunks(u_ref, lhs_ref, tr)
    z_ref[0] = jnp.dot(lhs_ref[...], w_ref[0], preferred_element_type=F32)


def _s5_out_kernel(u_ref, toep_ref, x_ref, w_ref, d_ref, y_ref, lhs_ref, *, tr):
    _s5_gather_chunks(u_ref, lhs_ref, tr)
    y = (jnp.dot(lhs_ref[...], toep_ref[0], preferred_element_type=F32)
         + jnp.dot(x_ref[0].astype(BF16), w_ref[0], preferred_element_type=F32))
    d = d_ref[...]
    for t in range(S5_CHUNK):
        rows = pl.ds(t, tr, stride=S5_CHUNK)
        y_ref[rows, :] = d * u_ref[rows, :] + y[:, t * LANES:(t + 1) * LANES]


def _s5_scan_kernel(z_ref, a_ref, x_ref, *, n_rows):
    n_s = S5_BLOCK_STATE
    a = a_ref[0]
    af_re, af_im, ab_re, ab_im = (a[:, i * n_s:(i + 1) * n_s] for i in range(4))

    def body(n, carry):
        f_re, f_im, b_re, b_im = carry
        rf = pl.ds(n, 1)
        rb = pl.ds(n_rows - 1 - n, 1)
        x_ref[0, rf, 0 * n_s:1 * n_s] = f_re
        x_ref[0, rf, 1 * n_s:2 * n_s] = f_im
        x_ref[0, rb, 2 * n_s:3 * n_s] = b_re
        x_ref[0, rb, 3 * n_s:4 * n_s] = b_im
        zf_re = z_ref[0, rf, 0 * n_s:1 * n_s]
        zf_im = z_ref[0, rf, 1 * n_s:2 * n_s]
        zb_re = z_ref[0, rb, 2 * n_s:3 * n_s]
        zb_im = z_ref[0, rb, 3 * n_s:4 * n_s]
        return (af_re * f_re - af_im * f_im + zf_re, af_re * f_im + af_im * f_re + zf_im,
                ab_re * b_re - ab_im * b_im + zb_re, ab_re * b_im + ab_im * b_re + zb_im)

    zero = jnp.zeros((1, n_s), F32)
    lax.fori_loop(0, n_rows, body, (zero, zero, zero, zero))


def _s5_glu_kernel(y_ref, w_ref, b_ref, o_ref):
    g = _gelu_tanh(y_ref[...])
    gate = jnp.dot(g.astype(BF16), w_ref[...], preferred_element_type=F32) + b_ref[...]
    o_ref[...] = (g * _sigmoid(gate)).astype(o_ref.dtype)


def _s5_mixer(u, s5w, d_skip, glu_w, glu_b):
    L = u.shape[0]
    toep, w_in, w_out, a_t = s5w
    nb = S5_LANE_BLOCKS
    n_rows = L // S5_CHUNK
    n_st = 4 * S5_BLOCK_STATE
    tr = _tile(n_rows, 256)
    tokens = pl.BlockSpec((tr * S5_CHUNK, LANES), lambda c, i: (i, c))
    blk = lambda cols: pl.BlockSpec((1, tr, cols), lambda c, i: (c, i, 0))
    wgt = lambda rows, cols: pl.BlockSpec((1, rows, cols), lambda c, i: (c, 0, 0))
    chunk_lhs = pltpu.VMEM((tr, S5_BLOCK_COLS), BF16)
    z = pl.pallas_call(
        functools.partial(_s5_in_kernel, tr=tr),
        grid=(nb, n_rows // tr),
        in_specs=[tokens, wgt(S5_BLOCK_COLS, n_st)],
        out_specs=blk(n_st),
        out_shape=jax.ShapeDtypeStruct((nb, n_rows, n_st), F32),
        scratch_shapes=[chunk_lhs],
        compiler_params=_cparams("parallel", "parallel"),
        name="s5_chunk_input",
    )(u, w_in)
    x = pl.pallas_call(
        functools.partial(_s5_scan_kernel, n_rows=n_rows),
        grid=(nb,),
        in_specs=[pl.BlockSpec((1, n_rows, n_st), lambda c: (c, 0, 0)),
                  pl.BlockSpec((1, 1, n_st), lambda c: (c, 0, 0))],
        out_specs=pl.BlockSpec((1, n_rows, n_st), lambda c: (c, 0, 0)),
        out_shape=jax.ShapeDtypeStruct((nb, n_rows, n_st), F32),
        compiler_params=_cparams("parallel"),
        name="s5_chunk_scan",
    )(z, a_t)
    y = pl.pallas_call(
        functools.partial(_s5_out_kernel, tr=tr),
        grid=(nb, n_rows // tr),
        in_specs=[tokens, wgt(S5_BLOCK_COLS, S5_BLOCK_COLS), blk(n_st), wgt(n_st, S5_BLOCK_COLS),
                  pl.BlockSpec((1, LANES), lambda c, i: (0, c))],
        out_specs=tokens,
        out_shape=jax.ShapeDtypeStruct((L, S5_WIDTH), F32),
        scratch_shapes=[chunk_lhs],
        compiler_params=_cparams("parallel", "parallel"),
        name="s5_chunk_output",
    )(u, toep, x, w_out, d_skip.reshape(1, -1))
    tm = _tile(L, 512)
    row = lambda: pl.BlockSpec((tm, S5_WIDTH), lambda i: (i, 0))
    vec = lambda: pl.BlockSpec((1, S5_WIDTH), lambda i: (0, 0))
    return pl.pallas_call(
        _s5_glu_kernel,
        grid=(L // tm,),
        in_specs=[row(), pl.BlockSpec((S5_WIDTH, S5_WIDTH), lambda i: (0, 0)), vec()],
        out_specs=row(),
        out_shape=jax.ShapeDtypeStruct((L, S5_WIDTH), BF16),
        compiler_params=_cparams("parallel"),
        name="s5_glu",
    )(y, glu_w, glu_b.reshape(1, -1))


def _rope_tables(L):
    rows = L // GRID_W
    r = jnp.repeat(jnp.arange(rows, dtype=F32), GRID_W)
    c = jnp.tile(jnp.arange(GRID_W, dtype=F32), rows)
    inv = ROPE_THETA ** (-jnp.arange(0, ROPE_AXIS, 2, dtype=F32) / ROPE_AXIS)
    ar = r[:, None] * inv
    ac = c[:, None] * inv
    cos = jnp.concatenate([jnp.cos(ar), jnp.cos(ar), jnp.cos(ac), jnp.cos(ac)], axis=-1)
    sin = jnp.concatenate([-jnp.sin(ar), jnp.sin(ar), -jnp.sin(ac), jnp.sin(ac)], axis=-1)
    return cos, sin


def _encoder(x, mem, p):
    L = x.shape[0]
    cos, sin = _rope_tables(L)
    for l in range(DEPTH):
        x = _ffn(x, p['ffn1_norm'][l], p['ffn1_w_gu'][l], p['ffn1_w_down'][l])
        if l % 2 == 0:
            e = l // 2
            u = _norm_matmul(x, p['mix_norm'][l], p['even_w_in_u'][e], F32)
            qkv = _norm_matmul(x, p['mix_norm'][l], p['even_w_in_qkv'][e], BF16, col_scale=p['diff_q_scale'])
            ya = _s5_mixer(u, p['s5'][e], p['s5_d'][e], p['s5_glu_w'][e], p['s5_glu_b'][e])
            yb = _diff_attention(qkv, p['diff_lambda_q1'][e], p['diff_lambda_k1'][e], p['diff_lambda_q2'][e],
                                 p['diff_lambda_k2'][e], p['diff_subln'][e], l)
            x = _matmul_residual(jnp.concatenate([ya, yb], axis=-1), p['even_w_out'][e], x)
        else:
            o = l // 2
            qkv = _odd_proj(x, p['mix_norm'][l], p['odd_w_in'][o], p['gqa_q_norm'][o], p['gqa_k_norm'][o], cos, sin)
            x = _matmul_residual(_gqa_attention(qkv), p['odd_w_out'][o], x)
        q = _norm_matmul(x, p['cross_norm'][l], p['cross_w_q'][l], BF16)
        kv = _norm_matmul(mem, p['mem_norm'][l], p['cross_w_kv'][l], BF16)
        x = _matmul_residual(_cross_attention(q, kv), p['cross_w_o'][l], x)
        x = _ffn(x, p['ffn2_norm'][l], p['ffn2_w_gu'][l], p['ffn2_w_down'][l],
                 final_g=p['final_norm'] if l == DEPTH - 1 else None)
    return x


def kernel(x_prompt, x_sample, mem_prompt, mem_sample, ffn1_norm, ffn1_w_gu, ffn1_w_down, mix_norm, even_w_in, even_w_out, s5_lambda_re, s5_lambda_im, s5_log_dt, s5_b_re, s5_b_im, s5_c_re, s5_c_im, s5_d, s5_glu_w, s5_glu_b, diff_lambda_q1, diff_lambda_k1, diff_lambda_q2, diff_lambda_k2, diff_subln, odd_w_in, odd_w_out, gqa_q_norm, gqa_k_norm, cross_norm, mem_norm, cross_w_q, cross_w_kv, cross_w_o, ffn2_norm, ffn2_w_gu, ffn2_w_down, final_norm):
    bf = lambda w: w.astype(BF16)
    diff_q_scale = jnp.concatenate([jnp.full((DIFF_WIDTH,), DIFF_HEAD ** -0.5 * LOG2E, F32),
                                    jnp.ones((2 * DIFF_WIDTH,), F32)])
    p = dict(
        ffn1_norm=ffn1_norm, ffn1_w_gu=bf(ffn1_w_gu), ffn1_w_down=bf(ffn1_w_down), mix_norm=mix_norm,
        even_w_in_u=bf(even_w_in[:, :, :S5_WIDTH]), even_w_in_qkv=bf(even_w_in[:, :, S5_WIDTH:]),
        diff_q_scale=diff_q_scale,
        even_w_out=bf(even_w_out),
        s5=[_s5_weights(s5_lambda_re[e], s5_lambda_im[e], s5_log_dt[e], s5_b_re[e], s5_b_im[e], s5_c_re[e], s5_c_im[e])
            for e in range(s5_lambda_re.shape[0])],
        s5_d=s5_d, s5_glu_w=bf(s5_glu_w), s5_glu_b=s5_glu_b,
        diff_lambda_q1=diff_lambda_q1, diff_lambda_k1=diff_lambda_k1, diff_lambda_q2=diff_lambda_q2,
        diff_lambda_k2=diff_lambda_k2, diff_subln=diff_subln,
        odd_w_in=bf(odd_w_in), odd_w_out=bf(odd_w_out), gqa_q_norm=gqa_q_norm, gqa_k_norm=gqa_k_norm,
        cross_norm=cross_norm, mem_norm=mem_norm, cross_w_q=bf(cross_w_q), cross_w_kv=bf(cross_w_kv),
        cross_w_o=bf(cross_w_o), ffn2_norm=ffn2_norm, ffn2_w_gu=bf(ffn2_w_gu), ffn2_w_down=bf(ffn2_w_down),
        final_norm=final_norm)
    outs = []
    for x, mem in ((x_prompt, mem_prompt), (x_sample, mem_sample)):
        outs.append(jnp.stack([_encoder(x[b], mem[b], p) for b in range(x.shape[0])]))
    return tuple(outs)
```

```python
import functools
import math

import jax
import jax.numpy as jnp
from jax import lax
from jax.experimental import pallas as pl
from jax.experimental.pallas import tpu as pltpu

F32 = jnp.float32
BF16 = jnp.bfloat16

D_MODEL = 2048
DEPTH = 4
EPS = 1e-6
D_FF = 5632
GRID_W = 64

S5_WIDTH = D_MODEL // 2
S5_GROUP = 16
S5_GROUPS = S5_WIDTH // S5_GROUP
S5_STATE = 64
S5_LAMBDA_RE_MAX = -1e-4

DIFF_WIDTH = D_MODEL - S5_WIDTH
DIFF_HEAD = 64
DIFF_HEADS = DIFF_WIDTH // (2 * DIFF_HEAD)
DIFF_VHEAD = 2 * DIFF_HEAD
DIFF_SUBLN_EPS = 1e-5

GQA_HEAD = 128
GQA_HEADS = D_MODEL // GQA_HEAD
GQA_KV_HEADS = 4
GQA_GROUP = GQA_HEADS // GQA_KV_HEADS
ROPE_AXIS = GQA_HEAD // 2
ROPE_THETA = 10000.0

X_HEADS = 4
X_HEAD = D_MODEL // X_HEADS

LOG2E = math.log2(math.e)

GQA_TQ = 512
GQA_TK = 4096
DIFF_TQ = 1024
DIFF_TK = 2048
ATTN_ROWS = 256
ATTN_KEYS = 256

LANES = 128
SOFTMAX_UNDERFLOW_LOG2 = 150.0
NORM_BOUND_MARGIN = 1.01
V7X_VMEM_BYTES = 64 * 1024 * 1024
VMEM_LIMIT_BYTES = V7X_VMEM_BYTES - 8 * 1024 * 1024

S5_CHUNK = 16
S5_LANE_GROUPS = LANES // S5_GROUP
S5_LANE_BLOCKS = S5_WIDTH // LANES
S5_BLOCK_COLS = S5_CHUNK * LANES
S5_BLOCK_STATE = S5_LANE_GROUPS * S5_STATE


def _tile(n, pref):
    t = min(pref, n)
    while n % t:
        t //= 2
    return t


def _cparams(*sem):
    return pltpu.CompilerParams(dimension_semantics=sem, vmem_limit_bytes=VMEM_LIMIT_BYTES)


def _rmsnorm(x, g, eps):
    ms = jnp.mean(x * x, axis=-1, keepdims=True)
    return x * lax.rsqrt(ms + eps) * g


def _sigmoid(x):
    return 1.0 / (1.0 + jnp.exp(-x))


def _gelu_tanh(x):
    c = math.sqrt(2.0 / math.pi)
    return x * (0.5 * (1.0 + jnp.tanh(c * (x + 0.044715 * (x * x * x)))))


def _ffn_kernel(x_ref, g_ref, wg_ref, wu_ref, wd_ref, *rest, n_f, final):
    if final:
        fg_ref, o_ref, h_ref, acc_ref = rest
    else:
        o_ref, h_ref, acc_ref = rest
    j = pl.program_id(1)

    @pl.when(j == 0)
    def _():
        h_ref[...] = _rmsnorm(x_ref[...], g_ref[...], EPS).astype(BF16)
        acc_ref[...] = jnp.zeros_like(acc_ref)

    h = h_ref[...]
    gate = jnp.dot(h, wg_ref[...], preferred_element_type=F32)
    up = jnp.dot(h, wu_ref[...], preferred_element_type=F32)
    act = gate * _sigmoid(gate) * up
    acc_ref[...] += jnp.dot(act.astype(BF16), wd_ref[...], preferred_element_type=F32)

    @pl.when(j == n_f - 1)
    def _():
        y = x_ref[...] + 0.5 * acc_ref[...]
        if final:
            y = _rmsnorm(y, fg_ref[...], EPS)
        o_ref[...] = y


def _ffn(x, g, w_gu, w_down, layer, final_g=None):
    L = x.shape[0]
    tm = _tile(L, 512)
    tf = 512
    n_f = D_FF // tf
    final = final_g is not None
    in_specs = [
        pl.BlockSpec((tm, D_MODEL), lambda i, j: (i, 0)),
        pl.BlockSpec((1, D_MODEL), lambda i, j: (0, 0)),
        pl.BlockSpec((None, D_MODEL, tf), lambda i, j: (layer, 0, j)),
        pl.BlockSpec((None, D_MODEL, tf), lambda i, j: (layer, 0, j + n_f)),
        pl.BlockSpec((None, tf, D_MODEL), lambda i, j: (layer, j, 0)),
    ]
    args = [x, g.reshape(1, D_MODEL), w_gu, w_gu, w_down]
    if final:
        in_specs.append(pl.BlockSpec((1, D_MODEL), lambda i, j: (0, 0)))
        args.append(final_g.reshape(1, D_MODEL))
    return pl.pallas_call(
        functools.partial(_ffn_kernel, n_f=n_f, final=final),
        grid=(L // tm, n_f),
        in_specs=in_specs,
        out_specs=pl.BlockSpec((tm, D_MODEL), lambda i, j: (i, 0)),
        out_shape=jax.ShapeDtypeStruct((L, D_MODEL), F32),
        scratch_shapes=[pltpu.VMEM((tm, D_MODEL), BF16), pltpu.VMEM((tm, D_MODEL), F32)],
        compiler_params=_cparams("parallel", "arbitrary"),
        name="ffn_final" if final else "ffn",
    )(*args)


def _norm_matmul_kernel(x_ref, g_ref, w_ref, *rest, scaled, tm, sub):
    if scaled:
        c_ref, o_ref = rest
    else:
        (o_ref,) = rest
    for c in range(tm // sub):
        rows = slice(c * sub, (c + 1) * sub)
        h = _rmsnorm(x_ref[rows, :], g_ref[...], EPS).astype(BF16)
        z = jnp.dot(h, w_ref[...], preferred_element_type=F32)
        if scaled:
            z = z * c_ref[...]
        o_ref[rows, :] = z.astype(o_ref.dtype)


def _norm_matmul(x, g, w, out_dtype, col_scale=None):
    L = x.shape[0]
    N = w.shape[1]
    tm = _tile(L, 512)
    scaled = col_scale is not None
    in_specs = [
        pl.BlockSpec((tm, D_MODEL), lambda i: (i, 0)),
        pl.BlockSpec((1, D_MODEL), lambda i: (0, 0)),
        pl.BlockSpec((D_MODEL, N), lambda i: (0, 0)),
    ]
    args = [x, g.reshape(1, D_MODEL), w]
    if scaled:
        in_specs.append(pl.BlockSpec((1, N), lambda i: (0, 0)))
        args.append(col_scale.reshape(1, N))
    return pl.pallas_call(
        functools.partial(_norm_matmul_kernel, scaled=scaled, tm=tm, sub=_tile(tm, 256)),
        grid=(L // tm,),
        in_specs=in_specs,
        out_specs=pl.BlockSpec((tm, N), lambda i: (i, 0)),
        out_shape=jax.ShapeDtypeStruct((L, N), out_dtype),
        compiler_params=_cparams("parallel"),
        name="norm_matmul",
    )(*args)


def _odd_proj_kernel(x_ref, g_ref, w_ref, qg_ref, kg_ref, cos_ref, sin_ref, o_ref, h_ref, *, heads_per_tile):
    j = pl.program_id(1)
    n_q_tiles = GQA_HEADS // heads_per_tile

    @pl.when(j == 0)
    def _():
        h_ref[...] = _rmsnorm(x_ref[...], g_ref[...], EPS).astype(BF16)

    z = jnp.dot(h_ref[...], w_ref[...], preferred_element_type=F32)
    lane = lax.broadcasted_iota(jnp.int32, (1, GQA_HEAD), 1)
    first_half = (lane & (ROPE_AXIS - 1)) < (ROPE_AXIS // 2)

    def normed_rotated(hh, gain, cos, sin):
        sl = slice(hh * GQA_HEAD, (hh + 1) * GQA_HEAD)
        y = _rmsnorm(z[:, sl], gain, EPS)
        partner = jnp.where(first_half, pltpu.roll(y, GQA_HEAD - ROPE_AXIS // 2, 1),
                            pltpu.roll(y, ROPE_AXIS // 2, 1))
        o_ref[:, sl] = (y * cos + partner * sin).astype(o_ref.dtype)

    @pl.when(j < n_q_tiles)
    def _():
        out_scale = GQA_HEAD ** -0.5 * LOG2E
        cos = cos_ref[...] * out_scale
        sin = sin_ref[...] * out_scale
        for hh in range(heads_per_tile):
            normed_rotated(hh, qg_ref[...], cos, sin)

    @pl.when(j >= n_q_tiles)
    def _():
        for hh in range(GQA_KV_HEADS):
            normed_rotated(hh, kg_ref[...], cos_ref[...], sin_ref[...])
        v_cols = slice(GQA_KV_HEADS * GQA_HEAD, 2 * GQA_KV_HEADS * GQA_HEAD)
        o_ref[:, v_cols] = z[:, v_cols].astype(o_ref.dtype)


def _odd_proj(x, g, w, q_g, k_g, cos, sin):
    L = x.shape[0]
    N = w.shape[1]
    tm = _tile(L, 512)
    heads_per_tile = 2 * GQA_KV_HEADS
    tn = heads_per_tile * GQA_HEAD
    return pl.pallas_call(
        functools.partial(_odd_proj_kernel, heads_per_tile=heads_per_tile),
        grid=(L // tm, N // tn),
        in_specs=[
            pl.BlockSpec((tm, D_MODEL), lambda i, j: (i, 0)),
            pl.BlockSpec((1, D_MODEL), lambda i, j: (0, 0)),
            pl.BlockSpec((D_MODEL, tn), lambda i, j: (0, j)),
            pl.BlockSpec((1, GQA_HEAD), lambda i, j: (0, 0)),
            pl.BlockSpec((1, GQA_HEAD), lambda i, j: (0, 0)),
            pl.BlockSpec((tm, GQA_HEAD), lambda i, j: (i, 0)),
            pl.BlockSpec((tm, GQA_HEAD), lambda i, j: (i, 0)),
        ],
        out_specs=pl.BlockSpec((tm, tn), lambda i, j: (i, j)),
        out_shape=jax.ShapeDtypeStruct((L, N), BF16),
        scratch_shapes=[pltpu.VMEM((tm, D_MODEL), BF16)],
        compiler_params=_cparams("parallel", "arbitrary"),
        name="odd_proj",
    )(x, g.reshape(1, D_MODEL), w, q_g.reshape(1, GQA_HEAD), k_g.reshape(1, GQA_HEAD), cos, sin)


def _matmul_residual_kernel(a_ref, w_ref, r_ref, o_ref):
    o_ref[...] = r_ref[...] + jnp.dot(a_ref[...], w_ref[...], preferred_element_type=F32)


def _matmul_residual(a, w, res):
    L, K = a.shape
    N = w.shape[1]
    tm = _tile(L, 512)
    return pl.pallas_call(
        _matmul_residual_kernel,
        grid=(L // tm,),
        in_specs=[
            pl.BlockSpec((tm, K), lambda i: (i, 0)),
            pl.BlockSpec((K, N), lambda i: (0, 0)),
            pl.BlockSpec((tm, N), lambda i: (i, 0)),
        ],
        out_specs=pl.BlockSpec((tm, N), lambda i: (i, 0)),
        out_shape=jax.ShapeDtypeStruct((L, N), F32),
        compiler_params=_cparams("parallel"),
        name="matmul_residual",
    )(a, w, res)


def _lane_tile(x, n):
    return jnp.concatenate([x] * n, axis=1) if n > 1 else x


def _softmax_block(t, shift, v_ones, m, acc):
    row_max = jnp.max(t, axis=1, keepdims=True)
    m_new = jnp.maximum(m, row_max if shift is None else row_max + shift)
    alpha = jnp.exp2(m - m_new)
    p = jnp.exp2((t - _lane_tile(m_new if shift is None else m_new - shift, t.shape[1] // LANES)).astype(BF16))
    acc = _lane_tile(alpha, 2) * acc + jnp.dot(p, v_ones, preferred_element_type=F32)
    return m_new, acc


def _softmax_init(m_ref, acc_ref, vo_ref):
    m_ref[...] = jnp.full_like(m_ref, -jnp.inf)
    acc_ref[...] = jnp.zeros_like(acc_ref)
    vo_ref[:, LANES:] = jnp.ones((vo_ref.shape[0], LANES), vo_ref.dtype)


def _softmax_result(acc):
    return acc[:, :LANES] / acc[:, LANES:]


_NT = (((1,), (1,)), ((), ()))


def _gqa_kernel(q_ref, k_ref, v_ref, o_ref, m_ref, acc_ref, vo_ref, *, n_k, tq, tk, rows, keys):
    ki = pl.program_id(2)

    @pl.when(ki == 0)
    def _():
        _softmax_init(m_ref, acc_ref, vo_ref)

    vo_ref[:, :LANES] = v_ref[...]
    for g in range(GQA_GROUP):
        for c in range(tq // rows):
            sl = slice(g * tq + c * rows, g * tq + (c + 1) * rows)
            q = q_ref[c * rows:(c + 1) * rows, g * GQA_HEAD:(g + 1) * GQA_HEAD]
            state = (m_ref[sl, :], acc_ref[sl, :])
            for kk in range(tk // keys):
                ksl = slice(kk * keys, (kk + 1) * keys)
                s = lax.dot_general(q, k_ref[ksl, :], _NT, preferred_element_type=F32)
                state = _softmax_block(s, None, vo_ref[ksl, :], *state)
            m_ref[sl, :], acc_ref[sl, :] = state

    @pl.when(ki == n_k - 1)
    def _():
        for g in range(GQA_GROUP):
            sl = slice(g * tq, (g + 1) * tq)
            o_ref[:, g * GQA_HEAD:(g + 1) * GQA_HEAD] = _softmax_result(acc_ref[sl, :]).astype(o_ref.dtype)


def _gqa_attention(qkv):
    L = qkv.shape[0]
    tq = _tile(L, GQA_TQ)
    tk = _tile(L, GQA_TK)
    n_k = L // tk
    group_cols = GQA_GROUP * GQA_HEAD
    return pl.pallas_call(
        functools.partial(_gqa_kernel, n_k=n_k, tq=tq, tk=tk, rows=_tile(tq, ATTN_ROWS), keys=_tile(tk, ATTN_KEYS)),
        grid=(GQA_KV_HEADS, L // tq, n_k),
        in_specs=[
            pl.BlockSpec((tq, group_cols), lambda h, i, j: (i, h)),
            pl.BlockSpec((tk, GQA_HEAD), lambda h, i, j: (j, GQA_HEADS + h)),
            pl.BlockSpec((tk, GQA_HEAD), lambda h, i, j: (j, GQA_HEADS + GQA_KV_HEADS + h)),
        ],
        out_specs=pl.BlockSpec((tq, group_cols), lambda h, i, j: (i, h)),
        out_shape=jax.ShapeDtypeStruct((L, D_MODEL), BF16),
        scratch_shapes=[
            pltpu.VMEM((GQA_GROUP * tq, LANES), F32),
            pltpu.VMEM((GQA_GROUP * tq, 2 * LANES), F32),
            pltpu.VMEM((tk, 2 * LANES), BF16),
        ],
        compiler_params=_cparams("parallel", "parallel", "arbitrary"),
        name="gqa_attention",
    )(qkv, qkv, qkv)


def _sq_norm_max_kernel(x_ref, o_ref, *, half):
    x = x_ref[...].astype(F32)
    x2 = x * x
    first = lax.broadcasted_iota(jnp.int32, (1, LANES), 1) < half
    blocks = []
    for b in range(x2.shape[1] // LANES):
        xb = x2[:, b * LANES:(b + 1) * LANES]
        n1 = jnp.max(jnp.sum(jnp.where(first, xb, 0.0), axis=1, keepdims=True), axis=0, keepdims=True)
        n2 = jnp.max(jnp.sum(jnp.where(first, 0.0, xb), axis=1, keepdims=True), axis=0, keepdims=True)
        blocks.append(jnp.where(first, n1, n2))
    cur = jnp.concatenate(blocks, axis=1)

    @pl.when(pl.program_id(0) == 0)
    def _():
        o_ref[...] = cur

    @pl.when(pl.program_id(0) > 0)
    def _():
        o_ref[...] = jnp.maximum(o_ref[...], cur)


def _sq_norm_max(x, n_cols, half):
    L = x.shape[0]
    tm = _tile(L, 512)
    return pl.pallas_call(
        functools.partial(_sq_norm_max_kernel, half=half),
        grid=(L // tm,),
        in_specs=[pl.BlockSpec((tm, n_cols), lambda i: (i, 0))],
        out_specs=pl.BlockSpec((1, n_cols), lambda i: (0, 0)),
        out_shape=jax.ShapeDtypeStruct((1, n_cols), F32),
        compiler_params=_cparams("arbitrary"),
        name="sq_norm_max",
    )(x)


def _diff_kernel(klo_ref, khi_ref, slope_ref, q_ref, k_ref, v_ref, lq1_ref, lk1_ref, lq2_ref, lk2_ref, sg_ref,
                 o_ref, qs_ref, m_ref, acc_ref, vo_ref, *, n_k, tq, tk, rows, keys, lambda_init):
    h = pl.program_id(0)
    qi = pl.program_id(1)
    j = pl.program_id(2)
    q0 = qi * tq
    kt = klo_ref[h, qi] + j
    active = kt <= khi_ref[h, qi]
    k0 = kt * tk
    n_c = tq // rows

    @pl.when(j == 0)
    def _():
        _softmax_init(m_ref, acc_ref, vo_ref)
        q = q_ref[...]
        lane = lax.broadcasted_iota(jnp.int32, (1, 2 * DIFF_HEAD), 1)
        zero = jnp.zeros_like(q)
        q1 = jnp.where(lane < DIFF_HEAD, q, zero)
        q2 = jnp.where(lane >= DIFF_HEAD, q, zero)
        for c in range(n_c):
            qs_ref[2 * c * rows:(2 * c + 1) * rows, :] = q1[c * rows:(c + 1) * rows]
            qs_ref[(2 * c + 1) * rows:(2 * c + 2) * rows, :] = q2[c * rows:(c + 1) * rows]

    slope2 = slope_ref[h] * LOG2E
    d0 = lax.broadcasted_iota(jnp.int32, (rows, keys), 0) - lax.broadcasted_iota(jnp.int32, (rows, keys), 1)

    def step(bias_fn):
        vo_ref[:, :LANES] = v_ref[...]
        for c in range(n_c):
            sl = slice(2 * c * rows, 2 * (c + 1) * rows)
            state = (m_ref[sl, :], acc_ref[sl, :])
            for kk in range(tk // keys):
                ksl = slice(kk * keys, (kk + 1) * keys)
                bias, shift = bias_fn(q0 + c * rows - (k0 + kk * keys))
                t = lax.dot_general(qs_ref[sl, :], k_ref[ksl, :], _NT, preferred_element_type=F32)
                t = t + jnp.concatenate([bias, bias], axis=0)
                state = _softmax_block(t, shift, vo_ref[ksl, :], *state)
            m_ref[sl, :], acc_ref[sl, :] = state

    crosses_diagonal = jnp.logical_and(k0 < q0 + tq, q0 < k0 + tk)

    @pl.when(jnp.logical_and(active, crosses_diagonal))
    def _():
        step(lambda off: ((-slope2) * jnp.abs(off + d0).astype(F32), None))

    @pl.when(jnp.logical_and(active, jnp.logical_not(crosses_diagonal)))
    def _():
        coef = jnp.where(q0 >= k0, -slope2, slope2)
        tile = coef * d0.astype(F32)
        step(lambda off: (tile, coef * off.astype(F32)))

    @pl.when(j == n_k - 1)
    def _():
        lam = (jnp.exp(jnp.sum(lq1_ref[...] * lk1_ref[...], axis=-1, keepdims=True))
               - jnp.exp(jnp.sum(lq2_ref[...] * lk2_ref[...], axis=-1, keepdims=True)) + lambda_init)
        for c in range(n_c):
            o1 = _softmax_result(acc_ref[2 * c * rows:(2 * c + 1) * rows, :])
            o2 = _softmax_result(acc_ref[(2 * c + 1) * rows:(2 * c + 2) * rows, :])
            o_ref[c * rows:(c + 1) * rows, :] = (
                _rmsnorm(o1 - lam * o2, sg_ref[...], DIFF_SUBLN_EPS) * (1.0 - lambda_init)).astype(o_ref.dtype)


def _diff_key_ranges(qkv, slopes, tq, tk):
    L = qkv.shape[0]
    n_q, n_k = L // tq, L // tk
    sq = _sq_norm_max(qkv, 2 * DIFF_WIDTH, DIFF_HEAD).reshape(2, DIFF_HEADS, 2, DIFF_HEAD)[..., 0]
    norms = jnp.sqrt(sq)
    bound = jnp.max(norms[0] * norms[1], axis=1) * NORM_BOUND_MARGIN
    radius = (SOFTMAX_UNDERFLOW_LOG2 + 2.0 * bound) / (slopes * LOG2E)
    radius = jnp.where(jnp.isfinite(radius), radius, float(L))[:, None]
    q_first = (jnp.arange(n_q, dtype=F32) * tq)[None, :]
    klo = jnp.clip(jnp.floor((q_first - radius) / tk), 0, n_k - 1).astype(jnp.int32)
    khi = jnp.clip(jnp.floor((q_first + (tq - 1) + radius) / tk), 0, n_k - 1).astype(jnp.int32)
    return klo, khi


def _diff_attention(qkv, lq1, lk1, lq2, lk2, subln_g, layer_idx):
    L = qkv.shape[0]
    tq = _tile(L, DIFF_TQ)
    tk = _tile(L, DIFF_TK)
    n_k = L // tk
    lambda_init = 0.8 - 0.6 * math.exp(-0.3 * layer_idx)
    slopes = jnp.asarray([2.0 ** (-8.0 * (i + 1) / DIFF_HEADS) for i in range(DIFF_HEADS)], dtype=F32)
    klo, khi = _diff_key_ranges(qkv, slopes, tq, tk)
    key_tile = lambda h, i, j, klo, khi: jnp.minimum(klo[h, i] + j, khi[h, i])
    vec = lambda: pl.BlockSpec((1, DIFF_HEAD), lambda h, i, j, klo, khi: (0, 0))
    return pl.pallas_call(
        functools.partial(_diff_kernel, n_k=n_k, tq=tq, tk=tk, rows=_tile(tq, ATTN_ROWS), keys=_tile(tk, ATTN_KEYS),
                          lambda_init=lambda_init),
        grid_spec=pltpu.PrefetchScalarGridSpec(
            num_scalar_prefetch=2,
            grid=(DIFF_HEADS, L // tq, n_k),
            in_specs=[
                pl.BlockSpec(memory_space=pltpu.SMEM),
                pl.BlockSpec((tq, DIFF_VHEAD), lambda h, i, j, klo, khi: (i, h)),
                pl.BlockSpec((tk, DIFF_VHEAD), lambda h, i, j, klo, khi: (key_tile(h, i, j, klo, khi), DIFF_HEADS + h)),
                pl.BlockSpec((tk, DIFF_VHEAD),
                             lambda h, i, j, klo, khi: (key_tile(h, i, j, klo, khi), 2 * DIFF_HEADS + h)),
                vec(), vec(), vec(), vec(),
                pl.BlockSpec((1, DIFF_VHEAD), lambda h, i, j, klo, khi: (0, 0)),
            ],
            out_specs=pl.BlockSpec((tq, DIFF_VHEAD), lambda h, i, j, klo, khi: (i, h)),
            scratch_shapes=[
                pltpu.VMEM((2 * tq, DIFF_VHEAD), BF16),
                pltpu.VMEM((2 * tq, LANES), F32),
                pltpu.VMEM((2 * tq, 2 * LANES), F32),
                pltpu.VMEM((tk, 2 * LANES), BF16),
            ],
        ),
        out_shape=jax.ShapeDtypeStruct((L, DIFF_WIDTH), BF16),
        compiler_params=_cparams("parallel", "parallel", "arbitrary"),
        name="diff_attention",
    )(klo, khi, slopes, qkv, qkv, qkv, lq1.reshape(1, -1), lk1.reshape(1, -1), lq2.reshape(1, -1),
      lk2.reshape(1, -1), subln_g.reshape(1, -1))


def _cross_kernel(q_ref, kv_ref, o_ref, *, scale):
    for h in range(X_HEADS):
        sl = slice(h * X_HEAD, (h + 1) * X_HEAD)
        vsl = slice(D_MODEL + h * X_HEAD, D_MODEL + (h + 1) * X_HEAD)
        s = lax.dot_general(q_ref[:, sl], kv_ref[:, sl], (((1,), (1,)), ((), ())),
                            preferred_element_type=F32) * scale
        e = jnp.exp(s - jnp.max(s, axis=-1, keepdims=True))
        p = e / jnp.sum(e, axis=-1, keepdims=True)
        o_ref[:, sl] = jnp.dot(p.astype(BF16), kv_ref[:, vsl], preferred_element_type=F32).astype(o_ref.dtype)


def _cross_attention(q, kv):
    L = q.shape[0]
    n_mem = kv.shape[0]
    tq = _tile(L, 512)
    return pl.pallas_call(
        functools.partial(_cross_kernel, scale=X_HEAD ** -0.5),
        grid=(L // tq,),
        in_specs=[
            pl.BlockSpec((tq, D_MODEL), lambda i: (i, 0)),
            pl.BlockSpec((n_mem, 2 * D_MODEL), lambda i: (0, 0)),
        ],
        out_specs=pl.BlockSpec((tq, D_MODEL), lambda i: (i, 0)),
        out_shape=jax.ShapeDtypeStruct((L, D_MODEL), BF16),
        compiler_params=_cparams("parallel"),
        name="cross_attention",
    )(q, kv)


def _s5_weights(lam_re, lam_im, log_dt, b_re, b_im, c_re, c_im):
    T = S5_CHUNK
    hi = lax.Precision.HIGHEST
    lr = jnp.minimum(lam_re, S5_LAMBDA_RE_MAX)
    li = lam_im
    dt = jnp.exp(log_dt)[..., None]
    mag = jnp.exp(lr * dt)
    ab_re = mag * jnp.cos(li * dt)
    ab_im = mag * jnp.sin(li * dt)
    nr = ab_re - 1.0
    den = lr * lr + li * li
    f_re = ((nr * lr + ab_im * li) / den)[..., None]
    f_im = ((ab_im * lr - nr * li) / den)[..., None]
    bb_re = f_re * b_re - f_im * b_im
    bb_im = f_re * b_im + f_im * b_re
    k = jnp.arange(T + 1, dtype=F32)[:, None, None, None]
    pmag = jnp.exp(lr * dt * k)
    pw_re = pmag * jnp.cos(li * dt * k)
    pw_im = pmag * jnp.sin(li * dt * k)
    ca_re = c_re * pw_re[:, :, :, None, :] - c_im * pw_im[:, :, :, None, :]
    ca_im = c_re * pw_im[:, :, :, None, :] + c_im * pw_re[:, :, :, None, :]
    nb, ng = S5_LANE_BLOCKS, S5_LANE_GROUPS

    def group_diagonal(x, rows_per_group):
        w = x.shape[-1]
        reps = jnp.tile(x.astype(BF16), (1, 1, 1, ng))
        row_group = (jnp.arange(x.shape[1]) // rows_per_group) % ng
        col_group = jnp.arange(ng * w) // w
        keep = (row_group[:, None] == col_group[None, :])[None, :, None, :]
        return jnp.where(keep, reps, jnp.zeros((), BF16))

    kern = (jnp.einsum('kdghp,dgpi->kdghi', ca_re[:T], bb_re, precision=hi)
            - jnp.einsum('kdghp,dgpi->kdghi', ca_im[:T], bb_im, precision=hi))
    ktab = jnp.concatenate([kern[1:, 1][::-1], (kern[0, 0] + kern[0, 1])[None], kern[1:, 0]], axis=0)
    kblk = ktab.reshape(2 * T - 1, nb, ng, S5_GROUP, S5_GROUP).transpose(0, 1, 2, 4, 3)
    kblk = group_diagonal(kblk.reshape((2 * T - 1) * nb, LANES, 1, S5_GROUP), S5_GROUP)
    kblk = kblk.reshape(2 * T - 1, nb, LANES, LANES)
    idx = jnp.arange(T)[None, :] - jnp.arange(T)[:, None] + (T - 1)
    toep = kblk[idx].transpose(2, 0, 3, 1, 4).reshape(nb, S5_BLOCK_COLS, S5_BLOCK_COLS)

    e_re = jnp.stack([pw_re[:T, 0][::-1], pw_re[:T, 1]], axis=1)[..., None]
    e_im = jnp.stack([pw_im[:T, 0][::-1], pw_im[:T, 1]], axis=1)[..., None]
    zb = jnp.stack([e_re * bb_re - e_im * bb_im, e_re * bb_im + e_im * bb_re], axis=2)
    z_rows = zb.reshape(T, 2, 2, nb, ng, S5_STATE, S5_GROUP).transpose(3, 0, 4, 6, 1, 2, 5)
    w_in = group_diagonal(z_rows.reshape(nb, S5_BLOCK_COLS, 4, S5_STATE), S5_GROUP)
    w_in = w_in.reshape(nb, S5_BLOCK_COLS, 4 * S5_BLOCK_STATE)

    cr = jnp.stack([ca_re[1:, 0], ca_re[1:, 1][::-1]], axis=1)
    ci = jnp.stack([ca_im[1:, 0], ca_im[1:, 1][::-1]], axis=1)
    c_rows = jnp.stack([cr, -ci], axis=2).reshape(T, 2, 2, nb, ng, S5_GROUP, S5_STATE).transpose(3, 1, 2, 4, 6, 0, 5)
    w_out = group_diagonal(c_rows.reshape(nb, 4 * S5_BLOCK_STATE, T, S5_GROUP), S5_STATE)
    w_out = w_out.reshape(nb, 4 * S5_BLOCK_STATE, S5_BLOCK_COLS)

    a4 = jnp.stack([pw_re[T], pw_im[T]], axis=1).reshape(2, 2, nb, ng, S5_STATE)
    a_t = a4.transpose(2, 0, 1, 3, 4).reshape(nb, 1, 4 * S5_BLOCK_STATE)
    return toep, w_in, w_out, a_t


def _s5_gather_chunks(u_ref, lhs_ref, tr):
    for t in range(S5_CHUNK):
        lhs_ref[:, t * LANES:(t + 1) * LANES] = u_ref[pl.ds(t, tr, stride=S5_CHUNK), :].astype(BF16)


def _s5_in_kernel(u_ref, w_ref, z_ref, lhs_ref, *, tr):
    _s5_gather_chunks(u_ref, lhs_ref, tr)
    z_ref[0] = jnp.dot(lhs_ref[...], w_ref[0], preferred_element_type=F32)


def _s5_out_kernel(u_ref, toep_ref, x_ref, w_ref, d_ref, y_ref, lhs_ref, *, tr):
    _s5_gather_chunks(u_ref, lhs_ref, tr)
    y = (jnp.dot(lhs_ref[...], toep_ref[0], preferred_element_type=F32)
         + jnp.dot(x_ref[0].astype(BF16), w_ref[0], preferred_element_type=F32))
    d = d_ref[...]
    for t in range(S5_CHUNK):
        rows = pl.ds(t, tr, stride=S5_CHUNK)
        y_ref[rows, :] = d * u_ref[rows, :] + y[:, t * LANES:(t + 1) * LANES]


def _s5_scan_kernel(z_ref, a_ref, x_ref, *, n_rows):
    n_s = S5_BLOCK_STATE
    a = a_ref[0]
    af_re, af_im, ab_re, ab_im = (a[:, i * n_s:(i + 1) * n_s] for i in range(4))

    def body(n, carry):
        f_re, f_im, b_re, b_im = carry
        rf = pl.ds(n, 1)
        rb = pl.ds(n_rows - 1 - n, 1)
        x_ref[0, rf, 0 * n_s:1 * n_s] = f_re
        x_ref[0, rf, 1 * n_s:2 * n_s] = f_im
        x_ref[0, rb, 2 * n_s:3 * n_s] = b_re
        x_ref[0, rb, 3 * n_s:4 * n_s] = b_im
        zf_re = z_ref[0, rf, 0 * n_s:1 * n_s]
        zf_im = z_ref[0, rf, 1 * n_s:2 * n_s]
        zb_re = z_ref[0, rb, 2 * n_s:3 * n_s]
        zb_im = z_ref[0, rb, 3 * n_s:4 * n_s]
        return (af_re * f_re - af_im * f_im + zf_re, af_re * f_im + af_im * f_re + zf_im,
                ab_re * b_re - ab_im * b_im + zb_re, ab_re * b_im + ab_im * b_re + zb_im)

    zero = jnp.zeros((1, n_s), F32)
    lax.fori_loop(0, n_rows, body, (zero, zero, zero, zero))


def _s5_glu_kernel(y_ref, w_ref, b_ref, o_ref):
    g = _gelu_tanh(y_ref[...])
    gate = jnp.dot(g.astype(BF16), w_ref[...], preferred_element_type=F32) + b_ref[...]
    o_ref[...] = (g * _sigmoid(gate)).astype(o_ref.dtype)


def _s5_mixer(u, s5w, d_skip, glu_w, glu_b):
    L = u.shape[0]
    toep, w_in, w_out, a_t = s5w
    nb = S5_LANE_BLOCKS
    n_rows = L // S5_CHUNK
    n_st = 4 * S5_BLOCK_STATE
    tr = _tile(n_rows, 256)
    tokens = pl.BlockSpec((tr * S5_CHUNK, LANES), lambda c, i: (i, c))
    blk = lambda cols: pl.BlockSpec((1, tr, cols), lambda c, i: (c, i, 0))
    wgt = lambda rows, cols: pl.BlockSpec((1, rows, cols), lambda c, i: (c, 0, 0))
    chunk_lhs = pltpu.VMEM((tr, S5_BLOCK_COLS), BF16)
    z = pl.pallas_call(
        functools.partial(_s5_in_kernel, tr=tr),
        grid=(nb, n_rows // tr),
        in_specs=[tokens, wgt(S5_BLOCK_COLS, n_st)],
        out_specs=blk(n_st),
        out_shape=jax.ShapeDtypeStruct((nb, n_rows, n_st), F32),
        scratch_shapes=[chunk_lhs],
        compiler_params=_cparams("parallel", "parallel"),
        name="s5_chunk_input",
    )(u, w_in)
    x = pl.pallas_call(
        functools.partial(_s5_scan_kernel, n_rows=n_rows),
        grid=(nb,),
        in_specs=[pl.BlockSpec((1, n_rows, n_st), lambda c: (c, 0, 0)),
                  pl.BlockSpec((1, 1, n_st), lambda c: (c, 0, 0))],
        out_specs=pl.BlockSpec((1, n_rows, n_st), lambda c: (c, 0, 0)),
        out_shape=jax.ShapeDtypeStruct((nb, n_rows, n_st), F32),
        compiler_params=_cparams("parallel"),
        name="s5_chunk_scan",
    )(z, a_t)
    y = pl.pallas_call(
        functools.partial(_s5_out_kernel, tr=tr),
        grid=(nb, n_rows // tr),
        in_specs=[tokens, wgt(S5_BLOCK_COLS, S5_BLOCK_COLS), blk(n_st), wgt(n_st, S5_BLOCK_COLS),
                  pl.BlockSpec((1, LANES), lambda c, i: (0, c))],
        out_specs=tokens,
        out_shape=jax.ShapeDtypeStruct((L, S5_WIDTH), F32),
        scratch_shapes=[chunk_lhs],
        compiler_params=_cparams("parallel", "parallel"),
        name="s5_chunk_output",
    )(u, toep, x, w_out, d_skip.reshape(1, -1))
    tm = _tile(L, 512)
    row = lambda: pl.BlockSpec((tm, S5_WIDTH), lambda i: (i, 0))
    vec = lambda: pl.BlockSpec((1, S5_WIDTH), lambda i: (0, 0))
    return pl.pallas_call(
        _s5_glu_kernel,
        grid=(L // tm,),
        in_specs=[row(), pl.BlockSpec((S5_WIDTH, S5_WIDTH), lambda i: (0, 0)), vec()],
        out_specs=row(),
        out_shape=jax.ShapeDtypeStruct((L, S5_WIDTH), BF16),
        compiler_params=_cparams("parallel"),
        name="s5_glu",
    )(y, glu_w, glu_b.reshape(1, -1))


def _rope_tables(L):
    rows = L // GRID_W
    r = jnp.repeat(jnp.arange(rows, dtype=F32), GRID_W)
    c = jnp.tile(jnp.arange(GRID_W, dtype=F32), rows)
    inv = ROPE_THETA ** (-jnp.arange(0, ROPE_AXIS, 2, dtype=F32) / ROPE_AXIS)
    ar = r[:, None] * inv
    ac = c[:, None] * inv
    cos = jnp.concatenate([jnp.cos(ar), jnp.cos(ar), jnp.cos(ac), jnp.cos(ac)], axis=-1)
    sin = jnp.concatenate([-jnp.sin(ar), jnp.sin(ar), -jnp.sin(ac), jnp.sin(ac)], axis=-1)
    return cos, sin


def _encoder(x, mem, p):
    L = x.shape[0]
    cos, sin = _rope_tables(L)
    for l in range(DEPTH):
        x = _ffn(x, p['ffn1_norm'][l], p['ffn1_w_gu'], p['ffn1_w_down'], l)
        if l % 2 == 0:
            e = l // 2
            u = _norm_matmul(x, p['mix_norm'][l], p['even_w_in_u'][e], F32)
            qkv = _norm_matmul(x, p['mix_norm'][l], p['even_w_in_qkv'][e], BF16, col_scale=p['diff_q_scale'])
            ya = _s5_mixer(u, p['s5'][e], p['s5_d'][e], p['s5_glu_w'][e], p['s5_glu_b'][e])
            yb = _diff_attention(qkv, p['diff_lambda_q1'][e], p['diff_lambda_k1'][e], p['diff_lambda_q2'][e],
                                 p['diff_lambda_k2'][e], p['diff_subln'][e], l)
            x = _matmul_residual(jnp.concatenate([ya, yb], axis=-1), p['even_w_out'][e], x)
        else:
            o = l // 2
            qkv = _odd_proj(x, p['mix_norm'][l], p['odd_w_in'][o], p['gqa_q_norm'][o], p['gqa_k_norm'][o], cos, sin)
            x = _matmul_residual(_gqa_attention(qkv), p['odd_w_out'][o], x)
        q = _norm_matmul(x, p['cross_norm'][l], p['cross_w_q'][l], BF16)
        kv = _norm_matmul(mem, p['mem_norm'][l], p['cross_w_kv'][l], BF16)
        x = _matmul_residual(_cross_attention(q, kv), p['cross_w_o'][l], x)
        x = _ffn(x, p['ffn2_norm'][l], p['ffn2_w_gu'], p['ffn2_w_down'], l,
                 final_g=p['final_norm'] if l == DEPTH - 1 else None)
    return x


def kernel(x_prompt, x_sample, mem_prompt, mem_sample, ffn1_norm, ffn1_w_gu, ffn1_w_down, mix_norm, even_w_in, even_w_out, s5_lambda_re, s5_lambda_im, s5_log_dt, s5_b_re, s5_b_im, s5_c_re, s5_c_im, s5_d, s5_glu_w, s5_glu_b, diff_lambda_q1, diff_lambda_k1, diff_lambda_q2, diff_lambda_k2, diff_subln, odd_w_in, odd_w_out, gqa_q_norm, gqa_k_norm, cross_norm, mem_norm, cross_w_q, cross_w_kv, cross_w_o, ffn2_norm, ffn2_w_gu, ffn2_w_down, final_norm):
    bf = lambda w: w.astype(BF16)
    diff_q_scale = jnp.concatenate([jnp.full((DIFF_WIDTH,), DIFF_HEAD ** -0.5 * LOG2E, F32),
                                    jnp.ones((2 * DIFF_WIDTH,), F32)])
    p = dict(
        ffn1_norm=ffn1_norm, ffn1_w_gu=bf(ffn1_w_gu), ffn1_w_down=bf(ffn1_w_down), mix_norm=mix_norm,
        even_w_in_u=bf(even_w_in[:, :, :S5_WIDTH]), even_w_in_qkv=bf(even_w_in[:, :, S5_WIDTH:]),
        diff_q_scale=diff_q_scale,
        even_w_out=bf(even_w_out),
        s5=[_s5_weights(s5_lambda_re[e], s5_lambda_im[e], s5_log_dt[e], s5_b_re[e], s5_b_im[e], s5_c_re[e], s5_c_im[e])
            for e in range(s5_lambda_re.shape[0])],
        s5_d=s5_d, s5_glu_w=bf(s5_glu_w), s5_glu_b=s5_glu_b,
        diff_lambda_q1=diff_lambda_q1, diff_lambda_k1=diff_lambda_k1, diff_lambda_q2=diff_lambda_q2,
        diff_lambda_k2=diff_lambda_k2, diff_subln=diff_subln,
        odd_w_in=bf(odd_w_in), odd_w_out=bf(odd_w_out), gqa_q_norm=gqa_q_norm, gqa_k_norm=gqa_k_norm,
        cross_norm=cross_norm, mem_norm=mem_norm, cross_w_q=bf(cross_w_q), cross_w_kv=bf(cross_w_kv),
        cross_w_o=bf(cross_w_o), ffn2_norm=ffn2_norm, ffn2_w_gu=bf(ffn2_w_gu), ffn2_w_down=bf(ffn2_w_down),
        final_norm=final_norm)
    outs = []
    for x, mem in ((x_prompt, mem_prompt), (x_sample, mem_sample)):
        outs.append(jnp.stack([_encoder(x[b], mem[b], p) for b in range(x.shape[0])]))
    return tuple(outs)
```

```python
import functools
import math

import jax
import jax.numpy as jnp
from jax import lax
from jax.experimental import pallas as pl
from jax.experimental.pallas import tpu as pltpu

F32 = jnp.float32
BF16 = jnp.bfloat16

D_MODEL = 2048
DEPTH = 4
EPS = 1e-6
D_FF = 5632
GRID_W = 64

S5_WIDTH = D_MODEL // 2
S5_GROUP = 16
S5_GROUPS = S5_WIDTH // S5_GROUP
S5_STATE = 64
S5_LAMBDA_RE_MAX = -1e-4

DIFF_WIDTH = D_MODEL - S5_WIDTH
DIFF_HEAD = 64
DIFF_HEADS = DIFF_WIDTH // (2 * DIFF_HEAD)
DIFF_VHEAD = 2 * DIFF_HEAD
DIFF_SUBLN_EPS = 1e-5

GQA_HEAD = 128
GQA_HEADS = D_MODEL // GQA_HEAD
GQA_KV_HEADS = 4
GQA_GROUP = GQA_HEADS // GQA_KV_HEADS
ROPE_AXIS = GQA_HEAD // 2
ROPE_THETA = 10000.0

X_HEADS = 4
X_HEAD = D_MODEL // X_HEADS

LOG2E = math.log2(math.e)

GQA_TQ = 512
GQA_TK = 4096
DIFF_TQ = 1024
DIFF_TK = 2048
ATTN_ROWS = 256
ATTN_KEYS = 256

LANES = 128
SOFTMAX_UNDERFLOW_LOG2 = 150.0
NORM_BOUND_MARGIN = 1.01
V7X_VMEM_BYTES = 64 * 1024 * 1024
VMEM_LIMIT_BYTES = V7X_VMEM_BYTES - 8 * 1024 * 1024

S5_CHUNK = 16
S5_LANE_GROUPS = LANES // S5_GROUP
S5_LANE_BLOCKS = S5_WIDTH // LANES
S5_BLOCK_COLS = S5_CHUNK * LANES
S5_BLOCK_STATE = S5_LANE_GROUPS * S5_STATE


def _tile(n, pref):
    t = min(pref, n)
    while n % t:
        t //= 2
    return t


def _cparams(*sem):
    return pltpu.CompilerParams(dimension_semantics=sem, vmem_limit_bytes=VMEM_LIMIT_BYTES)


def _rmsnorm(x, g, eps):
    ms = jnp.mean(x * x, axis=-1, keepdims=True)
    return x * lax.rsqrt(ms + eps) * g


def _sigmoid(x):
    return 1.0 / (1.0 + jnp.exp(-x))


def _gelu_tanh(x):
    c = math.sqrt(2.0 / math.pi)
    return x * (0.5 * (1.0 + jnp.tanh(c * (x + 0.044715 * (x * x * x)))))


def _norm_cast_kernel(x_ref, g_ref, h_ref):
    h_ref[...] = _rmsnorm(x_ref[...], g_ref[...], EPS).astype(h_ref.dtype)


def _norm_cast(x, g):
    L = x.shape[0]
    tm = _tile(L, 512)
    return pl.pallas_call(
        _norm_cast_kernel,
        grid=(L // tm,),
        in_specs=[pl.BlockSpec((tm, D_MODEL), lambda i: (i, 0)), pl.BlockSpec((1, D_MODEL), lambda i: (0, 0))],
        out_specs=pl.BlockSpec((tm, D_MODEL), lambda i: (i, 0)),
        out_shape=jax.ShapeDtypeStruct((L, D_MODEL), BF16),
        compiler_params=_cparams("parallel"),
        name="norm_cast",
    )(x, g.reshape(1, D_MODEL))


def _ffn_kernel(x_ref, h_ref, wg_ref, wu_ref, wd_ref, ng_ref, *rest, n_f, final):
    if final:
        (o_ref, acc_ref) = rest
    else:
        (o_ref, hn_ref, acc_ref) = rest
    j = pl.program_id(1)

    @pl.when(j == 0)
    def _():
        acc_ref[...] = jnp.zeros_like(acc_ref)

    h = h_ref[...]
    gate = jnp.dot(h, wg_ref[...], preferred_element_type=F32)
    up = jnp.dot(h, wu_ref[...], preferred_element_type=F32)
    act = gate * _sigmoid(gate) * up
    acc_ref[...] += jnp.dot(act.astype(BF16), wd_ref[...], preferred_element_type=F32)

    @pl.when(j == n_f - 1)
    def _():
        y = x_ref[...] + 0.5 * acc_ref[...]
        normed = _rmsnorm(y, ng_ref[...], EPS)
        if final:
            o_ref[...] = normed
        else:
            o_ref[...] = y
            hn_ref[...] = normed.astype(hn_ref.dtype)


def _ffn(x, h, w_gu, w_down, layer, next_g, final=False):
    L = x.shape[0]
    tm = _tile(L, 512)
    tf = 512
    n_f = D_FF // tf
    row = lambda: pl.BlockSpec((tm, D_MODEL), lambda i, j: (i, 0))
    in_specs = [
        row(),
        row(),
        pl.BlockSpec((None, D_MODEL, tf), lambda i, j: (layer, 0, j)),
        pl.BlockSpec((None, D_MODEL, tf), lambda i, j: (layer, 0, j + n_f)),
        pl.BlockSpec((None, tf, D_MODEL), lambda i, j: (layer, j, 0)),
        pl.BlockSpec((1, D_MODEL), lambda i, j: (0, 0)),
    ]
    x_out = jax.ShapeDtypeStruct((L, D_MODEL), F32)
    return pl.pallas_call(
        functools.partial(_ffn_kernel, n_f=n_f, final=final),
        grid=(L // tm, n_f),
        in_specs=in_specs,
        out_specs=row() if final else (row(), row()),
        out_shape=x_out if final else (x_out, jax.ShapeDtypeStruct((L, D_MODEL), BF16)),
        scratch_shapes=[pltpu.VMEM((tm, D_MODEL), F32)],
        compiler_params=_cparams("parallel", "arbitrary"),
        name="ffn_final" if final else "ffn",
    )(x, h, w_gu, w_gu, w_down, next_g.reshape(1, D_MODEL))


def _matmul_kernel(h_ref, w_ref, *rest, scaled):
    if scaled:
        c_ref, o_ref = rest
    else:
        (o_ref,) = rest
    z = jnp.dot(h_ref[...], w_ref[...], preferred_element_type=F32)
    if scaled:
        z = z * c_ref[...]
    o_ref[...] = z.astype(o_ref.dtype)


def _matmul(h, w, out_dtype, col_scale=None):
    L, K = h.shape
    N = w.shape[1]
    tm = _tile(L, 512)
    scaled = col_scale is not None
    in_specs = [pl.BlockSpec((tm, K), lambda i: (i, 0)), pl.BlockSpec((K, N), lambda i: (0, 0))]
    args = [h, w]
    if scaled:
        in_specs.append(pl.BlockSpec((1, N), lambda i: (0, 0)))
        args.append(col_scale.reshape(1, N))
    return pl.pallas_call(
        functools.partial(_matmul_kernel, scaled=scaled),
        grid=(L // tm,),
        in_specs=in_specs,
        out_specs=pl.BlockSpec((tm, N), lambda i: (i, 0)),
        out_shape=jax.ShapeDtypeStruct((L, N), out_dtype),
        compiler_params=_cparams("parallel"),
        name="matmul",
    )(*args)


def _norm_matmul_kernel(x_ref, g_ref, w_ref, *rest, scaled, tm, sub):
    if scaled:
        c_ref, o_ref = rest
    else:
        (o_ref,) = rest
    for c in range(tm // sub):
        rows = slice(c * sub, (c + 1) * sub)
        h = _rmsnorm(x_ref[rows, :], g_ref[...], EPS).astype(BF16)
        z = jnp.dot(h, w_ref[...], preferred_element_type=F32)
        if scaled:
            z = z * c_ref[...]
        o_ref[rows, :] = z.astype(o_ref.dtype)


def _norm_matmul(x, g, w, out_dtype, col_scale=None):
    L = x.shape[0]
    N = w.shape[1]
    tm = _tile(L, 512)
    scaled = col_scale is not None
    in_specs = [
        pl.BlockSpec((tm, D_MODEL), lambda i: (i, 0)),
        pl.BlockSpec((1, D_MODEL), lambda i: (0, 0)),
        pl.BlockSpec((D_MODEL, N), lambda i: (0, 0)),
    ]
    args = [x, g.reshape(1, D_MODEL), w]
    if scaled:
        in_specs.append(pl.BlockSpec((1, N), lambda i: (0, 0)))
        args.append(col_scale.reshape(1, N))
    return pl.pallas_call(
        functools.partial(_norm_matmul_kernel, scaled=scaled, tm=tm, sub=_tile(tm, 256)),
        grid=(L // tm,),
        in_specs=in_specs,
        out_specs=pl.BlockSpec((tm, N), lambda i: (i, 0)),
        out_shape=jax.ShapeDtypeStruct((L, N), out_dtype),
        compiler_params=_cparams("parallel"),
        name="norm_matmul",
    )(*args)


def _odd_proj_kernel(h_ref, w_ref, qg_ref, kg_ref, cos_ref, sin_ref, o_ref, *, heads_per_tile):
    j = pl.program_id(1)
    n_q_tiles = GQA_HEADS // heads_per_tile
    z = jnp.dot(h_ref[...], w_ref[...], preferred_element_type=F32)
    lane = lax.broadcasted_iota(jnp.int32, (1, GQA_HEAD), 1)
    first_half = (lane & (ROPE_AXIS - 1)) < (ROPE_AXIS // 2)

    def normed_rotated(hh, gain, cos, sin):
        sl = slice(hh * GQA_HEAD, (hh + 1) * GQA_HEAD)
        y = _rmsnorm(z[:, sl], gain, EPS)
        partner = jnp.where(first_half, pltpu.roll(y, GQA_HEAD - ROPE_AXIS // 2, 1),
                            pltpu.roll(y, ROPE_AXIS // 2, 1))
        o_ref[:, sl] = (y * cos + partner * sin).astype(o_ref.dtype)

    @pl.when(j < n_q_tiles)
    def _():
        out_scale = GQA_HEAD ** -0.5 * LOG2E
        cos = cos_ref[...] * out_scale
        sin = sin_ref[...] * out_scale
        for hh in range(heads_per_tile):
            normed_rotated(hh, qg_ref[...], cos, sin)

    @pl.when(j >= n_q_tiles)
    def _():
        for hh in range(GQA_KV_HEADS):
            normed_rotated(hh, kg_ref[...], cos_ref[...], sin_ref[...])
        v_cols = slice(GQA_KV_HEADS * GQA_HEAD, 2 * GQA_KV_HEADS * GQA_HEAD)
        o_ref[:, v_cols] = z[:, v_cols].astype(o_ref.dtype)


def _odd_proj(h, w, q_g, k_g, cos, sin):
    L = h.shape[0]
    N = w.shape[1]
    tm = _tile(L, 512)
    heads_per_tile = 2 * GQA_KV_HEADS
    tn = heads_per_tile * GQA_HEAD
    return pl.pallas_call(
        functools.partial(_odd_proj_kernel, heads_per_tile=heads_per_tile),
        grid=(L // tm, N // tn),
        in_specs=[
            pl.BlockSpec((tm, D_MODEL), lambda i, j: (i, 0)),
            pl.BlockSpec((D_MODEL, tn), lambda i, j: (0, j)),
            pl.BlockSpec((1, GQA_HEAD), lambda i, j: (0, 0)),
            pl.BlockSpec((1, GQA_HEAD), lambda i, j: (0, 0)),
            pl.BlockSpec((tm, GQA_HEAD), lambda i, j: (i, 0)),
            pl.BlockSpec((tm, GQA_HEAD), lambda i, j: (i, 0)),
        ],
        out_specs=pl.BlockSpec((tm, tn), lambda i, j: (i, j)),
        out_shape=jax.ShapeDtypeStruct((L, N), BF16),
        compiler_params=_cparams("parallel", "arbitrary"),
        name="odd_proj",
    )(h, w, q_g.reshape(1, GQA_HEAD), k_g.reshape(1, GQA_HEAD), cos, sin)


def _matmul_residual_kernel(a_ref, w_ref, r_ref, ng_ref, o_ref, hn_ref):
    y = r_ref[...] + jnp.dot(a_ref[...], w_ref[...], preferred_element_type=F32)
    o_ref[...] = y
    hn_ref[...] = _rmsnorm(y, ng_ref[...], EPS).astype(hn_ref.dtype)


def _matmul_residual(a, w, res, next_g):
    L, K = a.shape
    N = w.shape[1]
    tm = _tile(L, 512)
    return pl.pallas_call(
        _matmul_residual_kernel,
        grid=(L // tm,),
        in_specs=[
            pl.BlockSpec((tm, K), lambda i: (i, 0)),
            pl.BlockSpec((K, N), lambda i: (0, 0)),
            pl.BlockSpec((tm, N), lambda i: (i, 0)),
            pl.BlockSpec((1, N), lambda i: (0, 0)),
        ],
        out_specs=(pl.BlockSpec((tm, N), lambda i: (i, 0)), pl.BlockSpec((tm, N), lambda i: (i, 0))),
        out_shape=(jax.ShapeDtypeStruct((L, N), F32), jax.ShapeDtypeStruct((L, N), BF16)),
        compiler_params=_cparams("parallel"),
        name="matmul_residual",
    )(a, w, res, next_g.reshape(1, N))


def _lane_tile(x, n):
    return jnp.concatenate([x] * n, axis=1) if n > 1 else x


def _softmax_block(t, shift, v_ones, m, acc):
    row_max = jnp.max(t, axis=1, keepdims=True)
    m_new = jnp.maximum(m, row_max if shift is None else row_max + shift)
    alpha = jnp.exp2(m - m_new)
    p = jnp.exp2((t - _lane_tile(m_new if shift is None else m_new - shift, t.shape[1] // LANES)).astype(BF16))
    acc = _lane_tile(alpha, 2) * acc + jnp.dot(p, v_ones, preferred_element_type=F32)
    return m_new, acc


def _softmax_init(m_ref, acc_ref, vo_ref):
    m_ref[...] = jnp.full_like(m_ref, -jnp.inf)
    acc_ref[...] = jnp.zeros_like(acc_ref)
    vo_ref[:, LANES:] = jnp.ones((vo_ref.shape[0], LANES), vo_ref.dtype)


def _softmax_result(acc):
    return acc[:, :LANES] / acc[:, LANES:]


_NT = (((1,), (1,)), ((), ()))


def _gqa_kernel(q_ref, k_ref, v_ref, o_ref, m_ref, acc_ref, vo_ref, *, n_k, tq, tk, rows, keys):
    ki = pl.program_id(2)

    @pl.when(ki == 0)
    def _():
        _softmax_init(m_ref, acc_ref, vo_ref)

    vo_ref[:, :LANES] = v_ref[...]
    for g in range(GQA_GROUP):
        for c in range(tq // rows):
            sl = slice(g * tq + c * rows, g * tq + (c + 1) * rows)
            q = q_ref[c * rows:(c + 1) * rows, g * GQA_HEAD:(g + 1) * GQA_HEAD]
            state = (m_ref[sl, :], acc_ref[sl, :])
            for kk in range(tk // keys):
                ksl = slice(kk * keys, (kk + 1) * keys)
                s = lax.dot_general(q, k_ref[ksl, :], _NT, preferred_element_type=F32)
                state = _softmax_block(s, None, vo_ref[ksl, :], *state)
            m_ref[sl, :], acc_ref[sl, :] = state

    @pl.when(ki == n_k - 1)
    def _():
        for g in range(GQA_GROUP):
            sl = slice(g * tq, (g + 1) * tq)
            o_ref[:, g * GQA_HEAD:(g + 1) * GQA_HEAD] = _softmax_result(acc_ref[sl, :]).astype(o_ref.dtype)


def _gqa_attention(qkv):
    L = qkv.shape[0]
    tq = _tile(L, GQA_TQ)
    tk = _tile(L, GQA_TK)
    n_k = L // tk
    group_cols = GQA_GROUP * GQA_HEAD
    return pl.pallas_call(
        functools.partial(_gqa_kernel, n_k=n_k, tq=tq, tk=tk, rows=_tile(tq, ATTN_ROWS), keys=_tile(tk, ATTN_KEYS)),
        grid=(GQA_KV_HEADS, L // tq, n_k),
        in_specs=[
            pl.BlockSpec((tq, group_cols), lambda h, i, j: (i, h)),
            pl.BlockSpec((tk, GQA_HEAD), lambda h, i, j: (j, GQA_HEADS + h)),
            pl.BlockSpec((tk, GQA_HEAD), lambda h, i, j: (j, GQA_HEADS + GQA_KV_HEADS + h)),
        ],
        out_specs=pl.BlockSpec((tq, group_cols), lambda h, i, j: (i, h)),
        out_shape=jax.ShapeDtypeStruct((L, D_MODEL), BF16),
        scratch_shapes=[
            pltpu.VMEM((GQA_GROUP * tq, LANES), F32),
            pltpu.VMEM((GQA_GROUP * tq, 2 * LANES), F32),
            pltpu.VMEM((tk, 2 * LANES), BF16),
        ],
        compiler_params=_cparams("parallel", "parallel", "arbitrary"),
        name="gqa_attention",
    )(qkv, qkv, qkv)


def _sq_norm_max_kernel(x_ref, o_ref, *, half):
    x = x_ref[...].astype(F32)
    x2 = x * x
    first = lax.broadcasted_iota(jnp.int32, (1, LANES), 1) < half
    blocks = []
    for b in range(x2.shape[1] // LANES):
        xb = x2[:, b * LANES:(b + 1) * LANES]
        n1 = jnp.max(jnp.sum(jnp.where(first, xb, 0.0), axis=1, keepdims=True), axis=0, keepdims=True)
        n2 = jnp.max(jnp.sum(jnp.where(first, 0.0, xb), axis=1, keepdims=True), axis=0, keepdims=True)
        blocks.append(jnp.where(first, n1, n2))
    cur = jnp.concatenate(blocks, axis=1)

    @pl.when(pl.program_id(0) == 0)
    def _():
        o_ref[...] = cur

    @pl.when(pl.program_id(0) > 0)
    def _():
        o_ref[...] = jnp.maximum(o_ref[...], cur)


def _sq_norm_max(x, n_cols, half):
    L = x.shape[0]
    tm = _tile(L, 512)
    return pl.pallas_call(
        functools.partial(_sq_norm_max_kernel, half=half),
        grid=(L // tm,),
        in_specs=[pl.BlockSpec((tm, n_cols), lambda i: (i, 0))],
        out_specs=pl.BlockSpec((1, n_cols), lambda i: (0, 0)),
        out_shape=jax.ShapeDtypeStruct((1, n_cols), F32),
        compiler_params=_cparams("arbitrary"),
        name="sq_norm_max",
    )(x)


def _diff_kernel(klo_ref, khi_ref, slope_ref, q_ref, k_ref, v_ref, lq1_ref, lk1_ref, lq2_ref, lk2_ref, sg_ref,
                 o_ref, qs_ref, m_ref, acc_ref, vo_ref, *, n_k, tq, tk, rows, keys, lambda_init):
    h = pl.program_id(0)
    qi = pl.program_id(1)
    j = pl.program_id(2)
    q0 = qi * tq
    kt = klo_ref[h, qi] + j
    active = kt <= khi_ref[h, qi]
    k0 = kt * tk
    n_c = tq // rows

    @pl.when(j == 0)
    def _():
        _softmax_init(m_ref, acc_ref, vo_ref)
        q = q_ref[...]
        lane = lax.broadcasted_iota(jnp.int32, (1, 2 * DIFF_HEAD), 1)
        zero = jnp.zeros_like(q)
        q1 = jnp.where(lane < DIFF_HEAD, q, zero)
        q2 = jnp.where(lane >= DIFF_HEAD, q, zero)
        for c in range(n_c):
            qs_ref[2 * c * rows:(2 * c + 1) * rows, :] = q1[c * rows:(c + 1) * rows]
            qs_ref[(2 * c + 1) * rows:(2 * c + 2) * rows, :] = q2[c * rows:(c + 1) * rows]

    slope2 = slope_ref[h] * LOG2E
    d0 = lax.broadcasted_iota(jnp.int32, (rows, keys), 0) - lax.broadcasted_iota(jnp.int32, (rows, keys), 1)

    def step(bias_fn):
        vo_ref[:, :LANES] = v_ref[...]
        for c in range(n_c):
            sl = slice(2 * c * rows, 2 * (c + 1) * rows)
            state = (m_ref[sl, :], acc_ref[sl, :])
            for kk in range(tk // keys):
                ksl = slice(kk * keys, (kk + 1) * keys)
                bias, shift = bias_fn(q0 + c * rows - (k0 + kk * keys))
                t = lax.dot_general(qs_ref[sl, :], k_ref[ksl, :], _NT, preferred_element_type=F32)
                t = t + jnp.concatenate([bias, bias], axis=0)
                state = _softmax_block(t, shift, vo_ref[ksl, :], *state)
            m_ref[sl, :], acc_ref[sl, :] = state

    crosses_diagonal = jnp.logical_and(k0 < q0 + tq, q0 < k0 + tk)

    @pl.when(jnp.logical_and(active, crosses_diagonal))
    def _():
        step(lambda off: ((-slope2) * jnp.abs(off + d0).astype(F32), None))

    @pl.when(jnp.logical_and(active, jnp.logical_not(crosses_diagonal)))
    def _():
        coef = jnp.where(q0 >= k0, -slope2, slope2)
        tile = coef * d0.astype(F32)
        step(lambda off: (tile, coef * off.astype(F32)))

    @pl.when(j == n_k - 1)
    def _():
        lam = (jnp.exp(jnp.sum(lq1_ref[...] * lk1_ref[...], axis=-1, keepdims=True))
               - jnp.exp(jnp.sum(lq2_ref[...] * lk2_ref[...], axis=-1, keepdims=True)) + lambda_init)
        for c in range(n_c):
            o1 = _softmax_result(acc_ref[2 * c * rows:(2 * c + 1) * rows, :])
            o2 = _softmax_result(acc_ref[(2 * c + 1) * rows:(2 * c + 2) * rows, :])
            o_ref[c * rows:(c + 1) * rows, :] = (
                _rmsnorm(o1 - lam * o2, sg_ref[...], DIFF_SUBLN_EPS) * (1.0 - lambda_init)).astype(o_ref.dtype)


def _diff_key_ranges(qkv, slopes, tq, tk):
    L = qkv.shape[0]
    n_q, n_k = L // tq, L // tk
    sq = _sq_norm_max(qkv, 2 * DIFF_WIDTH, DIFF_HEAD).reshape(2, DIFF_HEADS, 2, DIFF_HEAD)[..., 0]
    norms = jnp.sqrt(sq)
    bound = jnp.max(norms[0] * norms[1], axis=1) * NORM_BOUND_MARGIN
    radius = (SOFTMAX_UNDERFLOW_LOG2 + 2.0 * bound) / (slopes * LOG2E)
    radius = jnp.where(jnp.isfinite(radius), radius, float(L))[:, None]
    q_first = (jnp.arange(n_q, dtype=F32) * tq)[None, :]
    klo = jnp.clip(jnp.floor((q_first - radius) / tk), 0, n_k - 1).astype(jnp.int32)
    khi = jnp.clip(jnp.floor((q_first + (tq - 1) + radius) / tk), 0, n_k - 1).astype(jnp.int32)
    return klo, khi


def _diff_attention(qkv, lq1, lk1, lq2, lk2, subln_g, layer_idx):
    L = qkv.shape[0]
    tq = _tile(L, DIFF_TQ)
    tk = _tile(L, DIFF_TK)
    n_k = L // tk
    lambda_init = 0.8 - 0.6 * math.exp(-0.3 * layer_idx)
    slopes = jnp.asarray([2.0 ** (-8.0 * (i + 1) / DIFF_HEADS) for i in range(DIFF_HEADS)], dtype=F32)
    klo, khi = _diff_key_ranges(qkv, slopes, tq, tk)
    key_tile = lambda h, i, j, klo, khi: jnp.minimum(klo[h, i] + j, khi[h, i])
    vec = lambda: pl.BlockSpec((1, DIFF_HEAD), lambda h, i, j, klo, khi: (0, 0))
    return pl.pallas_call(
        functools.partial(_diff_kernel, n_k=n_k, tq=tq, tk=tk, rows=_tile(tq, ATTN_ROWS), keys=_tile(tk, ATTN_KEYS),
                          lambda_init=lambda_init),
        grid_spec=pltpu.PrefetchScalarGridSpec(
            num_scalar_prefetch=2,
            grid=(DIFF_HEADS, L // tq, n_k),
            in_specs=[
                pl.BlockSpec(memory_space=pltpu.SMEM),
                pl.BlockSpec((tq, DIFF_VHEAD), lambda h, i, j, klo, khi: (i, h)),
                pl.BlockSpec((tk, DIFF_VHEAD), lambda h, i, j, klo, khi: (key_tile(h, i, j, klo, khi), DIFF_HEADS + h)),
                pl.BlockSpec((tk, DIFF_VHEAD),
                             lambda h, i, j, klo, khi: (key_tile(h, i, j, klo, khi), 2 * DIFF_HEADS + h)),
                vec(), vec(), vec(), vec(),
                pl.BlockSpec((1, DIFF_VHEAD), lambda h, i, j, klo, khi: (0, 0)),
            ],
            out_specs=pl.BlockSpec((tq, DIFF_VHEAD), lambda h, i, j, klo, khi: (i, h)),
            scratch_shapes=[
                pltpu.VMEM((2 * tq, DIFF_VHEAD), BF16),
                pltpu.VMEM((2 * tq, LANES), F32),
                pltpu.VMEM((2 * tq, 2 * LANES), F32),
                pltpu.VMEM((tk, 2 * LANES), BF16),
            ],
        ),
        out_shape=jax.ShapeDtypeStruct((L, DIFF_WIDTH), BF16),
        compiler_params=_cparams("parallel", "parallel", "arbitrary"),
        name="diff_attention",
    )(klo, khi, slopes, qkv, qkv, qkv, lq1.reshape(1, -1), lk1.reshape(1, -1), lq2.reshape(1, -1),
      lk2.reshape(1, -1), subln_g.reshape(1, -1))


def _cross_kernel(q_ref, kv_ref, o_ref, *, scale):
    for h in range(X_HEADS):
        sl = slice(h * X_HEAD, (h + 1) * X_HEAD)
        vsl = slice(D_MODEL + h * X_HEAD, D_MODEL + (h + 1) * X_HEAD)
        s = lax.dot_general(q_ref[:, sl], kv_ref[:, sl], (((1,), (1,)), ((), ())),
                            preferred_element_type=F32) * scale
        e = jnp.exp(s - jnp.max(s, axis=-1, keepdims=True))
        p = e / jnp.sum(e, axis=-1, keepdims=True)
        o_ref[:, sl] = jnp.dot(p.astype(BF16), kv_ref[:, vsl], preferred_element_type=F32).astype(o_ref.dtype)


def _cross_attention(q, kv):
    L = q.shape[0]
    n_mem = kv.shape[0]
    tq = _tile(L, 512)
    return pl.pallas_call(
        functools.partial(_cross_kernel, scale=X_HEAD ** -0.5),
        grid=(L // tq,),
        in_specs=[
            pl.BlockSpec((tq, D_MODEL), lambda i: (i, 0)),
            pl.BlockSpec((n_mem, 2 * D_MODEL), lambda i: (0, 0)),
        ],
        out_specs=pl.BlockSpec((tq, D_MODEL), lambda i: (i, 0)),
        out_shape=jax.ShapeDtypeStruct((L, D_MODEL), BF16),
        compiler_params=_cparams("parallel"),
        name="cross_attention",
    )(q, kv)


def _s5_table_kernel(ar_ref, ai_ref, br_ref, bi_ref, o_ref):
    hi = lax.Precision.HIGHEST
    o_ref[0, 0] = (jnp.dot(ar_ref[0, 0], br_ref[0, 0], preferred_element_type=F32, precision=hi)
                   - jnp.dot(ai_ref[0, 0], bi_ref[0, 0], preferred_element_type=F32, precision=hi))


def _s5_chunk_kernel_table(ca_re, ca_im, bb_re, bb_im):
    T = ca_re.shape[0]
    rows = T * S5_GROUP
    stack = lambda a: a.transpose(1, 2, 0, 3, 4).reshape(2, S5_GROUPS, rows, S5_STATE)
    lhs = pl.BlockSpec((1, 1, rows, S5_STATE), lambda d, g: (d, g, 0, 0))
    rhs = pl.BlockSpec((1, 1, S5_STATE, S5_GROUP), lambda d, g: (d, g, 0, 0))
    out = pl.pallas_call(
        _s5_table_kernel,
        grid=(2, S5_GROUPS),
        in_specs=[lhs, lhs, rhs, rhs],
        out_specs=pl.BlockSpec((1, 1, rows, S5_GROUP), lambda d, g: (d, g, 0, 0)),
        out_shape=jax.ShapeDtypeStruct((2, S5_GROUPS, rows, S5_GROUP), F32),
        compiler_params=_cparams("parallel", "parallel"),
        name="s5_kernel_table",
    )(stack(ca_re), stack(ca_im), bb_re, bb_im)
    return out.reshape(2, S5_GROUPS, T, S5_GROUP, S5_GROUP).transpose(2, 0, 1, 3, 4)


def _s5_weights(lam_re, lam_im, log_dt, b_re, b_im, c_re, c_im):
    T = S5_CHUNK
    lr = jnp.minimum(lam_re, S5_LAMBDA_RE_MAX)
    li = lam_im
    dt = jnp.exp(log_dt)[..., None]
    mag = jnp.exp(lr * dt)
    ab_re = mag * jnp.cos(li * dt)
    ab_im = mag * jnp.sin(li * dt)
    nr = ab_re - 1.0
    den = lr * lr + li * li
    f_re = ((nr * lr + ab_im * li) / den)[..., None]
    f_im = ((ab_im * lr - nr * li) / den)[..., None]
    bb_re = f_re * b_re - f_im * b_im
    bb_im = f_re * b_im + f_im * b_re
    k = jnp.arange(T + 1, dtype=F32)[:, None, None, None]
    pmag = jnp.exp(lr * dt * k)
    pw_re = pmag * jnp.cos(li * dt * k)
    pw_im = pmag * jnp.sin(li * dt * k)
    ca_re = c_re * pw_re[:, :, :, None, :] - c_im * pw_im[:, :, :, None, :]
    ca_im = c_re * pw_im[:, :, :, None, :] + c_im * pw_re[:, :, :, None, :]
    nb, ng = S5_LANE_BLOCKS, S5_LANE_GROUPS

    def group_diagonal(x, rows_per_group):
        w = x.shape[-1]
        reps = jnp.tile(x.astype(BF16), (1, 1, 1, ng))
        row_group = (jnp.arange(x.shape[1]) // rows_per_group) % ng
        col_group = jnp.arange(ng * w) // w
        keep = (row_group[:, None] == col_group[None, :])[None, :, None, :]
        return jnp.where(keep, reps, jnp.zeros((), BF16))

    kern = _s5_chunk_kernel_table(ca_re[:T], ca_im[:T], bb_re, bb_im)
    ktab = jnp.concatenate([kern[1:, 1][::-1], (kern[0, 0] + kern[0, 1])[None], kern[1:, 0]], axis=0)
    kblk = ktab.reshape(2 * T - 1, nb, ng, S5_GROUP, S5_GROUP).transpose(0, 1, 2, 4, 3)
    kblk = group_diagonal(kblk.reshape((2 * T - 1) * nb, LANES, 1, S5_GROUP), S5_GROUP)
    kblk = kblk.reshape(2 * T - 1, nb, LANES, LANES)
    idx = jnp.arange(T)[None, :] - jnp.arange(T)[:, None] + (T - 1)
    toep = kblk[idx].transpose(2, 0, 3, 1, 4).reshape(nb, S5_BLOCK_COLS, S5_BLOCK_COLS)

    e_re = jnp.stack([pw_re[:T, 0][::-1], pw_re[:T, 1]], axis=1)[..., None]
    e_im = jnp.stack([pw_im[:T, 0][::-1], pw_im[:T, 1]], axis=1)[..., None]
    zb = jnp.stack([e_re * bb_re - e_im * bb_im, e_re * bb_im + e_im * bb_re], axis=2)
    z_rows = zb.reshape(T, 2, 2, nb, ng, S5_STATE, S5_GROUP).transpose(3, 0, 4, 6, 1, 2, 5)
    w_in = group_diagonal(z_rows.reshape(nb, S5_BLOCK_COLS, 4, S5_STATE), S5_GROUP)
    w_in = w_in.reshape(nb, S5_BLOCK_COLS, 4 * S5_BLOCK_STATE)

    cr = jnp.stack([ca_re[1:, 0], ca_re[1:, 1][::-1]], axis=1)
    ci = jnp.stack([ca_im[1:, 0], ca_im[1:, 1][::-1]], axis=1)
    c_rows = jnp.stack([cr, -ci], axis=2).reshape(T, 2, 2, nb, ng, S5_GROUP, S5_STATE).transpose(3, 1, 2, 4, 6, 0, 5)
    w_out = group_diagonal(c_rows.reshape(nb, 4 * S5_BLOCK_STATE, T, S5_GROUP), S5_STATE)
    w_out = w_out.reshape(nb, 4 * S5_BLOCK_STATE, S5_BLOCK_COLS)

    a4 = jnp.stack([pw_re[T], pw_im[T]], axis=1).reshape(2, 2, nb, ng, S5_STATE)
    a_t = a4.transpose(2, 0, 1, 3, 4).reshape(nb, 1, 4 * S5_BLOCK_STATE)
    return toep, w_in, w_out, a_t


def _s5_gather_chunks(u_ref, lhs_ref, tr):
    for t in range(S5_CHUNK):
        lhs_ref[:, t * LANES:(t + 1) * LANES] = u_ref[pl.ds(t, tr, stride=S5_CHUNK), :].astype(BF16)


def _s5_in_kernel(u_ref, w_ref, z_ref, lhs_ref, *, tr):
    _s5_gather_chunks(u_ref, lhs_ref, tr)
    z_ref[0] = jnp.dot(lhs_ref[...], w_ref[0], preferred_element_type=F32)


def _s5_out_kernel(u_ref, toep_ref, x_ref, w_ref, d_ref, y_ref, lhs_ref, *, tr):
    _s5_gather_chunks(u_ref, lhs_ref, tr)
    y = (jnp.dot(lhs_ref[...], toep_ref[0], preferred_element_type=F32)
         + jnp.dot(x_ref[0].astype(BF16), w_ref[0], preferred_element_type=F32))
    d = d_ref[...]
    for t in range(S5_CHUNK):
        rows = pl.ds(t, tr, stride=S5_CHUNK)
        y_ref[rows, :] = d * u_ref[rows, :] + y[:, t * LANES:(t + 1) * LANES]


def _s5_scan_kernel(z_ref, a_ref, x_ref, *, n_rows):
    n_s = S5_BLOCK_STATE
    a = a_ref[0]
    af_re, af_im, ab_re, ab_im = (a[:, i * n_s:(i + 1) * n_s] for i in range(4))

    def body(n, carry):
        f_re, f_im, b_re, b_im = carry
        rf = pl.ds(n, 1)
        rb = pl.ds(n_rows - 1 - n, 1)
        x_ref[0, rf, 0 * n_s:1 * n_s] = f_re
        x_ref[0, rf, 1 * n_s:2 * n_s] = f_im
        x_ref[0, rb, 2 * n_s:3 * n_s] = b_re
        x_ref[0, rb, 3 * n_s:4 * n_s] = b_im
        zf_re = z_ref[0, rf, 0 * n_s:1 * n_s]
        zf_im = z_ref[0, rf, 1 * n_s:2 * n_s]
        zb_re = z_ref[0, rb, 2 * n_s:3 * n_s]
        zb_im = z_ref[0, rb, 3 * n_s:4 * n_s]
        return (af_re * f_re - af_im * f_im + zf_re, af_re * f_im + af_im * f_re + zf_im,
                ab_re * b_re - ab_im * b_im + zb_re, ab_re * b_im + ab_im * b_re + zb_im)

    zero = jnp.zeros((1, n_s), F32)
    lax.fori_loop(0, n_rows, body, (zero, zero, zero, zero))


def _s5_glu_kernel(y_ref, w_ref, b_ref, o_ref):
    g = _gelu_tanh(y_ref[...])
    gate = jnp.dot(g.astype(BF16), w_ref[...], preferred_element_type=F32) + b_ref[...]
    o_ref[...] = (g * _sigmoid(gate)).astype(o_ref.dtype)


def _s5_mixer(u, s5w, d_skip, glu_w, glu_b):
    L = u.shape[0]
    toep, w_in, w_out, a_t = s5w
    nb = S5_LANE_BLOCKS
    n_rows = L // S5_CHUNK
    n_st = 4 * S5_BLOCK_STATE
    tr = _tile(n_rows, 256)
    tokens = pl.BlockSpec((tr * S5_CHUNK, LANES), lambda c, i: (i, c))
    blk = lambda cols: pl.BlockSpec((1, tr, cols), lambda c, i: (c, i, 0))
    wgt = lambda rows, cols: pl.BlockSpec((1, rows, cols), lambda c, i: (c, 0, 0))
    chunk_lhs = pltpu.VMEM((tr, S5_BLOCK_COLS), BF16)
    z = pl.pallas_call(
        functools.partial(_s5_in_kernel, tr=tr),
        grid=(nb, n_rows // tr),
        in_specs=[tokens, wgt(S5_BLOCK_COLS, n_st)],
        out_specs=blk(n_st),
        out_shape=jax.ShapeDtypeStruct((nb, n_rows, n_st), F32),
        scratch_shapes=[chunk_lhs],
        compiler_params=_cparams("parallel", "parallel"),
        name="s5_chunk_input",
    )(u, w_in)
    x = pl.pallas_call(
        functools.partial(_s5_scan_kernel, n_rows=n_rows),
        grid=(nb,),
        in_specs=[pl.BlockSpec((1, n_rows, n_st), lambda c: (c, 0, 0)),
                  pl.BlockSpec((1, 1, n_st), lambda c: (c, 0, 0))],
        out_specs=pl.BlockSpec((1, n_rows, n_st), lambda c: (c, 0, 0)),
        out_shape=jax.ShapeDtypeStruct((nb, n_rows, n_st), F32),
        compiler_params=_cparams("parallel"),
        name="s5_chunk_scan",
    )(z, a_t)
    y = pl.pallas_call(
        functools.partial(_s5_out_kernel, tr=tr),
        grid=(nb, n_rows // tr),
        in_specs=[tokens, wgt(S5_BLOCK_COLS, S5_BLOCK_COLS), blk(n_st), wgt(n_st, S5_BLOCK_COLS),
                  pl.BlockSpec((1, LANES), lambda c, i: (0, c))],
        out_specs=tokens,
        out_shape=jax.ShapeDtypeStruct((L, S5_WIDTH), F32),
        scratch_shapes=[chunk_lhs],
        compiler_params=_cparams("parallel", "parallel"),
        name="s5_chunk_output",
    )(u, toep, x, w_out, d_skip.reshape(1, -1))
    tm = _tile(L, 512)
    row = lambda: pl.BlockSpec((tm, S5_WIDTH), lambda i: (i, 0))
    vec = lambda: pl.BlockSpec((1, S5_WIDTH), lambda i: (0, 0))
    return pl.pallas_call(
        _s5_glu_kernel,
        grid=(L // tm,),
        in_specs=[row(), pl.BlockSpec((S5_WIDTH, S5_WIDTH), lambda i: (0, 0)), vec()],
        out_specs=row(),
        out_shape=jax.ShapeDtypeStruct((L, S5_WIDTH), BF16),
        compiler_params=_cparams("parallel"),
        name="s5_glu",
    )(y, glu_w, glu_b.reshape(1, -1))


def _rope_tables(L):
    rows = L // GRID_W
    r = jnp.repeat(jnp.arange(rows, dtype=F32), GRID_W)
    c = jnp.tile(jnp.arange(GRID_W, dtype=F32), rows)
    inv = ROPE_THETA ** (-jnp.arange(0, ROPE_AXIS, 2, dtype=F32) / ROPE_AXIS)
    ar = r[:, None] * inv
    ac = c[:, None] * inv
    cos = jnp.concatenate([jnp.cos(ar), jnp.cos(ar), jnp.cos(ac), jnp.cos(ac)], axis=-1)
    sin = jnp.concatenate([-jnp.sin(ar), jnp.sin(ar), -jnp.sin(ac), jnp.sin(ac)], axis=-1)
    return cos, sin


def _encoder(x, mem, p):
    L = x.shape[0]
    cos, sin = _rope_tables(L)
    h = _norm_cast(x, p['ffn1_norm'][0])
    for l in range(DEPTH):
        x, h = _ffn(x, h, p['ffn1_w_gu'], p['ffn1_w_down'], l, p['mix_norm'][l])
        if l % 2 == 0:
            e = l // 2
            u = _matmul(h, p['even_w_in_u'][e], F32)
            qkv = _matmul(h, p['even_w_in_qkv'][e], BF16, col_scale=p['diff_q_scale'])
            ya = _s5_mixer(u, p['s5'][e], p['s5_d'][e], p['s5_glu_w'][e], p['s5_glu_b'][e])
            yb = _diff_attention(qkv, p['diff_lambda_q1'][e], p['diff_lambda_k1'][e], p['diff_lambda_q2'][e],
                                 p['diff_lambda_k2'][e], p['diff_subln'][e], l)
            mixed, w_out = jnp.concatenate([ya, yb], axis=-1), p['even_w_out'][e]
        else:
            o = l // 2
            qkv = _odd_proj(h, p['odd_w_in'][o], p['gqa_q_norm'][o], p['gqa_k_norm'][o], cos, sin)
            mixed, w_out = _gqa_attention(qkv), p['odd_w_out'][o]
        x, h = _matmul_residual(mixed, w_out, x, p['cross_norm'][l])
        q = _matmul(h, p['cross_w_q'][l], BF16)
        kv = _norm_matmul(mem, p['mem_norm'][l], p['cross_w_kv'][l], BF16)
        x, h = _matmul_residual(_cross_attention(q, kv), p['cross_w_o'][l], x, p['ffn2_norm'][l])
        if l == DEPTH - 1:
            return _ffn(x, h, p['ffn2_w_gu'], p['ffn2_w_down'], l, p['final_norm'], final=True)
        x, h = _ffn(x, h, p['ffn2_w_gu'], p['ffn2_w_down'], l, p['ffn1_norm'][l + 1])


def kernel(x_prompt, x_sample, mem_prompt, mem_sample, ffn1_norm, ffn1_w_gu, ffn1_w_down, mix_norm, even_w_in, even_w_out, s5_lambda_re, s5_lambda_im, s5_log_dt, s5_b_re, s5_b_im, s5_c_re, s5_c_im, s5_d, s5_glu_w, s5_glu_b, diff_lambda_q1, diff_lambda_k1, diff_lambda_q2, diff_lambda_k2, diff_subln, odd_w_in, odd_w_out, gqa_q_norm, gqa_k_norm, cross_norm, mem_norm, cross_w_q, cross_w_kv, cross_w_o, ffn2_norm, ffn2_w_gu, ffn2_w_down, final_norm):
    bf = lambda w: w.astype(BF16)
    diff_q_scale = jnp.concatenate([jnp.full((DIFF_WIDTH,), DIFF_HEAD ** -0.5 * LOG2E, F32),
                                    jnp.ones((2 * DIFF_WIDTH,), F32)])
    p = dict(
        ffn1_norm=ffn1_norm, ffn1_w_gu=bf(ffn1_w_gu), ffn1_w_down=bf(ffn1_w_down), mix_norm=mix_norm,
        even_w_in_u=bf(even_w_in[:, :, :S5_WIDTH]), even_w_in_qkv=bf(even_w_in[:, :, S5_WIDTH:]),
        diff_q_scale=diff_q_scale,
        even_w_out=bf(even_w_out),
        s5=[_s5_weights(s5_lambda_re[e], s5_lambda_im[e], s5_log_dt[e], s5_b_re[e], s5_b_im[e], s5_c_re[e], s5_c_im[e])
            for e in range(s5_lambda_re.shape[0])],
        s5_d=s5_d, s5_glu_w=bf(s5_glu_w), s5_glu_b=s5_glu_b,
        diff_lambda_q1=diff_lambda_q1, diff_lambda_k1=diff_lambda_k1, diff_lambda_q2=diff_lambda_q2,
        diff_lambda_k2=diff_lambda_k2, diff_subln=diff_subln,
        odd_w_in=bf(odd_w_in), odd_w_out=bf(odd_w_out), gqa_q_norm=gqa_q_norm, gqa_k_norm=gqa_k_norm,
        cross_norm=cross_norm, mem_norm=mem_norm, cross_w_q=bf(cross_w_q), cross_w_kv=bf(cross_w_kv),
        cross_w_o=bf(cross_w_o), ffn2_norm=ffn2_norm, ffn2_w_gu=bf(ffn2_w_gu), ffn2_w_down=bf(ffn2_w_down),
        final_norm=final_norm)
    outs = []
    for x, mem in ((x_prompt, mem_prompt), (x_sample, mem_sample)):
        outs.append(jnp.stack([_encoder(x[b], mem[b], p) for b in range(x.shape[0])]))
    return tuple(outs)
```

```python
import functools
import math

import jax
import jax.numpy as jnp
from jax import lax
from jax.experimental import pallas as pl
from jax.experimental.pallas import tpu as pltpu

F32 = jnp.float32
BF16 = jnp.bfloat16

D_MODEL = 2048
DEPTH = 4
EPS = 1e-6
D_FF = 5632
GRID_W = 64

S5_WIDTH = D_MODEL // 2
S5_GROUP = 16
S5_GROUPS = S5_WIDTH // S5_GROUP
S5_STATE = 64
S5_LAMBDA_RE_MAX = -1e-4

DIFF_WIDTH = D_MODEL - S5_WIDTH
DIFF_HEAD = 64
DIFF_HEADS = DIFF_WIDTH // (2 * DIFF_HEAD)
DIFF_VHEAD = 2 * DIFF_HEAD
DIFF_SUBLN_EPS = 1e-5

GQA_HEAD = 128
GQA_HEADS = D_MODEL // GQA_HEAD
GQA_KV_HEADS = 4
GQA_GROUP = GQA_HEADS // GQA_KV_HEADS
ROPE_AXIS = GQA_HEAD // 2
ROPE_THETA = 10000.0

X_HEADS = 4
X_HEAD = D_MODEL // X_HEADS

LOG2E = math.log2(math.e)

GQA_TQ = 512
GQA_TK = 4096
DIFF_TQ = 1024
DIFF_TK = 2048
ATTN_ROWS = 256
ATTN_KEYS = 256

LANES = 128
SOFTMAX_UNDERFLOW_LOG2 = 150.0
NORM_BOUND_MARGIN = 1.01
V7X_VMEM_BYTES = 64 * 1024 * 1024
VMEM_LIMIT_BYTES = V7X_VMEM_BYTES - 8 * 1024 * 1024

S5_CHUNK = 16
S5_LANE_GROUPS = LANES // S5_GROUP
S5_LANE_BLOCKS = S5_WIDTH // LANES
S5_BLOCK_COLS = S5_CHUNK * LANES
S5_BLOCK_STATE = S5_LANE_GROUPS * S5_STATE


def _tile(n, pref):
    t = min(pref, n)
    while n % t:
        t //= 2
    return t


def _cparams(*sem):
    return pltpu.CompilerParams(dimension_semantics=sem, vmem_limit_bytes=VMEM_LIMIT_BYTES)


def _rmsnorm(x, g, eps):
    ms = jnp.mean(x * x, axis=-1, keepdims=True)
    return x * lax.rsqrt(ms + eps) * g


def _sigmoid(x):
    return 1.0 / (1.0 + jnp.exp(-x))


def _gelu_tanh(x):
    c = math.sqrt(2.0 / math.pi)
    return x * (0.5 * (1.0 + jnp.tanh(c * (x + 0.044715 * (x * x * x)))))


def _ffn_kernel(x_ref, g_ref, wg_ref, wu_ref, wd_ref, *rest, n_f, tail):
    if tail is None:
        o_ref, h_ref, acc_ref = rest
    elif tail == 'emit':
        ng_ref, o_ref, hn_ref, h_ref, acc_ref = rest
    else:
        ng_ref, o_ref, h_ref, acc_ref = rest
    j = pl.program_id(1)

    @pl.when(j == 0)
    def _():
        h_ref[...] = _rmsnorm(x_ref[...], g_ref[...], EPS).astype(BF16)
        acc_ref[...] = jnp.zeros_like(acc_ref)

    h = h_ref[...]
    gate = jnp.dot(h, wg_ref[...], preferred_element_type=F32)
    up = jnp.dot(h, wu_ref[...], preferred_element_type=F32)
    act = gate * _sigmoid(gate) * up
    acc_ref[...] += jnp.dot(act.astype(BF16), wd_ref[...], preferred_element_type=F32)

    @pl.when(j == n_f - 1)
    def _():
        y = x_ref[...] + 0.5 * acc_ref[...]
        if tail == 'final':
            o_ref[...] = _rmsnorm(y, ng_ref[...], EPS)
        else:
            o_ref[...] = y
        if tail == 'emit':
            hn_ref[...] = _rmsnorm(y, ng_ref[...], EPS).astype(hn_ref.dtype)


def _ffn(x, g, w_gu, w_down, layer, next_g=None, final_g=None):
    L = x.shape[0]
    tm = _tile(L, 512)
    tf = 512
    n_f = D_FF // tf
    tail = 'final' if final_g is not None else ('emit' if next_g is not None else None)
    row = lambda: pl.BlockSpec((tm, D_MODEL), lambda i, j: (i, 0))
    vec = lambda: pl.BlockSpec((1, D_MODEL), lambda i, j: (0, 0))
    in_specs = [
        row(),
        vec(),
        pl.BlockSpec((None, D_MODEL, tf), lambda i, j: (layer, 0, j)),
        pl.BlockSpec((None, D_MODEL, tf), lambda i, j: (layer, 0, j + n_f)),
        pl.BlockSpec((None, tf, D_MODEL), lambda i, j: (layer, j, 0)),
    ]
    args = [x, g.reshape(1, D_MODEL), w_gu, w_gu, w_down]
    if tail is not None:
        in_specs.append(vec())
        args.append((final_g if tail == 'final' else next_g).reshape(1, D_MODEL))
    x_out = jax.ShapeDtypeStruct((L, D_MODEL), F32)
    emit = tail == 'emit'
    return pl.pallas_call(
        functools.partial(_ffn_kernel, n_f=n_f, tail=tail),
        grid=(L // tm, n_f),
        in_specs=in_specs,
        out_specs=(row(), row()) if emit else row(),
        out_shape=(x_out, jax.ShapeDtypeStruct((L, D_MODEL), BF16)) if emit else x_out,
        scratch_shapes=[pltpu.VMEM((tm, D_MODEL), BF16), pltpu.VMEM((tm, D_MODEL), F32)],
        compiler_params=_cparams("parallel", "arbitrary"),
        name="ffn_final" if tail == 'final' else "ffn",
    )(*args)


def _matmul_kernel(h_ref, w_ref, *rest, scaled):
    if scaled:
        c_ref, o_ref = rest
    else:
        (o_ref,) = rest
    z = jnp.dot(h_ref[...], w_ref[...], preferred_element_type=F32)
    if scaled:
        z = z * c_ref[...]
    o_ref[...] = z.astype(o_ref.dtype)


def _matmul(h, w, out_dtype, col_scale=None):
    L, K = h.shape
    N = w.shape[1]
    tm = _tile(L, 512)
    scaled = col_scale is not None
    in_specs = [pl.BlockSpec((tm, K), lambda i: (i, 0)), pl.BlockSpec((K, N), lambda i: (0, 0))]
    args = [h, w]
    if scaled:
        in_specs.append(pl.BlockSpec((1, N), lambda i: (0, 0)))
        args.append(col_scale.reshape(1, N))
    return pl.pallas_call(
        functools.partial(_matmul_kernel, scaled=scaled),
        grid=(L // tm,),
        in_specs=in_specs,
        out_specs=pl.BlockSpec((tm, N), lambda i: (i, 0)),
        out_shape=jax.ShapeDtypeStruct((L, N), out_dtype),
        compiler_params=_cparams("parallel"),
        name="matmul",
    )(*args)


def _norm_matmul_kernel(x_ref, g_ref, w_ref, *rest, scaled, tm, sub):
    if scaled:
        c_ref, o_ref = rest
    else:
        (o_ref,) = rest
    for c in range(tm // sub):
        rows = slice(c * sub, (c + 1) * sub)
        h = _rmsnorm(x_ref[rows, :], g_ref[...], EPS).astype(BF16)
        z = jnp.dot(h, w_ref[...], preferred_element_type=F32)
        if scaled:
            z = z * c_ref[...]
        o_ref[rows, :] = z.astype(o_ref.dtype)


def _norm_matmul(x, g, w, out_dtype, col_scale=None):
    L = x.shape[0]
    N = w.shape[1]
    tm = _tile(L, 512)
    scaled = col_scale is not None
    in_specs = [
        pl.BlockSpec((tm, D_MODEL), lambda i: (i, 0)),
        pl.BlockSpec((1, D_MODEL), lambda i: (0, 0)),
        pl.BlockSpec((D_MODEL, N), lambda i: (0, 0)),
    ]
    args = [x, g.reshape(1, D_MODEL), w]
    if scaled:
        in_specs.append(pl.BlockSpec((1, N), lambda i: (0, 0)))
        args.append(col_scale.reshape(1, N))
    return pl.pallas_call(
        functools.partial(_norm_matmul_kernel, scaled=scaled, tm=tm, sub=_tile(tm, 256)),
        grid=(L // tm,),
        in_specs=in_specs,
        out_specs=pl.BlockSpec((tm, N), lambda i: (i, 0)),
        out_shape=jax.ShapeDtypeStruct((L, N), out_dtype),
        compiler_params=_cparams("parallel"),
        name="norm_matmul",
    )(*args)


def _odd_proj_kernel(h_ref, w_ref, qg_ref, kg_ref, cos_ref, sin_ref, o_ref, *, heads_per_tile):
    j = pl.program_id(1)
    n_q_tiles = GQA_HEADS // heads_per_tile
    z = jnp.dot(h_ref[...], w_ref[...], preferred_element_type=F32)
    lane = lax.broadcasted_iota(jnp.int32, (1, GQA_HEAD), 1)
    first_half = (lane & (ROPE_AXIS - 1)) < (ROPE_AXIS // 2)

    def normed_rotated(hh, gain, cos, sin):
        sl = slice(hh * GQA_HEAD, (hh + 1) * GQA_HEAD)
        y = _rmsnorm(z[:, sl], gain, EPS)
        partner = jnp.where(first_half, pltpu.roll(y, GQA_HEAD - ROPE_AXIS // 2, 1),
                            pltpu.roll(y, ROPE_AXIS // 2, 1))
        o_ref[:, sl] = (y * cos + partner * sin).astype(o_ref.dtype)

    @pl.when(j < n_q_tiles)
    def _():
        out_scale = GQA_HEAD ** -0.5 * LOG2E
        cos = cos_ref[...] * out_scale
        sin = sin_ref[...] * out_scale
        for hh in range(heads_per_tile):
            normed_rotated(hh, qg_ref[...], cos, sin)

    @pl.when(j >= n_q_tiles)
    def _():
        for hh in range(GQA_KV_HEADS):
            normed_rotated(hh, kg_ref[...], cos_ref[...], sin_ref[...])
        v_cols = slice(GQA_KV_HEADS * GQA_HEAD, 2 * GQA_KV_HEADS * GQA_HEAD)
        o_ref[:, v_cols] = z[:, v_cols].astype(o_ref.dtype)


def _odd_proj(h, w, q_g, k_g, cos, sin):
    L = h.shape[0]
    N = w.shape[1]
    tm = _tile(L, 512)
    heads_per_tile = 2 * GQA_KV_HEADS
    tn = heads_per_tile * GQA_HEAD
    return pl.pallas_call(
        functools.partial(_odd_proj_kernel, heads_per_tile=heads_per_tile),
        grid=(L // tm, N // tn),
        in_specs=[
            pl.BlockSpec((tm, D_MODEL), lambda i, j: (i, 0)),
            pl.BlockSpec((D_MODEL, tn), lambda i, j: (0, j)),
            pl.BlockSpec((1, GQA_HEAD), lambda i, j: (0, 0)),
            pl.BlockSpec((1, GQA_HEAD), lambda i, j: (0, 0)),
            pl.BlockSpec((tm, GQA_HEAD), lambda i, j: (i, 0)),
            pl.BlockSpec((tm, GQA_HEAD), lambda i, j: (i, 0)),
        ],
        out_specs=pl.BlockSpec((tm, tn), lambda i, j: (i, j)),
        out_shape=jax.ShapeDtypeStruct((L, N), BF16),
        compiler_params=_cparams("parallel", "arbitrary"),
        name="odd_proj",
    )(h, w, q_g.reshape(1, GQA_HEAD), k_g.reshape(1, GQA_HEAD), cos, sin)


def _matmul_residual_kernel(a_ref, w_ref, r_ref, *rest, emit):
    y = r_ref[...] + jnp.dot(a_ref[...], w_ref[...], preferred_element_type=F32)
    if emit:
        ng_ref, o_ref, hn_ref = rest
        hn_ref[...] = _rmsnorm(y, ng_ref[...], EPS).astype(hn_ref.dtype)
    else:
        (o_ref,) = rest
    o_ref[...] = y


def _matmul_residual(a, w, res, next_g=None):
    L, K = a.shape
    N = w.shape[1]
    tm = _tile(L, 512)
    emit = next_g is not None
    row = lambda: pl.BlockSpec((tm, N), lambda i: (i, 0))
    in_specs = [pl.BlockSpec((tm, K), lambda i: (i, 0)), pl.BlockSpec((K, N), lambda i: (0, 0)), row()]
    args = [a, w, res]
    if emit:
        in_specs.append(pl.BlockSpec((1, N), lambda i: (0, 0)))
        args.append(next_g.reshape(1, N))
    y_out = jax.ShapeDtypeStruct((L, N), F32)
    return pl.pallas_call(
        functools.partial(_matmul_residual_kernel, emit=emit),
        grid=(L // tm,),
        in_specs=in_specs,
        out_specs=(row(), row()) if emit else row(),
        out_shape=(y_out, jax.ShapeDtypeStruct((L, N), BF16)) if emit else y_out,
        compiler_params=_cparams("parallel"),
        name="matmul_residual",
    )(*args)


def _lane_tile(x, n):
    return jnp.concatenate([x] * n, axis=1) if n > 1 else x


def _softmax_block(t, shift, v_ones, m, acc):
    row_max = jnp.max(t, axis=1, keepdims=True)
    m_new = jnp.maximum(m, row_max if shift is None else row_max + shift)
    alpha = jnp.exp2(m - m_new)
    p = jnp.exp2((t - _lane_tile(m_new if shift is None else m_new - shift, t.shape[1] // LANES)).astype(BF16))
    acc = _lane_tile(alpha, 2) * acc + jnp.dot(p, v_ones, preferred_element_type=F32)
    return m_new, acc


def _softmax_init(m_ref, acc_ref, vo_ref):
    m_ref[...] = jnp.full_like(m_ref, -jnp.inf)
    acc_ref[...] = jnp.zeros_like(acc_ref)
    vo_ref[:, LANES:] = jnp.ones((vo_ref.shape[0], LANES), vo_ref.dtype)


def _softmax_result(acc):
    return acc[:, :LANES] / acc[:, LANES:]


_NT = (((1,), (1,)), ((), ()))


def _gqa_kernel(q_ref, k_ref, v_ref, o_ref, m_ref, acc_ref, vo_ref, *, n_k, tq, tk, rows, keys):
    ki = pl.program_id(2)

    @pl.when(ki == 0)
    def _():
        _softmax_init(m_ref, acc_ref, vo_ref)

    vo_ref[:, :LANES] = v_ref[...]
    for g in range(GQA_GROUP):
        for c in range(tq // rows):
            sl = slice(g * tq + c * rows, g * tq + (c + 1) * rows)
            q = q_ref[c * rows:(c + 1) * rows, g * GQA_HEAD:(g + 1) * GQA_HEAD]
            state = (m_ref[sl, :], acc_ref[sl, :])
            for kk in range(tk // keys):
                ksl = slice(kk * keys, (kk + 1) * keys)
                s = lax.dot_general(q, k_ref[ksl, :], _NT, preferred_element_type=F32)
                state = _softmax_block(s, None, vo_ref[ksl, :], *state)
            m_ref[sl, :], acc_ref[sl, :] = state

    @pl.when(ki == n_k - 1)
    def _():
        for g in range(GQA_GROUP):
            sl = slice(g * tq, (g + 1) * tq)
            o_ref[:, g * GQA_HEAD:(g + 1) * GQA_HEAD] = _softmax_result(acc_ref[sl, :]).astype(o_ref.dtype)


def _gqa_attention(qkv):
    L = qkv.shape[0]
    tq = _tile(L, GQA_TQ)
    tk = _tile(L, GQA_TK)
    n_k = L // tk
    group_cols = GQA_GROUP * GQA_HEAD
    return pl.pallas_call(
        functools.partial(_gqa_kernel, n_k=n_k, tq=tq, tk=tk, rows=_tile(tq, ATTN_ROWS), keys=_tile(tk, ATTN_KEYS)),
        grid=(GQA_KV_HEADS, L // tq, n_k),
        in_specs=[
            pl.BlockSpec((tq, group_cols), lambda h, i, j: (i, h)),
            pl.BlockSpec((tk, GQA_HEAD), lambda h, i, j: (j, GQA_HEADS + h)),
            pl.BlockSpec((tk, GQA_HEAD), lambda h, i, j: (j, GQA_HEADS + GQA_KV_HEADS + h)),
        ],
        out_specs=pl.BlockSpec((tq, group_cols), lambda h, i, j: (i, h)),
        out_shape=jax.ShapeDtypeStruct((L, D_MODEL), BF16),
        scratch_shapes=[
            pltpu.VMEM((GQA_GROUP * tq, LANES), F32),
            pltpu.VMEM((GQA_GROUP * tq, 2 * LANES), F32),
            pltpu.VMEM((tk, 2 * LANES), BF16),
        ],
        compiler_params=_cparams("parallel", "parallel", "arbitrary"),
        name="gqa_attention",
    )(qkv, qkv, qkv)


def _sq_norm_max_kernel(x_ref, o_ref, *, half):
    x = x_ref[...].astype(F32)
    x2 = x * x
    first = lax.broadcasted_iota(jnp.int32, (1, LANES), 1) < half
    blocks = []
    for b in range(x2.shape[1] // LANES):
        xb = x2[:, b * LANES:(b + 1) * LANES]
        n1 = jnp.max(jnp.sum(jnp.where(first, xb, 0.0), axis=1, keepdims=True), axis=0, keepdims=True)
        n2 = jnp.max(jnp.sum(jnp.where(first, 0.0, xb), axis=1, keepdims=True), axis=0, keepdims=True)
        blocks.append(jnp.where(first, n1, n2))
    cur = jnp.concatenate(blocks, axis=1)

    @pl.when(pl.program_id(0) == 0)
    def _():
        o_ref[...] = cur

    @pl.when(pl.program_id(0) > 0)
    def _():
        o_ref[...] = jnp.maximum(o_ref[...], cur)


def _sq_norm_max(x, n_cols, half):
    L = x.shape[0]
    tm = _tile(L, 512)
    return pl.pallas_call(
        functools.partial(_sq_norm_max_kernel, half=half),
        grid=(L // tm,),
        in_specs=[pl.BlockSpec((tm, n_cols), lambda i: (i, 0))],
        out_specs=pl.BlockSpec((1, n_cols), lambda i: (0, 0)),
        out_shape=jax.ShapeDtypeStruct((1, n_cols), F32),
        compiler_params=_cparams("arbitrary"),
        name="sq_norm_max",
    )(x)


def _diff_kernel(klo_ref, khi_ref, slope_ref, q_ref, k_ref, v_ref, lq1_ref, lk1_ref, lq2_ref, lk2_ref, sg_ref,
                 o_ref, qs_ref, m_ref, acc_ref, vo_ref, *, n_k, tq, tk, rows, keys, lambda_init):
    h = pl.program_id(0)
    qi = pl.program_id(1)
    j = pl.program_id(2)
    q0 = qi * tq
    kt = klo_ref[h, qi] + j
    active = kt <= khi_ref[h, qi]
    k0 = kt * tk
    n_c = tq // rows

    @pl.when(j == 0)
    def _():
        _softmax_init(m_ref, acc_ref, vo_ref)
        q = q_ref[...]
        lane = lax.broadcasted_iota(jnp.int32, (1, 2 * DIFF_HEAD), 1)
        zero = jnp.zeros_like(q)
        q1 = jnp.where(lane < DIFF_HEAD, q, zero)
        q2 = jnp.where(lane >= DIFF_HEAD, q, zero)
        for c in range(n_c):
            qs_ref[2 * c * rows:(2 * c + 1) * rows, :] = q1[c * rows:(c + 1) * rows]
            qs_ref[(2 * c + 1) * rows:(2 * c + 2) * rows, :] = q2[c * rows:(c + 1) * rows]

    slope2 = slope_ref[h] * LOG2E
    d0 = lax.broadcasted_iota(jnp.int32, (rows, keys), 0) - lax.broadcasted_iota(jnp.int32, (rows, keys), 1)

    def step(bias_fn):
        vo_ref[:, :LANES] = v_ref[...]
        for c in range(n_c):
            sl = slice(2 * c * rows, 2 * (c + 1) * rows)
            state = (m_ref[sl, :], acc_ref[sl, :])
            for kk in range(tk // keys):
                ksl = slice(kk * keys, (kk + 1) * keys)
                bias, shift = bias_fn(q0 + c * rows - (k0 + kk * keys))
                t = lax.dot_general(qs_ref[sl, :], k_ref[ksl, :], _NT, preferred_element_type=F32)
                t = t + jnp.concatenate([bias, bias], axis=0)
                state = _softmax_block(t, shift, vo_ref[ksl, :], *state)
            m_ref[sl, :], acc_ref[sl, :] = state

    crosses_diagonal = jnp.logical_and(k0 < q0 + tq, q0 < k0 + tk)

    @pl.when(jnp.logical_and(active, crosses_diagonal))
    def _():
        step(lambda off: ((-slope2) * jnp.abs(off + d0).astype(F32), None))

    @pl.when(jnp.logical_and(active, jnp.logical_not(crosses_diagonal)))
    def _():
        coef = jnp.where(q0 >= k0, -slope2, slope2)
        tile = coef * d0.astype(F32)
        step(lambda off: (tile, coef * off.astype(F32)))

    @pl.when(j == n_k - 1)
    def _():
        lam = (jnp.exp(jnp.sum(lq1_ref[...] * lk1_ref[...], axis=-1, keepdims=True))
               - jnp.exp(jnp.sum(lq2_ref[...] * lk2_ref[...], axis=-1, keepdims=True)) + lambda_init)
        for c in range(n_c):
            o1 = _softmax_result(acc_ref[2 * c * rows:(2 * c + 1) * rows, :])
            o2 = _softmax_result(acc_ref[(2 * c + 1) * rows:(2 * c + 2) * rows, :])
            o_ref[c * rows:(c + 1) * rows, :] = (
                _rmsnorm(o1 - lam * o2, sg_ref[...], DIFF_SUBLN_EPS) * (1.0 - lambda_init)).astype(o_ref.dtype)


def _diff_key_ranges(qkv, slopes, tq, tk):
    L = qkv.shape[0]
    n_q, n_k = L // tq, L // tk
    sq = _sq_norm_max(qkv, 2 * DIFF_WIDTH, DIFF_HEAD).reshape(2, DIFF_HEADS, 2, DIFF_HEAD)[..., 0]
    norms = jnp.sqrt(sq)
    bound = jnp.max(norms[0] * norms[1], axis=1) * NORM_BOUND_MARGIN
    radius = (SOFTMAX_UNDERFLOW_LOG2 + 2.0 * bound) / (slopes * LOG2E)
    radius = jnp.where(jnp.isfinite(radius), radius, float(L))[:, None]
    q_first = (jnp.arange(n_q, dtype=F32) * tq)[None, :]
    klo = jnp.clip(jnp.floor((q_first - radius) / tk), 0, n_k - 1).astype(jnp.int32)
    khi = jnp.clip(jnp.floor((q_first + (tq - 1) + radius) / tk), 0, n_k - 1).astype(jnp.int32)
    return klo, khi


def _diff_attention(qkv, lq1, lk1, lq2, lk2, subln_g, layer_idx):
    L = qkv.shape[0]
    tq = _tile(L, DIFF_TQ)
    tk = _tile(L, DIFF_TK)
    n_k = L // tk
    lambda_init = 0.8 - 0.6 * math.exp(-0.3 * layer_idx)
    slopes = jnp.asarray([2.0 ** (-8.0 * (i + 1) / DIFF_HEADS) for i in range(DIFF_HEADS)], dtype=F32)
    klo, khi = _diff_key_ranges(qkv, slopes, tq, tk)
    key_tile = lambda h, i, j, klo, khi: jnp.minimum(klo[h, i] + j, khi[h, i])
    vec = lambda: pl.BlockSpec((1, DIFF_HEAD), lambda h, i, j, klo, khi: (0, 0))
    return pl.pallas_call(
        functools.partial(_diff_kernel, n_k=n_k, tq=tq, tk=tk, rows=_tile(tq, ATTN_ROWS), keys=_tile(tk, ATTN_KEYS),
                          lambda_init=lambda_init),
        grid_spec=pltpu.PrefetchScalarGridSpec(
            num_scalar_prefetch=2,
            grid=(DIFF_HEADS, L // tq, n_k),
            in_specs=[
                pl.BlockSpec(memory_space=pltpu.SMEM),
                pl.BlockSpec((tq, DIFF_VHEAD), lambda h, i, j, klo, khi: (i, h)),
                pl.BlockSpec((tk, DIFF_VHEAD), lambda h, i, j, klo, khi: (key_tile(h, i, j, klo, khi), DIFF_HEADS + h)),
                pl.BlockSpec((tk, DIFF_VHEAD),
                             lambda h, i, j, klo, khi: (key_tile(h, i, j, klo, khi), 2 * DIFF_HEADS + h)),
                vec(), vec(), vec(), vec(),
                pl.BlockSpec((1, DIFF_VHEAD), lambda h, i, j, klo, khi: (0, 0)),
            ],
            out_specs=pl.BlockSpec((tq, DIFF_VHEAD), lambda h, i, j, klo, khi: (i, h)),
            scratch_shapes=[
                pltpu.VMEM((2 * tq, DIFF_VHEAD), BF16),
                pltpu.VMEM((2 * tq, LANES), F32),
                pltpu.VMEM((2 * tq, 2 * LANES), F32),
                pltpu.VMEM((tk, 2 * LANES), BF16),
            ],
        ),
        out_shape=jax.ShapeDtypeStruct((L, DIFF_WIDTH), BF16),
        compiler_params=_cparams("parallel", "parallel", "arbitrary"),
        name="diff_attention",
    )(klo, khi, slopes, qkv, qkv, qkv, lq1.reshape(1, -1), lk1.reshape(1, -1), lq2.reshape(1, -1),
      lk2.reshape(1, -1), subln_g.reshape(1, -1))


def _cross_kernel(q_ref, kv_ref, o_ref, *, scale):
    for h in range(X_HEADS):
        sl = slice(h * X_HEAD, (h + 1) * X_HEAD)
        vsl = slice(D_MODEL + h * X_HEAD, D_MODEL + (h + 1) * X_HEAD)
        s = lax.dot_general(q_ref[:, sl], kv_ref[:, sl], (((1,), (1,)), ((), ())),
                            preferred_element_type=F32) * scale
        e = jnp.exp(s - jnp.max(s, axis=-1, keepdims=True))
        p = e / jnp.sum(e, axis=-1, keepdims=True)
        o_ref[:, sl] = jnp.dot(p.astype(BF16), kv_ref[:, vsl], preferred_element_type=F32).astype(o_ref.dtype)


def _cross_attention(q, kv):
    L = q.shape[0]
    n_mem = kv.shape[0]
    tq = _tile(L, 512)
    return pl.pallas_call(
        functools.partial(_cross_kernel, scale=X_HEAD ** -0.5),
        grid=(L // tq,),
        in_specs=[
            pl.BlockSpec((tq, D_MODEL), lambda i: (i, 0)),
            pl.BlockSpec((n_mem, 2 * D_MODEL), lambda i: (0, 0)),
        ],
        out_specs=pl.BlockSpec((tq, D_MODEL), lambda i: (i, 0)),
        out_shape=jax.ShapeDtypeStruct((L, D_MODEL), BF16),
        compiler_params=_cparams("parallel"),
        name="cross_attention",
    )(q, kv)


def _s5_table_kernel(ar_ref, ai_ref, br_ref, bi_ref, o_ref):
    hi = lax.Precision.HIGHEST
    o_ref[0, 0] = (jnp.dot(ar_ref[0, 0], br_ref[0, 0], preferred_element_type=F32, precision=hi)
                   - jnp.dot(ai_ref[0, 0], bi_ref[0, 0], preferred_element_type=F32, precision=hi))


def _s5_chunk_kernel_table(ca_re, ca_im, bb_re, bb_im):
    T = ca_re.shape[0]
    rows = T * S5_GROUP
    stack = lambda a: a.transpose(1, 2, 0, 3, 4).reshape(2, S5_GROUPS, rows, S5_STATE)
    lhs = pl.BlockSpec((1, 1, rows, S5_STATE), lambda d, g: (d, g, 0, 0))
    rhs = pl.BlockSpec((1, 1, S5_STATE, S5_GROUP), lambda d, g: (d, g, 0, 0))
    out = pl.pallas_call(
        _s5_table_kernel,
        grid=(2, S5_GROUPS),
        in_specs=[lhs, lhs, rhs, rhs],
        out_specs=pl.BlockSpec((1, 1, rows, S5_GROUP), lambda d, g: (d, g, 0, 0)),
        out_shape=jax.ShapeDtypeStruct((2, S5_GROUPS, rows, S5_GROUP), F32),
        compiler_params=_cparams("parallel", "parallel"),
        name="s5_kernel_table",
    )(stack(ca_re), stack(ca_im), bb_re, bb_im)
    return out.reshape(2, S5_GROUPS, T, S5_GROUP, S5_GROUP).transpose(2, 0, 1, 3, 4)


def _s5_weights(lam_re, lam_im, log_dt, b_re, b_im, c_re, c_im):
    T = S5_CHUNK
    lr = jnp.minimum(lam_re, S5_LAMBDA_RE_MAX)
    li = lam_im
    dt = jnp.exp(log_dt)[..., None]
    mag = jnp.exp(lr * dt)
    ab_re = mag * jnp.cos(li * dt)
    ab_im = mag * jnp.sin(li * dt)
    nr = ab_re - 1.0
    den = lr * lr + li * li
    f_re = ((nr * lr + ab_im * li) / den)[..., None]
    f_im = ((ab_im * lr - nr * li) / den)[..., None]
    bb_re = f_re * b_re - f_im * b_im
    bb_im = f_re * b_im + f_im * b_re
    k = jnp.arange(T + 1, dtype=F32)[:, None, None, None]
    pmag = jnp.exp(lr * dt * k)
    pw_re = pmag * jnp.cos(li * dt * k)
    pw_im = pmag * jnp.sin(li * dt * k)
    ca_re = c_re * pw_re[:, :, :, None, :] - c_im * pw_im[:, :, :, None, :]
    ca_im = c_re * pw_im[:, :, :, None, :] + c_im * pw_re[:, :, :, None, :]
    nb, ng = S5_LANE_BLOCKS, S5_LANE_GROUPS

    def group_diagonal(x, rows_per_group):
        w = x.shape[-1]
        reps = jnp.tile(x.astype(BF16), (1, 1, 1, ng))
        row_group = (jnp.arange(x.shape[1]) // rows_per_group) % ng
        col_group = jnp.arange(ng * w) // w
        keep = (row_group[:, None] == col_group[None, :])[None, :, None, :]
        return jnp.where(keep, reps, jnp.zeros((), BF16))

    kern = _s5_chunk_kernel_table(ca_re[:T], ca_im[:T], bb_re, bb_im)
    ktab = jnp.concatenate([kern[1:, 1][::-1], (kern[0, 0] + kern[0, 1])[None], kern[1:, 0]], axis=0)
    kblk = ktab.reshape(2 * T - 1, nb, ng, S5_GROUP, S5_GROUP).transpose(0, 1, 2, 4, 3)
    kblk = group_diagonal(kblk.reshape((2 * T - 1) * nb, LANES, 1, S5_GROUP), S5_GROUP)
    kblk = kblk.reshape(2 * T - 1, nb, LANES, LANES)
    idx = jnp.arange(T)[None, :] - jnp.arange(T)[:, None] + (T - 1)
    toep = kblk[idx].transpose(2, 0, 3, 1, 4).reshape(nb, S5_BLOCK_COLS, S5_BLOCK_COLS)

    e_re = jnp.stack([pw_re[:T, 0][::-1], pw_re[:T, 1]], axis=1)[..., None]
    e_im = jnp.stack([pw_im[:T, 0][::-1], pw_im[:T, 1]], axis=1)[..., None]
    zb = jnp.stack([e_re * bb_re - e_im * bb_im, e_re * bb_im + e_im * bb_re], axis=2)
    z_rows = zb.reshape(T, 2, 2, nb, ng, S5_STATE, S5_GROUP).transpose(3, 0, 4, 6, 1, 2, 5)
    w_in = group_diagonal(z_rows.reshape(nb, S5_BLOCK_COLS, 4, S5_STATE), S5_GROUP)
    w_in = w_in.reshape(nb, S5_BLOCK_COLS, 4 * S5_BLOCK_STATE)

    cr = jnp.stack([ca_re[1:, 0], ca_re[1:, 1][::-1]], axis=1)
    ci = jnp.stack([ca_im[1:, 0], ca_im[1:, 1][::-1]], axis=1)
    c_rows = jnp.stack([cr, -ci], axis=2).reshape(T, 2, 2, nb, ng, S5_GROUP, S5_STATE).transpose(3, 1, 2, 4, 6, 0, 5)
    w_out = group_diagonal(c_rows.reshape(nb, 4 * S5_BLOCK_STATE, T, S5_GROUP), S5_STATE)
    w_out = w_out.reshape(nb, 4 * S5_BLOCK_STATE, S5_BLOCK_COLS)

    a4 = jnp.stack([pw_re[T], pw_im[T]], axis=1).reshape(2, 2, nb, ng, S5_STATE)
    a_t = a4.transpose(2, 0, 1, 3, 4).reshape(nb, 1, 4 * S5_BLOCK_STATE)
    return toep, w_in, w_out, a_t


def _s5_gather_chunks(u_ref, lhs_ref, tr):
    for t in range(S5_CHUNK):
        lhs_ref[:, t * LANES:(t + 1) * LANES] = u_ref[pl.ds(t, tr, stride=S5_CHUNK), :].astype(BF16)


def _s5_in_kernel(u_ref, w_ref, z_ref, lhs_ref, *, tr):
    _s5_gather_chunks(u_ref, lhs_ref, tr)
    z_ref[0] = jnp.dot(lhs_ref[...], w_ref[0], preferred_element_type=F32)


def _s5_out_kernel(u_ref, toep_ref, x_ref, w_ref, d_ref, y_ref, lhs_ref, *, tr):
    _s5_gather_chunks(u_ref, lhs_ref, tr)
    y = (jnp.dot(lhs_ref[...], toep_ref[0], preferred_element_type=F32)
         + jnp.dot(x_ref[0].astype(BF16), w_ref[0], preferred_element_type=F32))
    d = d_ref[...]
    for t in range(S5_CHUNK):
        rows = pl.ds(t, tr, stride=S5_CHUNK)
        y_ref[rows, :] = d * u_ref[rows, :] + y[:, t * LANES:(t + 1) * LANES]


def _s5_scan_kernel(z_ref, a_ref, x_ref, *, n_rows):
    n_s = S5_BLOCK_STATE
    a = a_ref[0]
    af_re, af_im, ab_re, ab_im = (a[:, i * n_s:(i + 1) * n_s] for i in range(4))

    def body(n, carry):
        f_re, f_im, b_re, b_im = carry
        rf = pl.ds(n, 1)
        rb = pl.ds(n_rows - 1 - n, 1)
        x_ref[0, rf, 0 * n_s:1 * n_s] = f_re
        x_ref[0, rf, 1 * n_s:2 * n_s] = f_im
        x_ref[0, rb, 2 * n_s:3 * n_s] = b_re
        x_ref[0, rb, 3 * n_s:4 * n_s] = b_im
        zf_re = z_ref[0, rf, 0 * n_s:1 * n_s]
        zf_im = z_ref[0, rf, 1 * n_s:2 * n_s]
        zb_re = z_ref[0, rb, 2 * n_s:3 * n_s]
        zb_im = z_ref[0, rb, 3 * n_s:4 * n_s]
        return (af_re * f_re - af_im * f_im + zf_re, af_re * f_im + af_im * f_re + zf_im,
                ab_re * b_re - ab_im * b_im + zb_re, ab_re * b_im + ab_im * b_re + zb_im)

    zero = jnp.zeros((1, n_s), F32)
    lax.fori_loop(0, n_rows, body, (zero, zero, zero, zero))


def _s5_glu_kernel(y_ref, w_ref, b_ref, o_ref):
    g = _gelu_tanh(y_ref[...])
    gate = jnp.dot(g.astype(BF16), w_ref[...], preferred_element_type=F32) + b_ref[...]
    o_ref[...] = (g * _sigmoid(gate)).astype(o_ref.dtype)


def _s5_mixer(u, s5w, d_skip, glu_w, glu_b):
    L = u.shape[0]
    toep, w_in, w_out, a_t = s5w
    nb = S5_LANE_BLOCKS
    n_rows = L // S5_CHUNK
    n_st = 4 * S5_BLOCK_STATE
    tr = _tile(n_rows, 256)
    tokens = pl.BlockSpec((tr * S5_CHUNK, LANES), lambda c, i: (i, c))
    blk = lambda cols: pl.BlockSpec((1, tr, cols), lambda c, i: (c, i, 0))
    wgt = lambda rows, cols: pl.BlockSpec((1, rows, cols), lambda c, i: (c, 0, 0))
    chunk_lhs = pltpu.VMEM((tr, S5_BLOCK_COLS), BF16)
    z = pl.pallas_call(
        functools.partial(_s5_in_kernel, tr=tr),
        grid=(nb, n_rows // tr),
        in_specs=[tokens, wgt(S5_BLOCK_COLS, n_st)],
        out_specs=blk(n_st),
        out_shape=jax.ShapeDtypeStruct((nb, n_rows, n_st), F32),
        scratch_shapes=[chunk_lhs],
        compiler_params=_cparams("parallel", "parallel"),
        name="s5_chunk_input",
    )(u, w_in)
    x = pl.pallas_call(
        functools.partial(_s5_scan_kernel, n_rows=n_rows),
        grid=(nb,),
        in_specs=[pl.BlockSpec((1, n_rows, n_st), lambda c: (c, 0, 0)),
                  pl.BlockSpec((1, 1, n_st), lambda c: (c, 0, 0))],
        out_specs=pl.BlockSpec((1, n_rows, n_st), lambda c: (c, 0, 0)),
        out_shape=jax.ShapeDtypeStruct((nb, n_rows, n_st), F32),
        compiler_params=_cparams("parallel"),
        name="s5_chunk_scan",
    )(z, a_t)
    y = pl.pallas_call(
        functools.partial(_s5_out_kernel, tr=tr),
        grid=(nb, n_rows // tr),
        in_specs=[tokens, wgt(S5_BLOCK_COLS, S5_BLOCK_COLS), blk(n_st), wgt(n_st, S5_BLOCK_COLS),
                  pl.BlockSpec((1, LANES), lambda c, i: (0, c))],
        out_specs=tokens,
        out_shape=jax.ShapeDtypeStruct((L, S5_WIDTH), F32),
        scratch_shapes=[chunk_lhs],
        compiler_params=_cparams("parallel", "parallel"),
        name="s5_chunk_output",
    )(u, toep, x, w_out, d_skip.reshape(1, -1))
    tm = _tile(L, 512)
    row = lambda: pl.BlockSpec((tm, S5_WIDTH), lambda i: (i, 0))
    vec = lambda: pl.BlockSpec((1, S5_WIDTH), lambda i: (0, 0))
    return pl.pallas_call(
        _s5_glu_kernel,
        grid=(L // tm,),
        in_specs=[row(), pl.BlockSpec((S5_WIDTH, S5_WIDTH), lambda i: (0, 0)), vec()],
        out_specs=row(),
        out_shape=jax.ShapeDtypeStruct((L, S5_WIDTH), BF16),
        compiler_params=_cparams("parallel"),
        name="s5_glu",
    )(y, glu_w, glu_b.reshape(1, -1))


def _rope_tables(L):
    rows = L // GRID_W
    r = jnp.repeat(jnp.arange(rows, dtype=F32), GRID_W)
    c = jnp.tile(jnp.arange(GRID_W, dtype=F32), rows)
    inv = ROPE_THETA ** (-jnp.arange(0, ROPE_AXIS, 2, dtype=F32) / ROPE_AXIS)
    ar = r[:, None] * inv
    ac = c[:, None] * inv
    cos = jnp.concatenate([jnp.cos(ar), jnp.cos(ar), jnp.cos(ac), jnp.cos(ac)], axis=-1)
    sin = jnp.concatenate([-jnp.sin(ar), jnp.sin(ar), -jnp.sin(ac), jnp.sin(ac)], axis=-1)
    return cos, sin


def _encoder(x, mem, p):
    L = x.shape[0]
    cos, sin = _rope_tables(L)
    for l in range(DEPTH):
        x, h = _ffn(x, p['ffn1_norm'][l], p['ffn1_w_gu'], p['ffn1_w_down'], l, next_g=p['mix_norm'][l])
        if l % 2 == 0:
            e = l // 2
            u = _matmul(h, p['even_w_in_u'][e], F32)
            qkv = _matmul(h, p['even_w_in_qkv'][e], BF16, col_scale=p['diff_q_scale'])
            ya = _s5_mixer(u, p['s5'][e], p['s5_d'][e], p['s5_glu_w'][e], p['s5_glu_b'][e])
            yb = _diff_attention(qkv, p['diff_lambda_q1'][e], p['diff_lambda_k1'][e], p['diff_lambda_q2'][e],
                                 p['diff_lambda_k2'][e], p['diff_subln'][e], l)
            mixed, w_out = jnp.concatenate([ya, yb], axis=-1), p['even_w_out'][e]
        else:
            o = l // 2
            qkv = _odd_proj(h, p['odd_w_in'][o], p['gqa_q_norm'][o], p['gqa_k_norm'][o], cos, sin)
            mixed, w_out = _gqa_attention(qkv), p['odd_w_out'][o]
        x, h = _matmul_residual(mixed, w_out, x, next_g=p['cross_norm'][l])
        q = _matmul(h, p['cross_w_q'][l], BF16)
        kv = _norm_matmul(mem, p['mem_norm'][l], p['cross_w_kv'][l], BF16)
        x = _matmul_residual(_cross_attention(q, kv), p['cross_w_o'][l], x)
        x = _ffn(x, p['ffn2_norm'][l], p['ffn2_w_gu'], p['ffn2_w_down'], l,
                 final_g=p['final_norm'] if l == DEPTH - 1 else None)
    return x


def kernel(x_prompt, x_sample, mem_prompt, mem_sample, ffn1_norm, ffn1_w_gu, ffn1_w_down, mix_norm, even_w_in, even_w_out, s5_lambda_re, s5_lambda_im, s5_log_dt, s5_b_re, s5_b_im, s5_c_re, s5_c_im, s5_d, s5_glu_w, s5_glu_b, diff_lambda_q1, diff_lambda_k1, diff_lambda_q2, diff_lambda_k2, diff_subln, odd_w_in, odd_w_out, gqa_q_norm, gqa_k_norm, cross_norm, mem_norm, cross_w_q, cross_w_kv, cross_w_o, ffn2_norm, ffn2_w_gu, ffn2_w_down, final_norm):
    bf = lambda w: w.astype(BF16)
    diff_q_scale = jnp.concatenate([jnp.full((DIFF_WIDTH,), DIFF_HEAD ** -0.5 * LOG2E, F32),
                                    jnp.ones((2 * DIFF_WIDTH,), F32)])
    p = dict(
        ffn1_norm=ffn1_norm, ffn1_w_gu=bf(ffn1_w_gu), ffn1_w_down=bf(ffn1_w_down), mix_norm=mix_norm,
        even_w_in_u=bf(even_w_in[:, :, :S5_WIDTH]), even_w_in_qkv=bf(even_w_in[:, :, S5_WIDTH:]),
        diff_q_scale=diff_q_scale,
        even_w_out=bf(even_w_out),
        s5=[_s5_weights(s5_lambda_re[e], s5_lambda_im[e], s5_log_dt[e], s5_b_re[e], s5_b_im[e], s5_c_re[e], s5_c_im[e])
            for e in range(s5_lambda_re.shape[0])],
        s5_d=s5_d, s5_glu_w=bf(s5_glu_w), s5_glu_b=s5_glu_b,
        diff_lambda_q1=diff_lambda_q1, diff_lambda_k1=diff_lambda_k1, diff_lambda_q2=diff_lambda_q2,
        diff_lambda_k2=diff_lambda_k2, diff_subln=diff_subln,
        odd_w_in=bf(odd_w_in), odd_w_out=bf(odd_w_out), gqa_q_norm=gqa_q_norm, gqa_k_norm=gqa_k_norm,
        cross_norm=cross_norm, mem_norm=mem_norm, cross_w_q=bf(cross_w_q), cross_w_kv=bf(cross_w_kv),
        cross_w_o=bf(cross_w_o), ffn2_norm=ffn2_norm, ffn2_w_gu=bf(ffn2_w_gu), ffn2_w_down=bf(ffn2_w_down),
        final_norm=final_norm)
    outs = []
    for x, mem in ((x_prompt, mem_prompt), (x_sample, mem_sample)):
        outs.append(jnp.stack([_encoder(x[b], mem[b], p) for b in range(x.shape[0])]))
    return tuple(outs)
```

```python
import functools
import math

import jax
import jax.numpy as jnp
from jax import lax
from jax.experimental import pallas as pl
from jax.experimental.pallas import tpu as pltpu

F32 = jnp.float32
BF16 = jnp.bfloat16

D_MODEL = 2048
DEPTH = 4
EPS = 1e-6
D_FF = 5632
GRID_W = 64

S5_WIDTH = D_MODEL // 2
S5_GROUP = 16
S5_GROUPS = S5_WIDTH // S5_GROUP
S5_STATE = 64
S5_LAMBDA_RE_MAX = -1e-4

DIFF_WIDTH = D_MODEL - S5_WIDTH
DIFF_HEAD = 64
DIFF_HEADS = DIFF_WIDTH // (2 * DIFF_HEAD)
DIFF_VHEAD = 2 * DIFF_HEAD
DIFF_SUBLN_EPS = 1e-5

GQA_HEAD = 128
GQA_HEADS = D_MODEL // GQA_HEAD
GQA_KV_HEADS = 4
GQA_GROUP = GQA_HEADS // GQA_KV_HEADS
ROPE_AXIS = GQA_HEAD // 2
ROPE_THETA = 10000.0

X_HEADS = 4
X_HEAD = D_MODEL // X_HEADS

LOG2E = math.log2(math.e)

GQA_TQ = 512
GQA_TK = 8192
DIFF_TQ = 1024
DIFF_TK = 2048
ATTN_ROWS = 256
ATTN_KEYS = 256

LANES = 128
SOFTMAX_UNDERFLOW_LOG2 = 150.0
NORM_BOUND_MARGIN = 1.01
V7X_VMEM_BYTES = 64 * 1024 * 1024
VMEM_LIMIT_BYTES = V7X_VMEM_BYTES - 8 * 1024 * 1024

S5_CHUNK = 16
S5_LANE_GROUPS = LANES // S5_GROUP
S5_LANE_BLOCKS = S5_WIDTH // LANES
S5_BLOCK_COLS = S5_CHUNK * LANES
S5_BLOCK_STATE = S5_LANE_GROUPS * S5_STATE


def _tile(n, pref):
    t = min(pref, n)
    while n % t:
        t //= 2
    return t


def _cparams(*sem):
    return pltpu.CompilerParams(dimension_semantics=sem, vmem_limit_bytes=VMEM_LIMIT_BYTES)


def _rmsnorm(x, g, eps):
    ms = jnp.mean(x * x, axis=-1, keepdims=True)
    return x * lax.rsqrt(ms + eps) * g


def _sigmoid(x):
    return 1.0 / (1.0 + jnp.exp(-x))


def _gelu_tanh(x):
    c = math.sqrt(2.0 / math.pi)
    return x * (0.5 * (1.0 + jnp.tanh(c * (x + 0.044715 * (x * x * x)))))


def _ffn_kernel(x_ref, g_ref, wg_ref, wu_ref, wd_ref, *rest, n_f, tail):
    if tail is None:
        o_ref, h_ref, acc_ref = rest
    elif tail == 'emit':
        ng_ref, o_ref, hn_ref, h_ref, acc_ref = rest
    else:
        ng_ref, o_ref, h_ref, acc_ref = rest
    j = pl.program_id(1)

    @pl.when(j == 0)
    def _():
        h_ref[...] = _rmsnorm(x_ref[...], g_ref[...], EPS).astype(BF16)
        acc_ref[...] = jnp.zeros_like(acc_ref)

    h = h_ref[...]
    gate = jnp.dot(h, wg_ref[...], preferred_element_type=F32)
    up = jnp.dot(h, wu_ref[...], preferred_element_type=F32)
    act = gate * _sigmoid(gate) * up
    acc_ref[...] += jnp.dot(act.astype(BF16), wd_ref[...], preferred_element_type=F32)

    @pl.when(j == n_f - 1)
    def _():
        y = x_ref[...] + 0.5 * acc_ref[...]
        if tail == 'final':
            o_ref[...] = _rmsnorm(y, ng_ref[...], EPS)
        else:
            o_ref[...] = y
        if tail == 'emit':
            hn_ref[...] = _rmsnorm(y, ng_ref[...], EPS).astype(hn_ref.dtype)


def _ffn(x, g, w_gu, w_down, layer, next_g=None, final_g=None):
    L = x.shape[0]
    tm = _tile(L, 512)
    tf = 512
    n_f = D_FF // tf
    tail = 'final' if final_g is not None else ('emit' if next_g is not None else None)
    row = lambda: pl.BlockSpec((tm, D_MODEL), lambda i, j: (i, 0))
    vec = lambda: pl.BlockSpec((1, D_MODEL), lambda i, j: (0, 0))
    in_specs = [
        row(),
        vec(),
        pl.BlockSpec((None, D_MODEL, tf), lambda i, j: (layer, 0, j)),
        pl.BlockSpec((None, D_MODEL, tf), lambda i, j: (layer, 0, j + n_f)),
        pl.BlockSpec((None, tf, D_MODEL), lambda i, j: (layer, j, 0)),
    ]
    args = [x, g.reshape(1, D_MODEL), w_gu, w_gu, w_down]
    if tail is not None:
        in_specs.append(vec())
        args.append((final_g if tail == 'final' else next_g).reshape(1, D_MODEL))
    x_out = jax.ShapeDtypeStruct((L, D_MODEL), F32)
    emit = tail == 'emit'
    return pl.pallas_call(
        functools.partial(_ffn_kernel, n_f=n_f, tail=tail),
        grid=(L // tm, n_f),
        in_specs=in_specs,
        out_specs=(row(), row()) if emit else row(),
        out_shape=(x_out, jax.ShapeDtypeStruct((L, D_MODEL), BF16)) if emit else x_out,
        scratch_shapes=[pltpu.VMEM((tm, D_MODEL), BF16), pltpu.VMEM((tm, D_MODEL), F32)],
        compiler_params=_cparams("parallel", "arbitrary"),
        name="ffn_final" if tail == 'final' else "ffn",
    )(*args)


def _matmul_kernel(h_ref, w_ref, *rest, scaled):
    if scaled:
        c_ref, o_ref = rest
    else:
        (o_ref,) = rest
    z = jnp.dot(h_ref[...], w_ref[...], preferred_element_type=F32)
    if scaled:
        z = z * c_ref[...]
    o_ref[...] = z.astype(o_ref.dtype)


def _matmul(h, w, out_dtype, col_scale=None):
    L, K = h.shape
    N = w.shape[1]
    tm = _tile(L, 512)
    scaled = col_scale is not None
    in_specs = [pl.BlockSpec((tm, K), lambda i: (i, 0)), pl.BlockSpec((K, N), lambda i: (0, 0))]
    args = [h, w]
    if scaled:
        in_specs.append(pl.BlockSpec((1, N), lambda i: (0, 0)))
        args.append(col_scale.reshape(1, N))
    return pl.pallas_call(
        functools.partial(_matmul_kernel, scaled=scaled),
        grid=(L // tm,),
        in_specs=in_specs,
        out_specs=pl.BlockSpec((tm, N), lambda i: (i, 0)),
        out_shape=jax.ShapeDtypeStruct((L, N), out_dtype),
        compiler_params=_cparams("parallel"),
        name="matmul",
    )(*args)


def _norm_matmul_kernel(x_ref, g_ref, w_ref, *rest, scaled, tm, sub):
    if scaled:
        c_ref, o_ref = rest
    else:
        (o_ref,) = rest
    for c in range(tm // sub):
        rows = slice(c * sub, (c + 1) * sub)
        h = _rmsnorm(x_ref[rows, :], g_ref[...], EPS).astype(BF16)
        z = jnp.dot(h, w_ref[...], preferred_element_type=F32)
        if scaled:
            z = z * c_ref[...]
        o_ref[rows, :] = z.astype(o_ref.dtype)


def _norm_matmul(x, g, w, out_dtype, col_scale=None):
    L = x.shape[0]
    N = w.shape[1]
    tm = _tile(L, 512)
    scaled = col_scale is not None
    in_specs = [
        pl.BlockSpec((tm, D_MODEL), lambda i: (i, 0)),
        pl.BlockSpec((1, D_MODEL), lambda i: (0, 0)),
        pl.BlockSpec((D_MODEL, N), lambda i: (0, 0)),
    ]
    args = [x, g.reshape(1, D_MODEL), w]
    if scaled:
        in_specs.append(pl.BlockSpec((1, N), lambda i: (0, 0)))
        args.append(col_scale.reshape(1, N))
    return pl.pallas_call(
        functools.partial(_norm_matmul_kernel, scaled=scaled, tm=tm, sub=_tile(tm, 256)),
        grid=(L // tm,),
        in_specs=in_specs,
        out_specs=pl.BlockSpec((tm, N), lambda i: (i, 0)),
        out_shape=jax.ShapeDtypeStruct((L, N), out_dtype),
        compiler_params=_cparams("parallel"),
        name="norm_matmul",
    )(*args)


def _odd_proj_kernel(h_ref, w_ref, qg_ref, kg_ref, cos_ref, sin_ref, o_ref, *, heads_per_tile):
    j = pl.program_id(1)
    n_q_tiles = GQA_HEADS // heads_per_tile
    z = jnp.dot(h_ref[...], w_ref[...], preferred_element_type=F32)
    lane = lax.broadcasted_iota(jnp.int32, (1, GQA_HEAD), 1)
    first_half = (lane & (ROPE_AXIS - 1)) < (ROPE_AXIS // 2)

    def normed_rotated(hh, gain, cos, sin):
        sl = slice(hh * GQA_HEAD, (hh + 1) * GQA_HEAD)
        y = _rmsnorm(z[:, sl], gain, EPS)
        partner = jnp.where(first_half, pltpu.roll(y, GQA_HEAD - ROPE_AXIS // 2, 1),
                            pltpu.roll(y, ROPE_AXIS // 2, 1))
        o_ref[:, sl] = (y * cos + partner * sin).astype(o_ref.dtype)

    @pl.when(j < n_q_tiles)
    def _():
        out_scale = GQA_HEAD ** -0.5 * LOG2E
        cos = cos_ref[...] * out_scale
        sin = sin_ref[...] * out_scale
        for hh in range(heads_per_tile):
            normed_rotated(hh, qg_ref[...], cos, sin)

    @pl.when(j >= n_q_tiles)
    def _():
        for hh in range(GQA_KV_HEADS):
            normed_rotated(hh, kg_ref[...], cos_ref[...], sin_ref[...])
        v_cols = slice(GQA_KV_HEADS * GQA_HEAD, 2 * GQA_KV_HEADS * GQA_HEAD)
        o_ref[:, v_cols] = z[:, v_cols].astype(o_ref.dtype)


def _odd_proj(h, w, q_g, k_g, cos, sin):
    L = h.shape[0]
    N = w.shape[1]
    tm = _tile(L, 512)
    heads_per_tile = 2 * GQA_KV_HEADS
    tn = heads_per_tile * GQA_HEAD
    return pl.pallas_call(
        functools.partial(_odd_proj_kernel, heads_per_tile=heads_per_tile),
        grid=(L // tm, N // tn),
        in_specs=[
            pl.BlockSpec((tm, D_MODEL), lambda i, j: (i, 0)),
            pl.BlockSpec((D_MODEL, tn), lambda i, j: (0, j)),
            pl.BlockSpec((1, GQA_HEAD), lambda i, j: (0, 0)),
            pl.BlockSpec((1, GQA_HEAD), lambda i, j: (0, 0)),
            pl.BlockSpec((tm, GQA_HEAD), lambda i, j: (i, 0)),
            pl.BlockSpec((tm, GQA_HEAD), lambda i, j: (i, 0)),
        ],
        out_specs=pl.BlockSpec((tm, tn), lambda i, j: (i, j)),
        out_shape=jax.ShapeDtypeStruct((L, N), BF16),
        compiler_params=_cparams("parallel", "arbitrary"),
        name="odd_proj",
    )(h, w, q_g.reshape(1, GQA_HEAD), k_g.reshape(1, GQA_HEAD), cos, sin)


def _matmul_residual_kernel(a_ref, w_ref, r_ref, *rest, emit):
    y = r_ref[...] + jnp.dot(a_ref[...], w_ref[...], preferred_element_type=F32)
    if emit:
        ng_ref, o_ref, hn_ref = rest
        hn_ref[...] = _rmsnorm(y, ng_ref[...], EPS).astype(hn_ref.dtype)
    else:
        (o_ref,) = rest
    o_ref[...] = y


def _matmul_residual(a, w, res, next_g=None):
    L, K = a.shape
    N = w.shape[1]
    tm = _tile(L, 512)
    emit = next_g is not None
    row = lambda: pl.BlockSpec((tm, N), lambda i: (i, 0))
    in_specs = [pl.BlockSpec((tm, K), lambda i: (i, 0)), pl.BlockSpec((K, N), lambda i: (0, 0)), row()]
    args = [a, w, res]
    if emit:
        in_specs.append(pl.BlockSpec((1, N), lambda i: (0, 0)))
        args.append(next_g.reshape(1, N))
    y_out = jax.ShapeDtypeStruct((L, N), F32)
    return pl.pallas_call(
        functools.partial(_matmul_residual_kernel, emit=emit),
        grid=(L // tm,),
        in_specs=in_specs,
        out_specs=(row(), row()) if emit else row(),
        out_shape=(y_out, jax.ShapeDtypeStruct((L, N), BF16)) if emit else y_out,
        compiler_params=_cparams("parallel"),
        name="matmul_residual",
    )(*args)


def _lane_tile(x, n):
    return jnp.concatenate([x] * n, axis=1) if n > 1 else x


def _softmax_block(t, shift, v_ones, m, acc):
    row_max = jnp.max(t, axis=1, keepdims=True)
    m_new = jnp.maximum(m, row_max if shift is None else row_max + shift)
    alpha = jnp.exp2(m - m_new)
    p = jnp.exp2((t - _lane_tile(m_new if shift is None else m_new - shift, t.shape[1] // LANES)).astype(BF16))
    acc = _lane_tile(alpha, 2) * acc + jnp.dot(p, v_ones, preferred_element_type=F32)
    return m_new, acc


def _softmax_init(m_ref, acc_ref, vo_ref):
    m_ref[...] = jnp.full_like(m_ref, -jnp.inf)
    acc_ref[...] = jnp.zeros_like(acc_ref)
    vo_ref[:, LANES:] = jnp.ones((vo_ref.shape[0], LANES), vo_ref.dtype)


def _softmax_result(acc):
    return acc[:, :LANES] / acc[:, LANES:]


_NT = (((1,), (1,)), ((), ()))


def _gqa_kernel(q_ref, k_ref, v_ref, o_ref, m_ref, acc_ref, vo_ref, *, n_k, tq, tk, rows, keys):
    ki = pl.program_id(2)

    @pl.when(ki == 0)
    def _():
        _softmax_init(m_ref, acc_ref, vo_ref)

    vo_ref[:, :LANES] = v_ref[...]
    for g in range(GQA_GROUP):
        for c in range(tq // rows):
            sl = slice(g * tq + c * rows, g * tq + (c + 1) * rows)
            q = q_ref[c * rows:(c + 1) * rows, g * GQA_HEAD:(g + 1) * GQA_HEAD]
            state = (m_ref[sl, :], acc_ref[sl, :])
            for kk in range(tk // keys):
                ksl = slice(kk * keys, (kk + 1) * keys)
                s = lax.dot_general(q, k_ref[ksl, :], _NT, preferred_element_type=F32)
                state = _softmax_block(s, None, vo_ref[ksl, :], *state)
            m_ref[sl, :], acc_ref[sl, :] = state

    @pl.when(ki == n_k - 1)
    def _():
        for g in range(GQA_GROUP):
            sl = slice(g * tq, (g + 1) * tq)
            o_ref[:, g * GQA_HEAD:(g + 1) * GQA_HEAD] = _softmax_result(acc_ref[sl, :]).astype(o_ref.dtype)


def _gqa_attention(qkv):
    L = qkv.shape[0]
    tq = _tile(L, GQA_TQ)
    tk = _tile(L, GQA_TK)
    n_k = L // tk
    group_cols = GQA_GROUP * GQA_HEAD
    return pl.pallas_call(
        functools.partial(_gqa_kernel, n_k=n_k, tq=tq, tk=tk, rows=_tile(tq, ATTN_ROWS), keys=_tile(tk, ATTN_KEYS)),
        grid=(GQA_KV_HEADS, L // tq, n_k),
        in_specs=[
            pl.BlockSpec((tq, group_cols), lambda h, i, j: (i, h)),
            pl.BlockSpec((tk, GQA_HEAD), lambda h, i, j: (j, GQA_HEADS + h)),
            pl.BlockSpec((tk, GQA_HEAD), lambda h, i, j: (j, GQA_HEADS + GQA_KV_HEADS + h)),
        ],
        out_specs=pl.BlockSpec((tq, group_cols), lambda h, i, j: (i, h)),
        out_shape=jax.ShapeDtypeStruct((L, D_MODEL), BF16),
        scratch_shapes=[
            pltpu.VMEM((GQA_GROUP * tq, LANES), F32),
            pltpu.VMEM((GQA_GROUP * tq, 2 * LANES), F32),
            pltpu.VMEM((tk, 2 * LANES), BF16),
        ],
        compiler_params=_cparams("parallel", "parallel", "arbitrary"),
        name="gqa_attention",
    )(qkv, qkv, qkv)


def _sq_norm_max_kernel(x_ref, o_ref, *, half):
    x = x_ref[...].astype(F32)
    x2 = x * x
    first = lax.broadcasted_iota(jnp.int32, (1, LANES), 1) < half
    blocks = []
    for b in range(x2.shape[1] // LANES):
        xb = x2[:, b * LANES:(b + 1) * LANES]
        n1 = jnp.max(jnp.sum(jnp.where(first, xb, 0.0), axis=1, keepdims=True), axis=0, keepdims=True)
        n2 = jnp.max(jnp.sum(jnp.where(first, 0.0, xb), axis=1, keepdims=True), axis=0, keepdims=True)
        blocks.append(jnp.where(first, n1, n2))
    cur = jnp.concatenate(blocks, axis=1)

    @pl.when(pl.program_id(0) == 0)
    def _():
        o_ref[...] = cur

    @pl.when(pl.program_id(0) > 0)
    def _():
        o_ref[...] = jnp.maximum(o_ref[...], cur)


def _sq_norm_max(x, n_cols, half):
    L = x.shape[0]
    tm = _tile(L, 512)
    return pl.pallas_call(
        functools.partial(_sq_norm_max_kernel, half=half),
        grid=(L // tm,),
        in_specs=[pl.BlockSpec((tm, n_cols), lambda i: (i, 0))],
        out_specs=pl.BlockSpec((1, n_cols), lambda i: (0, 0)),
        out_shape=jax.ShapeDtypeStruct((1, n_cols), F32),
        compiler_params=_cparams("arbitrary"),
        name="sq_norm_max",
    )(x)


def _diff_kernel(klo_ref, khi_ref, slope_ref, q_ref, k_ref, v_ref, lq1_ref, lk1_ref, lq2_ref, lk2_ref, sg_ref,
                 o_ref, qs_ref, m_ref, acc_ref, vo_ref, *, n_k, tq, tk, rows, keys, lambda_init):
    h = pl.program_id(0)
    qi = pl.program_id(1)
    j = pl.program_id(2)
    q0 = qi * tq
    kt = klo_ref[h, qi] + j
    active = kt <= khi_ref[h, qi]
    k0 = kt * tk
    n_c = tq // rows

    @pl.when(j == 0)
    def _():
        _softmax_init(m_ref, acc_ref, vo_ref)
        q = q_ref[...]
        lane = lax.broadcasted_iota(jnp.int32, (1, 2 * DIFF_HEAD), 1)
        zero = jnp.zeros_like(q)
        q1 = jnp.where(lane < DIFF_HEAD, q, zero)
        q2 = jnp.where(lane >= DIFF_HEAD, q, zero)
        for c in range(n_c):
            qs_ref[2 * c * rows:(2 * c + 1) * rows, :] = q1[c * rows:(c + 1) * rows]
            qs_ref[(2 * c + 1) * rows:(2 * c + 2) * rows, :] = q2[c * rows:(c + 1) * rows]

    slope2 = slope_ref[h] * LOG2E
    d0 = lax.broadcasted_iota(jnp.int32, (rows, keys), 0) - lax.broadcasted_iota(jnp.int32, (rows, keys), 1)

    def step(bias_fn):
        vo_ref[:, :LANES] = v_ref[...]
        for c in range(n_c):
            sl = slice(2 * c * rows, 2 * (c + 1) * rows)
            state = (m_ref[sl, :], acc_ref[sl, :])
            for kk in range(tk // keys):
                ksl = slice(kk * keys, (kk + 1) * keys)
                bias, shift = bias_fn(q0 + c * rows - (k0 + kk * keys))
                t = lax.dot_general(qs_ref[sl, :], k_ref[ksl, :], _NT, preferred_element_type=F32)
                t = t + jnp.concatenate([bias, bias], axis=0)
                state = _softmax_block(t, shift, vo_ref[ksl, :], *state)
            m_ref[sl, :], acc_ref[sl, :] = state

    crosses_diagonal = jnp.logical_and(k0 < q0 + tq, q0 < k0 + tk)

    @pl.when(jnp.logical_and(active, crosses_diagonal))
    def _():
        step(lambda off: ((-slope2) * jnp.abs(off + d0).astype(F32), None))

    @pl.when(jnp.logical_and(active, jnp.logical_not(crosses_diagonal)))
    def _():
        coef = jnp.where(q0 >= k0, -slope2, slope2)
        tile = coef * d0.astype(F32)
        step(lambda off: (tile, coef * off.astype(F32)))

    @pl.when(j == n_k - 1)
    def _():
        lam = (jnp.exp(jnp.sum(lq1_ref[...] * lk1_ref[...], axis=-1, keepdims=True))
               - jnp.exp(jnp.sum(lq2_ref[...] * lk2_ref[...], axis=-1, keepdims=True)) + lambda_init)
        for c in range(n_c):
            o1 = _softmax_result(acc_ref[2 * c * rows:(2 * c + 1) * rows, :])
            o2 = _softmax_result(acc_ref[(2 * c + 1) * rows:(2 * c + 2) * rows, :])
            o_ref[c * rows:(c + 1) * rows, :] = (
                _rmsnorm(o1 - lam * o2, sg_ref[...], DIFF_SUBLN_EPS) * (1.0 - lambda_init)).astype(o_ref.dtype)


def _diff_key_ranges(qkv, slopes, tq, tk):
    L = qkv.shape[0]
    n_q, n_k = L // tq, L // tk
    sq = _sq_norm_max(qkv, 2 * DIFF_WIDTH, DIFF_HEAD).reshape(2, DIFF_HEADS, 2, DIFF_HEAD)[..., 0]
    norms = jnp.sqrt(sq)
    bound = jnp.max(norms[0] * norms[1], axis=1) * NORM_BOUND_MARGIN
    radius = (SOFTMAX_UNDERFLOW_LOG2 + 2.0 * bound) / (slopes * LOG2E)
    radius = jnp.where(jnp.isfinite(radius), radius, float(L))[:, None]
    q_first = (jnp.arange(n_q, dtype=F32) * tq)[None, :]
    klo = jnp.clip(jnp.floor((q_first - radius) / tk), 0, n_k - 1).astype(jnp.int32)
    khi = jnp.clip(jnp.floor((q_first + (tq - 1) + radius) / tk), 0, n_k - 1).astype(jnp.int32)
    return klo, khi


def _diff_attention(qkv, lq1, lk1, lq2, lk2, subln_g, layer_idx):
    L = qkv.shape[0]
    tq = _tile(L, DIFF_TQ)
    tk = _tile(L, DIFF_TK)
    n_k = L // tk
    lambda_init = 0.8 - 0.6 * math.exp(-0.3 * layer_idx)
    slopes = jnp.asarray([2.0 ** (-8.0 * (i + 1) / DIFF_HEADS) for i in range(DIFF_HEADS)], dtype=F32)
    klo, khi = _diff_key_ranges(qkv, slopes, tq, tk)
    key_tile = lambda h, i, j, klo, khi: jnp.minimum(klo[h, i] + j, khi[h, i])
    vec = lambda: pl.BlockSpec((1, DIFF_HEAD), lambda h, i, j, klo, khi: (0, 0))
    return pl.pallas_call(
        functools.partial(_diff_kernel, n_k=n_k, tq=tq, tk=tk, rows=_tile(tq, ATTN_ROWS), keys=_tile(tk, ATTN_KEYS),
                          lambda_init=lambda_init),
        grid_spec=pltpu.PrefetchScalarGridSpec(
            num_scalar_prefetch=2,
            grid=(DIFF_HEADS, L // tq, n_k),
            in_specs=[
                pl.BlockSpec(memory_space=pltpu.SMEM),
                pl.BlockSpec((tq, DIFF_VHEAD), lambda h, i, j, klo, khi: (i, h)),
                pl.BlockSpec((tk, DIFF_VHEAD), lambda h, i, j, klo, khi: (key_tile(h, i, j, klo, khi), DIFF_HEADS + h)),
                pl.BlockSpec((tk, DIFF_VHEAD),
                             lambda h, i, j, klo, khi: (key_tile(h, i, j, klo, khi), 2 * DIFF_HEADS + h)),
                vec(), vec(), vec(), vec(),
                pl.BlockSpec((1, DIFF_VHEAD), lambda h, i, j, klo, khi: (0, 0)),
            ],
            out_specs=pl.BlockSpec((tq, DIFF_VHEAD), lambda h, i, j, klo, khi: (i, h)),
            scratch_shapes=[
                pltpu.VMEM((2 * tq, DIFF_VHEAD), BF16),
                pltpu.VMEM((2 * tq, LANES), F32),
                pltpu.VMEM((2 * tq, 2 * LANES), F32),
                pltpu.VMEM((tk, 2 * LANES), BF16),
            ],
        ),
        out_shape=jax.ShapeDtypeStruct((L, DIFF_WIDTH), BF16),
        compiler_params=_cparams("parallel", "parallel", "arbitrary"),
        name="diff_attention",
    )(klo, khi, slopes, qkv, qkv, qkv, lq1.reshape(1, -1), lk1.reshape(1, -1), lq2.reshape(1, -1),
      lk2.reshape(1, -1), subln_g.reshape(1, -1))


def _cross_kernel(q_ref, kv_ref, o_ref, *, scale):
    for h in range(X_HEADS):
        sl = slice(h * X_HEAD, (h + 1) * X_HEAD)
        vsl = slice(D_MODEL + h * X_HEAD, D_MODEL + (h + 1) * X_HEAD)
        s = lax.dot_general(q_ref[:, sl], kv_ref[:, sl], (((1,), (1,)), ((), ())),
                            preferred_element_type=F32) * scale
        e = jnp.exp(s - jnp.max(s, axis=-1, keepdims=True))
        p = e / jnp.sum(e, axis=-1, keepdims=True)
        o_ref[:, sl] = jnp.dot(p.astype(BF16), kv_ref[:, vsl], preferred_element_type=F32).astype(o_ref.dtype)


def _cross_attention(q, kv):
    L = q.shape[0]
    n_mem = kv.shape[0]
    tq = _tile(L, 512)
    return pl.pallas_call(
        functools.partial(_cross_kernel, scale=X_HEAD ** -0.5),
        grid=(L // tq,),
        in_specs=[
            pl.BlockSpec((tq, D_MODEL), lambda i: (i, 0)),
            pl.BlockSpec((n_mem, 2 * D_MODEL), lambda i: (0, 0)),
        ],
        out_specs=pl.BlockSpec((tq, D_MODEL), lambda i: (i, 0)),
        out_shape=jax.ShapeDtypeStruct((L, D_MODEL), BF16),
        compiler_params=_cparams("parallel"),
        name="cross_attention",
    )(q, kv)


def _s5_table_kernel(ar_ref, ai_ref, br_ref, bi_ref, o_ref):
    hi = lax.Precision.HIGHEST
    o_ref[0, 0] = (jnp.dot(ar_ref[0, 0], br_ref[0, 0], preferred_element_type=F32, precision=hi)
                   - jnp.dot(ai_ref[0, 0], bi_ref[0, 0], preferred_element_type=F32, precision=hi))


def _s5_chunk_kernel_table(ca_re, ca_im, bb_re, bb_im):
    T = ca_re.shape[0]
    rows = T * S5_GROUP
    stack = lambda a: a.transpose(1, 2, 0, 3, 4).reshape(2, S5_GROUPS, rows, S5_STATE)
    lhs = pl.BlockSpec((1, 1, rows, S5_STATE), lambda d, g: (d, g, 0, 0))
    rhs = pl.BlockSpec((1, 1, S5_STATE, S5_GROUP), lambda d, g: (d, g, 0, 0))
    out = pl.pallas_call(
        _s5_table_kernel,
        grid=(2, S5_GROUPS),
        in_specs=[lhs, lhs, rhs, rhs],
        out_specs=pl.BlockSpec((1, 1, rows, S5_GROUP), lambda d, g: (d, g, 0, 0)),
        out_shape=jax.ShapeDtypeStruct((2, S5_GROUPS, rows, S5_GROUP), F32),
        compiler_params=_cparams("parallel", "parallel"),
        name="s5_kernel_table",
    )(stack(ca_re), stack(ca_im), bb_re, bb_im)
    return out.reshape(2, S5_GROUPS, T, S5_GROUP, S5_GROUP).transpose(2, 0, 1, 3, 4)


def _s5_weights(lam_re, lam_im, log_dt, b_re, b_im, c_re, c_im):
    T = S5_CHUNK
    lr = jnp.minimum(lam_re, S5_LAMBDA_RE_MAX)
    li = lam_im
    dt = jnp.exp(log_dt)[..., None]
    mag = jnp.exp(lr * dt)
    ab_re = mag * jnp.cos(li * dt)
    ab_im = mag * jnp.sin(li * dt)
    nr = ab_re - 1.0
    den = lr * lr + li * li
    f_re = ((nr * lr + ab_im * li) / den)[..., None]
    f_im = ((ab_im * lr - nr * li) / den)[..., None]
    bb_re = f_re * b_re - f_im * b_im
    bb_im = f_re * b_im + f_im * b_re
    k = jnp.arange(T + 1, dtype=F32)[:, None, None, None]
    pmag = jnp.exp(lr * dt * k)
    pw_re = pmag * jnp.cos(li * dt * k)
    pw_im = pmag * jnp.sin(li * dt * k)
    ca_re = c_re * pw_re[:, :, :, None, :] - c_im * pw_im[:, :, :, None, :]
    ca_im = c_re * pw_im[:, :, :, None, :] + c_im * pw_re[:, :, :, None, :]
    nb, ng = S5_LANE_BLOCKS, S5_LANE_GROUPS

    def group_diagonal(x, rows_per_group):
        w = x.shape[-1]
        reps = jnp.tile(x.astype(BF16), (1, 1, 1, ng))
        row_group = (jnp.arange(x.shape[1]) // rows_per_group) % ng
        col_group = jnp.arange(ng * w) // w
        keep = (row_group[:, None] == col_group[None, :])[None, :, None, :]
        return jnp.where(keep, reps, jnp.zeros((), BF16))

    kern = _s5_chunk_kernel_table(ca_re[:T], ca_im[:T], bb_re, bb_im)
    ktab = jnp.concatenate([kern[1:, 1][::-1], (kern[0, 0] + kern[0, 1])[None], kern[1:, 0]], axis=0)
    kblk = ktab.reshape(2 * T - 1, nb, ng, S5_GROUP, S5_GROUP).transpose(0, 1, 2, 4, 3)
    kblk = group_diagonal(kblk.reshape((2 * T - 1) * nb, LANES, 1, S5_GROUP), S5_GROUP)
    kblk = kblk.reshape(2 * T - 1, nb, LANES, LANES)
    idx = jnp.arange(T)[None, :] - jnp.arange(T)[:, None] + (T - 1)
    toep = kblk[idx].transpose(2, 0, 3, 1, 4).reshape(nb, S5_BLOCK_COLS, S5_BLOCK_COLS)

    e_re = jnp.stack([pw_re[:T, 0][::-1], pw_re[:T, 1]], axis=1)[..., None]
    e_im = jnp.stack([pw_im[:T, 0][::-1], pw_im[:T, 1]], axis=1)[..., None]
    zb = jnp.stack([e_re * bb_re - e_im * bb_im, e_re * bb_im + e_im * bb_re], axis=2)
    z_rows = zb.reshape(T, 2, 2, nb, ng, S5_STATE, S5_GROUP).transpose(3, 0, 4, 6, 1, 2, 5)
    w_in = group_diagonal(z_rows.reshape(nb, S5_BLOCK_COLS, 4, S5_STATE), S5_GROUP)
    w_in = w_in.reshape(nb, S5_BLOCK_COLS, 4 * S5_BLOCK_STATE)

    cr = jnp.stack([ca_re[1:, 0], ca_re[1:, 1][::-1]], axis=1)
    ci = jnp.stack([ca_im[1:, 0], ca_im[1:, 1][::-1]], axis=1)
    c_rows = jnp.stack([cr, -ci], axis=2).reshape(T, 2, 2, nb, ng, S5_GROUP, S5_STATE).transpose(3, 1, 2, 4, 6, 0, 5)
    w_out = group_diagonal(c_rows.reshape(nb, 4 * S5_BLOCK_STATE, T, S5_GROUP), S5_STATE)
    w_out = w_out.reshape(nb, 4 * S5_BLOCK_STATE, S5_BLOCK_COLS)

    a4 = jnp.stack([pw_re[T], pw_im[T]], axis=1).reshape(2, 2, nb, ng, S5_STATE)
    a_t = a4.transpose(2, 0, 1, 3, 4).reshape(nb, 1, 4 * S5_BLOCK_STATE)
    return toep, w_in, w_out, a_t


def _s5_gather_chunks(u_ref, lhs_ref, tr):
    for t in range(S5_CHUNK):
        lhs_ref[:, t * LANES:(t + 1) * LANES] = u_ref[pl.ds(t, tr, stride=S5_CHUNK), :].astype(BF16)


def _s5_in_kernel(u_ref, w_ref, z_ref, lhs_ref, *, tr):
    _s5_gather_chunks(u_ref, lhs_ref, tr)
    z_ref[0] = jnp.dot(lhs_ref[...], w_ref[0], preferred_element_type=F32)


def _s5_out_kernel(u_ref, toep_ref, x_ref, w_ref, d_ref, y_ref, lhs_ref, *, tr):
    _s5_gather_chunks(u_ref, lhs_ref, tr)
    y = (jnp.dot(lhs_ref[...], toep_ref[0], preferred_element_type=F32)
         + jnp.dot(x_ref[0].astype(BF16), w_ref[0], preferred_element_type=F32))
    d = d_ref[...]
    for t in range(S5_CHUNK):
        rows = pl.ds(t, tr, stride=S5_CHUNK)
        y_ref[rows, :] = d * u_ref[rows, :] + y[:, t * LANES:(t + 1) * LANES]


def _s5_scan_kernel(z_ref, a_ref, x_ref, *, n_rows):
    n_s = S5_BLOCK_STATE
    a = a_ref[0]
    af_re, af_im, ab_re, ab_im = (a[:, i * n_s:(i + 1) * n_s] for i in range(4))

    def body(n, carry):
        f_re, f_im, b_re, b_im = carry
        rf = pl.ds(n, 1)
        rb = pl.ds(n_rows - 1 - n, 1)
        x_ref[0, rf, 0 * n_s:1 * n_s] = f_re
        x_ref[0, rf, 1 * n_s:2 * n_s] = f_im
        x_ref[0, rb, 2 * n_s:3 * n_s] = b_re
        x_ref[0, rb, 3 * n_s:4 * n_s] = b_im
        zf_re = z_ref[0, rf, 0 * n_s:1 * n_s]
        zf_im = z_ref[0, rf, 1 * n_s:2 * n_s]
        zb_re = z_ref[0, rb, 2 * n_s:3 * n_s]
        zb_im = z_ref[0, rb, 3 * n_s:4 * n_s]
        return (af_re * f_re - af_im * f_im + zf_re, af_re * f_im + af_im * f_re + zf_im,
                ab_re * b_re - ab_im * b_im + zb_re, ab_re * b_im + ab_im * b_re + zb_im)

    zero = jnp.zeros((1, n_s), F32)
    lax.fori_loop(0, n_rows, body, (zero, zero, zero, zero))


def _s5_glu_kernel(y_ref, w_ref, b_ref, o_ref):
    g = _gelu_tanh(y_ref[...])
    gate = jnp.dot(g.astype(BF16), w_ref[...], preferred_element_type=F32) + b_ref[...]
    o_ref[...] = (g * _sigmoid(gate)).astype(o_ref.dtype)


def _s5_mixer(u, s5w, d_skip, glu_w, glu_b):
    L = u.shape[0]
    toep, w_in, w_out, a_t = s5w
    nb = S5_LANE_BLOCKS
    n_rows = L // S5_CHUNK
    n_st = 4 * S5_BLOCK_STATE
    tr = _tile(n_rows, 256)
    tokens = pl.BlockSpec((tr * S5_CHUNK, LANES), lambda c, i: (i, c))
    blk = lambda cols: pl.BlockSpec((1, tr, cols), lambda c, i: (c, i, 0))
    wgt = lambda rows, cols: pl.BlockSpec((1, rows, cols), lambda c, i: (c, 0, 0))
    chunk_lhs = pltpu.VMEM((tr, S5_BLOCK_COLS), BF16)
    z = pl.pallas_call(
        functools.partial(_s5_in_kernel, tr=tr),
        grid=(nb, n_rows // tr),
        in_specs=[tokens, wgt(S5_BLOCK_COLS, n_st)],
        out_specs=blk(n_st),
        out_shape=jax.ShapeDtypeStruct((nb, n_rows, n_st), F32),
        scratch_shapes=[chunk_lhs],
        compiler_params=_cparams("parallel", "parallel"),
        name="s5_chunk_input",
    )(u, w_in)
    x = pl.pallas_call(
        functools.partial(_s5_scan_kernel, n_rows=n_rows),
        grid=(nb,),
        in_specs=[pl.BlockSpec((1, n_rows, n_st), lambda c: (c, 0, 0)),
                  pl.BlockSpec((1, 1, n_st), lambda c: (c, 0, 0))],
        out_specs=pl.BlockSpec((1, n_rows, n_st), lambda c: (c, 0, 0)),
        out_shape=jax.ShapeDtypeStruct((nb, n_rows, n_st), F32),
        compiler_params=_cparams("parallel"),
        name="s5_chunk_scan",
    )(z, a_t)
    y = pl.pallas_call(
        functools.partial(_s5_out_kernel, tr=tr),
        grid=(nb, n_rows // tr),
        in_specs=[tokens, wgt(S5_BLOCK_COLS, S5_BLOCK_COLS), blk(n_st), wgt(n_st, S5_BLOCK_COLS),
                  pl.BlockSpec((1, LANES), lambda c, i: (0, c))],
        out_specs=tokens,
        out_shape=jax.ShapeDtypeStruct((L, S5_WIDTH), F32),
        scratch_shapes=[chunk_lhs],
        compiler_params=_cparams("parallel", "parallel"),
        name="s5_chunk_output",
    )(u, toep, x, w_out, d_skip.reshape(1, -1))
    tm = _tile(L, 512)
    row = lambda: pl.BlockSpec((tm, S5_WIDTH), lambda i: (i, 0))
    vec = lambda: pl.BlockSpec((1, S5_WIDTH), lambda i: (0, 0))
    return pl.pallas_call(
        _s5_glu_kernel,
        grid=(L // tm,),
        in_specs=[row(), pl.BlockSpec((S5_WIDTH, S5_WIDTH), lambda i: (0, 0)), vec()],
        out_specs=row(),
        out_shape=jax.ShapeDtypeStruct((L, S5_WIDTH), BF16),
        compiler_params=_cparams("parallel"),
        name="s5_glu",
    )(y, glu_w, glu_b.reshape(1, -1))


def _rope_tables(L):
    rows = L // GRID_W
    r = jnp.repeat(jnp.arange(rows, dtype=F32), GRID_W)
    c = jnp.tile(jnp.arange(GRID_W, dtype=F32), rows)
    inv = ROPE_THETA ** (-jnp.arange(0, ROPE_AXIS, 2, dtype=F32) / ROPE_AXIS)
    ar = r[:, None] * inv
    ac = c[:, None] * inv
    cos = jnp.concatenate([jnp.cos(ar), jnp.cos(ar), jnp.cos(ac), jnp.cos(ac)], axis=-1)
    sin = jnp.concatenate([-jnp.sin(ar), jnp.sin(ar), -jnp.sin(ac), jnp.sin(ac)], axis=-1)
    return cos, sin


def _encoder(x, mem, p):
    L = x.shape[0]
    cos, sin = _rope_tables(L)
    for l in range(DEPTH):
        x, h = _ffn(x, p['ffn1_norm'][l], p['ffn1_w_gu'], p['ffn1_w_down'], l, next_g=p['mix_norm'][l])
        if l % 2 == 0:
            e = l // 2
            u = _matmul(h, p['even_w_in_u'][e], F32)
            qkv = _matmul(h, p['even_w_in_qkv'][e], BF16, col_scale=p['diff_q_scale'])
            ya = _s5_mixer(u, p['s5'][e], p['s5_d'][e], p['s5_glu_w'][e], p['s5_glu_b'][e])
            yb = _diff_attention(qkv, p['diff_lambda_q1'][e], p['diff_lambda_k1'][e], p['diff_lambda_q2'][e],
                                 p['diff_lambda_k2'][e], p['diff_subln'][e], l)
            mixed, w_out = jnp.concatenate([ya, yb], axis=-1), p['even_w_out'][e]
        else:
            o = l // 2
            qkv = _odd_proj(h, p['odd_w_in'][o], p['gqa_q_norm'][o], p['gqa_k_norm'][o], cos, sin)
            mixed, w_out = _gqa_attention(qkv), p['odd_w_out'][o]
        x, h = _matmul_residual(mixed, w_out, x, next_g=p['cross_norm'][l])
        q = _matmul(h, p['cross_w_q'][l], BF16)
        kv = _norm_matmul(mem, p['mem_norm'][l], p['cross_w_kv'][l], BF16)
        x = _matmul_residual(_cross_attention(q, kv), p['cross_w_o'][l], x)
        x = _ffn(x, p['ffn2_norm'][l], p['ffn2_w_gu'], p['ffn2_w_down'], l,
                 final_g=p['final_norm'] if l == DEPTH - 1 else None)
    return x


def kernel(x_prompt, x_sample, mem_prompt, mem_sample, ffn1_norm, ffn1_w_gu, ffn1_w_down, mix_norm, even_w_in, even_w_out, s5_lambda_re, s5_lambda_im, s5_log_dt, s5_b_re, s5_b_im, s5_c_re, s5_c_im, s5_d, s5_glu_w, s5_glu_b, diff_lambda_q1, diff_lambda_k1, diff_lambda_q2, diff_lambda_k2, diff_subln, odd_w_in, odd_w_out, gqa_q_norm, gqa_k_norm, cross_norm, mem_norm, cross_w_q, cross_w_kv, cross_w_o, ffn2_norm, ffn2_w_gu, ffn2_w_down, final_norm):
    bf = lambda w: w.astype(BF16)
    diff_q_scale = jnp.concatenate([jnp.full((DIFF_WIDTH,), DIFF_HEAD ** -0.5 * LOG2E, F32),
                                    jnp.ones((2 * DIFF_WIDTH,), F32)])
    p = dict(
        ffn1_norm=ffn1_norm, ffn1_w_gu=bf(ffn1_w_gu), ffn1_w_down=bf(ffn1_w_down), mix_norm=mix_norm,
        even_w_in_u=bf(even_w_in[:, :, :S5_WIDTH]), even_w_in_qkv=bf(even_w_in[:, :, S5_WIDTH:]),
        diff_q_scale=diff_q_scale,
        even_w_out=bf(even_w_out),
        s5=[_s5_weights(s5_lambda_re[e], s5_lambda_im[e], s5_log_dt[e], s5_b_re[e], s5_b_im[e], s5_c_re[e], s5_c_im[e])
            for e in range(s5_lambda_re.shape[0])],
        s5_d=s5_d, s5_glu_w=bf(s5_glu_w), s5_glu_b=s5_glu_b,
        diff_lambda_q1=diff_lambda_q1, diff_lambda_k1=diff_lambda_k1, diff_lambda_q2=diff_lambda_q2,
        diff_lambda_k2=diff_lambda_k2, diff_subln=diff_subln,
        odd_w_in=bf(odd_w_in), odd_w_out=bf(odd_w_out), gqa_q_norm=gqa_q_norm, gqa_k_norm=gqa_k_norm,
        cross_norm=cross_norm, mem_norm=mem_norm, cross_w_q=bf(cross_w_q), cross_w_kv=bf(cross_w_kv),
        cross_w_o=bf(cross_w_o), ffn2_norm=ffn2_norm, ffn2_w_gu=bf(ffn2_w_gu), ffn2_w_down=bf(ffn2_w_down),
        final_norm=final_norm)
    outs = []
    for x, mem in ((x_prompt, mem_prompt), (x_sample, mem_sample)):
        outs.append(jnp.stack([_encoder(x[b], mem[b], p) for b in range(x.shape[0])]))
    return tuple(outs)
```

```python
import functools
import math

import jax
import jax.numpy as jnp
from jax import lax
from jax.experimental import pallas as pl
from jax.experimental.pallas import tpu as pltpu

F32 = jnp.float32
BF16 = jnp.bfloat16

D_MODEL = 2048
DEPTH = 4
EPS = 1e-6
D_FF = 5632
GRID_W = 64

S5_WIDTH = D_MODEL // 2
S5_GROUP = 16
S5_GROUPS = S5_WIDTH // S5_GROUP
S5_STATE = 64
S5_LAMBDA_RE_MAX = -1e-4

DIFF_WIDTH = D_MODEL - S5_WIDTH
DIFF_HEAD = 64
DIFF_HEADS = DIFF_WIDTH // (2 * DIFF_HEAD)
DIFF_VHEAD = 2 * DIFF_HEAD
DIFF_SUBLN_EPS = 1e-5

GQA_HEAD = 128
GQA_HEADS = D_MODEL // GQA_HEAD
GQA_KV_HEADS = 4
GQA_GROUP = GQA_HEADS // GQA_KV_HEADS
ROPE_AXIS = GQA_HEAD // 2
ROPE_THETA = 10000.0

X_HEADS = 4
X_HEAD = D_MODEL // X_HEADS

LOG2E = math.log2(math.e)

GQA_TQ = 512
GQA_TK = 8192
DIFF_TQ = 1024
DIFF_TK = 2048
ATTN_ROWS = 256
ATTN_KEYS = 256

LANES = 128
SOFTMAX_UNDERFLOW_LOG2 = 150.0
NORM_BOUND_MARGIN = 1.01
V7X_VMEM_BYTES = 64 * 1024 * 1024
VMEM_LIMIT_BYTES = V7X_VMEM_BYTES - 8 * 1024 * 1024

S5_CHUNK = 16
S5_LANE_GROUPS = LANES // S5_GROUP
S5_LANE_BLOCKS = S5_WIDTH // LANES
S5_BLOCK_COLS = S5_CHUNK * LANES
S5_BLOCK_STATE = S5_LANE_GROUPS * S5_STATE


def _tile(n, pref):
    t = min(pref, n)
    while n % t:
        t //= 2
    return t


def _cparams(*sem):
    return pltpu.CompilerParams(dimension_semantics=sem, vmem_limit_bytes=VMEM_LIMIT_BYTES)


def _rmsnorm(x, g, eps):
    ms = jnp.mean(x * x, axis=-1, keepdims=True)
    return x * lax.rsqrt(ms + eps) * g


def _sigmoid(x):
    return 1.0 / (1.0 + jnp.exp(-x))


def _gelu_tanh(x):
    c = math.sqrt(2.0 / math.pi)
    return x * (0.5 * (1.0 + jnp.tanh(c * (x + 0.044715 * (x * x * x)))))


def _ffn_kernel(x_ref, g_ref, wg_ref, wu_ref, wd_ref, *rest, n_f, tail):
    if tail is None:
        o_ref, h_ref, acc_ref = rest
    elif tail == 'emit':
        ng_ref, o_ref, hn_ref, h_ref, acc_ref = rest
    else:
        ng_ref, o_ref, h_ref, acc_ref = rest
    j = pl.program_id(1)

    @pl.when(j == 0)
    def _():
        h_ref[...] = _rmsnorm(x_ref[...], g_ref[...], EPS).astype(BF16)
        acc_ref[...] = jnp.zeros_like(acc_ref)

    h = h_ref[...]
    gate = jnp.dot(h, wg_ref[...], preferred_element_type=F32)
    up = jnp.dot(h, wu_ref[...], preferred_element_type=F32)
    act = gate * _sigmoid(gate) * up
    acc_ref[...] += jnp.dot(act.astype(BF16), wd_ref[...], preferred_element_type=F32)

    @pl.when(j == n_f - 1)
    def _():
        y = x_ref[...] + 0.5 * acc_ref[...]
        if tail == 'final':
            o_ref[...] = _rmsnorm(y, ng_ref[...], EPS)
        else:
            o_ref[...] = y
        if tail == 'emit':
            hn_ref[...] = _rmsnorm(y, ng_ref[...], EPS).astype(hn_ref.dtype)


def _ffn(x, g, w_gu, w_down, layer, next_g=None, final_g=None):
    L = x.shape[0]
    tm = _tile(L, 512)
    tf = 512
    n_f = D_FF // tf
    tail = 'final' if final_g is not None else ('emit' if next_g is not None else None)
    row = lambda: pl.BlockSpec((tm, D_MODEL), lambda i, j: (i, 0))
    vec = lambda: pl.BlockSpec((1, D_MODEL), lambda i, j: (0, 0))
    in_specs = [
        row(),
        vec(),
        pl.BlockSpec((None, D_MODEL, tf), lambda i, j: (layer, 0, j)),
        pl.BlockSpec((None, D_MODEL, tf), lambda i, j: (layer, 0, j + n_f)),
        pl.BlockSpec((None, tf, D_MODEL), lambda i, j: (layer, j, 0)),
    ]
    args = [x, g.reshape(1, D_MODEL), w_gu, w_gu, w_down]
    if tail is not None:
        in_specs.append(vec())
        args.append((final_g if tail == 'final' else next_g).reshape(1, D_MODEL))
    x_out = jax.ShapeDtypeStruct((L, D_MODEL), F32)
    emit = tail == 'emit'
    return pl.pallas_call(
        functools.partial(_ffn_kernel, n_f=n_f, tail=tail),
        grid=(L // tm, n_f),
        in_specs=in_specs,
        out_specs=(row(), row()) if emit else row(),
        out_shape=(x_out, jax.ShapeDtypeStruct((L, D_MODEL), BF16)) if emit else x_out,
        scratch_shapes=[pltpu.VMEM((tm, D_MODEL), BF16), pltpu.VMEM((tm, D_MODEL), F32)],
        compiler_params=_cparams("parallel", "arbitrary"),
        name="ffn_final" if tail == 'final' else "ffn",
    )(*args)


def _matmul_kernel(h_ref, w_ref, *rest, scaled):
    if scaled:
        c_ref, o_ref = rest
    else:
        (o_ref,) = rest
    z = jnp.dot(h_ref[...], w_ref[...], preferred_element_type=F32)
    if scaled:
        z = z * c_ref[...]
    o_ref[...] = z.astype(o_ref.dtype)


def _matmul(h, w, out_dtype, col_scale=None):
    L, K = h.shape
    N = w.shape[1]
    tm = _tile(L, 512)
    scaled = col_scale is not None
    in_specs = [pl.BlockSpec((tm, K), lambda i: (i, 0)),
                pl.BlockSpec((K, N), lambda i: (0, 0), pipeline_mode=pl.Buffered(1))]
    args = [h, w]
    if scaled:
        in_specs.append(pl.BlockSpec((1, N), lambda i: (0, 0)))
        args.append(col_scale.reshape(1, N))
    return pl.pallas_call(
        functools.partial(_matmul_kernel, scaled=scaled),
        grid=(L // tm,),
        in_specs=in_specs,
        out_specs=pl.BlockSpec((tm, N), lambda i: (i, 0)),
        out_shape=jax.ShapeDtypeStruct((L, N), out_dtype),
        compiler_params=_cparams("parallel"),
        name="matmul",
    )(*args)


def _norm_matmul_kernel(x_ref, g_ref, w_ref, *rest, scaled, tm, sub):
    if scaled:
        c_ref, o_ref = rest
    else:
        (o_ref,) = rest
    for c in range(tm // sub):
        rows = slice(c * sub, (c + 1) * sub)
        h = _rmsnorm(x_ref[rows, :], g_ref[...], EPS).astype(BF16)
        z = jnp.dot(h, w_ref[...], preferred_element_type=F32)
        if scaled:
            z = z * c_ref[...]
        o_ref[rows, :] = z.astype(o_ref.dtype)


def _norm_matmul(x, g, w, out_dtype, col_scale=None):
    L = x.shape[0]
    N = w.shape[1]
    tm = _tile(L, 512)
    scaled = col_scale is not None
    in_specs = [
        pl.BlockSpec((tm, D_MODEL), lambda i: (i, 0)),
        pl.BlockSpec((1, D_MODEL), lambda i: (0, 0)),
        pl.BlockSpec((D_MODEL, N), lambda i: (0, 0)),
    ]
    args = [x, g.reshape(1, D_MODEL), w]
    if scaled:
        in_specs.append(pl.BlockSpec((1, N), lambda i: (0, 0)))
        args.append(col_scale.reshape(1, N))
    return pl.pallas_call(
        functools.partial(_norm_matmul_kernel, scaled=scaled, tm=tm, sub=_tile(tm, 256)),
        grid=(L // tm,),
        in_specs=in_specs,
        out_specs=pl.BlockSpec((tm, N), lambda i: (i, 0)),
        out_shape=jax.ShapeDtypeStruct((L, N), out_dtype),
        compiler_params=_cparams("parallel"),
        name="norm_matmul",
    )(*args)


def _odd_proj_kernel(h_ref, w_ref, qg_ref, kg_ref, cos_ref, sin_ref, o_ref, *, heads_per_tile):
    j = pl.program_id(1)
    n_q_tiles = GQA_HEADS // heads_per_tile
    z = jnp.dot(h_ref[...], w_ref[...], preferred_element_type=F32)
    lane = lax.broadcasted_iota(jnp.int32, (1, GQA_HEAD), 1)
    first_half = (lane & (ROPE_AXIS - 1)) < (ROPE_AXIS // 2)

    def normed_rotated(hh, gain, cos, sin):
        sl = slice(hh * GQA_HEAD, (hh + 1) * GQA_HEAD)
        y = _rmsnorm(z[:, sl], gain, EPS)
        partner = jnp.where(first_half, pltpu.roll(y, GQA_HEAD - ROPE_AXIS // 2, 1),
                            pltpu.roll(y, ROPE_AXIS // 2, 1))
        o_ref[:, sl] = (y * cos + partner * sin).astype(o_ref.dtype)

    @pl.when(j < n_q_tiles)
    def _():
        out_scale = GQA_HEAD ** -0.5 * LOG2E
        cos = cos_ref[...] * out_scale
        sin = sin_ref[...] * out_scale
        for hh in range(heads_per_tile):
            normed_rotated(hh, qg_ref[...], cos, sin)

    @pl.when(j >= n_q_tiles)
    def _():
        for hh in range(GQA_KV_HEADS):
            normed_rotated(hh, kg_ref[...], cos_ref[...], sin_ref[...])
        v_cols = slice(GQA_KV_HEADS * GQA_HEAD, 2 * GQA_KV_HEADS * GQA_HEAD)
        o_ref[:, v_cols] = z[:, v_cols].astype(o_ref.dtype)


def _odd_proj(h, w, q_g, k_g, cos, sin):
    L = h.shape[0]
    N = w.shape[1]
    tm = _tile(L, 512)
    heads_per_tile = 2 * GQA_KV_HEADS
    tn = heads_per_tile * GQA_HEAD
    return pl.pallas_call(
        functools.partial(_odd_proj_kernel, heads_per_tile=heads_per_tile),
        grid=(L // tm, N // tn),
        in_specs=[
            pl.BlockSpec((tm, D_MODEL), lambda i, j: (i, 0)),
            pl.BlockSpec((D_MODEL, tn), lambda i, j: (0, j)),
            pl.BlockSpec((1, GQA_HEAD), lambda i, j: (0, 0)),
            pl.BlockSpec((1, GQA_HEAD), lambda i, j: (0, 0)),
            pl.BlockSpec((tm, GQA_HEAD), lambda i, j: (i, 0)),
            pl.BlockSpec((tm, GQA_HEAD), lambda i, j: (i, 0)),
        ],
        out_specs=pl.BlockSpec((tm, tn), lambda i, j: (i, j)),
        out_shape=jax.ShapeDtypeStruct((L, N), BF16),
        compiler_params=_cparams("parallel", "arbitrary"),
        name="odd_proj",
    )(h, w, q_g.reshape(1, GQA_HEAD), k_g.reshape(1, GQA_HEAD), cos, sin)


def _matmul_residual_kernel(a_ref, w_ref, r_ref, *rest, emit):
    y = r_ref[...] + jnp.dot(a_ref[...], w_ref[...], preferred_element_type=F32)
    if emit:
        ng_ref, o_ref, hn_ref = rest
        hn_ref[...] = _rmsnorm(y, ng_ref[...], EPS).astype(hn_ref.dtype)
    else:
        (o_ref,) = rest
    o_ref[...] = y


def _matmul_residual(a, w, res, next_g=None):
    L, K = a.shape
    N = w.shape[1]
    tm = _tile(L, 512)
    emit = next_g is not None
    row = lambda: pl.BlockSpec((tm, N), lambda i: (i, 0))
    in_specs = [pl.BlockSpec((tm, K), lambda i: (i, 0)),
                pl.BlockSpec((K, N), lambda i: (0, 0), pipeline_mode=pl.Buffered(1)), row()]
    args = [a, w, res]
    if emit:
        in_specs.append(pl.BlockSpec((1, N), lambda i: (0, 0)))
        args.append(next_g.reshape(1, N))
    y_out = jax.ShapeDtypeStruct((L, N), F32)
    return pl.pallas_call(
        functools.partial(_matmul_residual_kernel, emit=emit),
        grid=(L // tm,),
        in_specs=in_specs,
        out_specs=(row(), row()) if emit else row(),
        out_shape=(y_out, jax.ShapeDtypeStruct((L, N), BF16)) if emit else y_out,
        compiler_params=_cparams("parallel"),
        name="matmul_residual",
    )(*args)


def _lane_tile(x, n):
    return jnp.concatenate([x] * n, axis=1) if n > 1 else x


def _softmax_block(t, shift, v_ones, m, acc):
    row_max = jnp.max(t, axis=1, keepdims=True)
    m_new = jnp.maximum(m, row_max if shift is None else row_max + shift)
    alpha = jnp.exp2(m - m_new)
    p = jnp.exp2((t - _lane_tile(m_new if shift is None else m_new - shift, t.shape[1] // LANES)).astype(BF16))
    acc = _lane_tile(alpha, 2) * acc + jnp.dot(p, v_ones, preferred_element_type=F32)
    return m_new, acc


def _softmax_init(m_ref, acc_ref, vo_ref):
    m_ref[...] = jnp.full_like(m_ref, -jnp.inf)
    acc_ref[...] = jnp.zeros_like(acc_ref)
    vo_ref[:, LANES:] = jnp.ones((vo_ref.shape[0], LANES), vo_ref.dtype)


def _softmax_result(acc):
    return acc[:, :LANES] / acc[:, LANES:]


_NT = (((1,), (1,)), ((), ()))


def _gqa_kernel(q_ref, k_ref, v_ref, o_ref, m_ref, acc_ref, vo_ref, *, n_k, tq, tk, rows, keys):
    ki = pl.program_id(2)

    @pl.when(ki == 0)
    def _():
        _softmax_init(m_ref, acc_ref, vo_ref)

    vo_ref[:, :LANES] = v_ref[...]
    for g in range(GQA_GROUP):
        for c in range(tq // rows):
            sl = slice(g * tq + c * rows, g * tq + (c + 1) * rows)
            q = q_ref[c * rows:(c + 1) * rows, g * GQA_HEAD:(g + 1) * GQA_HEAD]
            state = (m_ref[sl, :], acc_ref[sl, :])
            for kk in range(tk // keys):
                ksl = slice(kk * keys, (kk + 1) * keys)
                s = lax.dot_general(q, k_ref[ksl, :], _NT, preferred_element_type=F32)
                state = _softmax_block(s, None, vo_ref[ksl, :], *state)
            m_ref[sl, :], acc_ref[sl, :] = state

    @pl.when(ki == n_k - 1)
    def _():
        for g in range(GQA_GROUP):
            sl = slice(g * tq, (g + 1) * tq)
            o_ref[:, g * GQA_HEAD:(g + 1) * GQA_HEAD] = _softmax_result(acc_ref[sl, :]).astype(o_ref.dtype)


def _gqa_attention(qkv):
    L = qkv.shape[0]
    tq = _tile(L, GQA_TQ)
    tk = _tile(L, GQA_TK)
    n_k = L // tk
    group_cols = GQA_GROUP * GQA_HEAD
    return pl.pallas_call(
        functools.partial(_gqa_kernel, n_k=n_k, tq=tq, tk=tk, rows=_tile(tq, ATTN_ROWS), keys=_tile(tk, ATTN_KEYS)),
        grid=(GQA_KV_HEADS, L // tq, n_k),
        in_specs=[
            pl.BlockSpec((tq, group_cols), lambda h, i, j: (i, h)),
            pl.BlockSpec((tk, GQA_HEAD), lambda h, i, j: (j, GQA_HEADS + h)),
            pl.BlockSpec((tk, GQA_HEAD), lambda h, i, j: (j, GQA_HEADS + GQA_KV_HEADS + h)),
        ],
        out_specs=pl.BlockSpec((tq, group_cols), lambda h, i, j: (i, h)),
        out_shape=jax.ShapeDtypeStruct((L, D_MODEL), BF16),
        scratch_shapes=[
            pltpu.VMEM((GQA_GROUP * tq, LANES), F32),
            pltpu.VMEM((GQA_GROUP * tq, 2 * LANES), F32),
            pltpu.VMEM((tk, 2 * LANES), BF16),
        ],
        compiler_params=_cparams("parallel", "parallel", "arbitrary"),
        name="gqa_attention",
    )(qkv, qkv, qkv)


def _sq_norm_max_kernel(x_ref, o_ref, *, half):
    x = x_ref[...].astype(F32)
    x2 = x * x
    first = lax.broadcasted_iota(jnp.int32, (1, LANES), 1) < half
    blocks = []
    for b in range(x2.shape[1] // LANES):
        xb = x2[:, b * LANES:(b + 1) * LANES]
        n1 = jnp.max(jnp.sum(jnp.where(first, xb, 0.0), axis=1, keepdims=True), axis=0, keepdims=True)
        n2 = jnp.max(jnp.sum(jnp.where(first, 0.0, xb), axis=1, keepdims=True), axis=0, keepdims=True)
        blocks.append(jnp.where(first, n1, n2))
    cur = jnp.concatenate(blocks, axis=1)

    @pl.when(pl.program_id(0) == 0)
    def _():
        o_ref[...] = cur

    @pl.when(pl.program_id(0) > 0)
    def _():
        o_ref[...] = jnp.maximum(o_ref[...], cur)


def _sq_norm_max(x, n_cols, half):
    L = x.shape[0]
    tm = _tile(L, 512)
    return pl.pallas_call(
        functools.partial(_sq_norm_max_kernel, half=half),
        grid=(L // tm,),
        in_specs=[pl.BlockSpec((tm, n_cols), lambda i: (i, 0))],
        out_specs=pl.BlockSpec((1, n_cols), lambda i: (0, 0)),
        out_shape=jax.ShapeDtypeStruct((1, n_cols), F32),
        compiler_params=_cparams("arbitrary"),
        name="sq_norm_max",
    )(x)


def _diff_kernel(klo_ref, khi_ref, slope_ref, q_ref, k_ref, v_ref, lq1_ref, lk1_ref, lq2_ref, lk2_ref, sg_ref,
                 o_ref, qs_ref, m_ref, acc_ref, vo_ref, *, n_k, tq, tk, rows, keys, lambda_init):
    h = pl.program_id(0)
    qi = pl.program_id(1)
    j = pl.program_id(2)
    q0 = qi * tq
    kt = klo_ref[h, qi] + j
    active = kt <= khi_ref[h, qi]
    k0 = kt * tk
    n_c = tq // rows

    @pl.when(j == 0)
    def _():
        _softmax_init(m_ref, acc_ref, vo_ref)
        q = q_ref[...]
        lane = lax.broadcasted_iota(jnp.int32, (1, 2 * DIFF_HEAD), 1)
        zero = jnp.zeros_like(q)
        q1 = jnp.where(lane < DIFF_HEAD, q, zero)
        q2 = jnp.where(lane >= DIFF_HEAD, q, zero)
        for c in range(n_c):
            qs_ref[2 * c * rows:(2 * c + 1) * rows, :] = q1[c * rows:(c + 1) * rows]
            qs_ref[(2 * c + 1) * rows:(2 * c + 2) * rows, :] = q2[c * rows:(c + 1) * rows]

    slope2 = slope_ref[h] * LOG2E
    d0 = lax.broadcasted_iota(jnp.int32, (rows, keys), 0) - lax.broadcasted_iota(jnp.int32, (rows, keys), 1)

    def step(bias_fn):
        vo_ref[:, :LANES] = v_ref[...]
        for c in range(n_c):
            sl = slice(2 * c * rows, 2 * (c + 1) * rows)
            state = (m_ref[sl, :], acc_ref[sl, :])
            for kk in range(tk // keys):
                ksl = slice(kk * keys, (kk + 1) * keys)
                bias, shift = bias_fn(q0 + c * rows - (k0 + kk * keys))
                t = lax.dot_general(qs_ref[sl, :], k_ref[ksl, :], _NT, preferred_element_type=F32)
                t = t + jnp.concatenate([bias, bias], axis=0)
                state = _softmax_block(t, shift, vo_ref[ksl, :], *state)
            m_ref[sl, :], acc_ref[sl, :] = state

    crosses_diagonal = jnp.logical_and(k0 < q0 + tq, q0 < k0 + tk)

    @pl.when(jnp.logical_and(active, crosses_diagonal))
    def _():
        step(lambda off: ((-slope2) * jnp.abs(off + d0).astype(F32), None))

    @pl.when(jnp.logical_and(active, jnp.logical_not(crosses_diagonal)))
    def _():
        coef = jnp.where(q0 >= k0, -slope2, slope2)
        tile = coef * d0.astype(F32)
        step(lambda off: (tile, coef * off.astype(F32)))

    @pl.when(j == n_k - 1)
    def _():
        lam = (jnp.exp(jnp.sum(lq1_ref[...] * lk1_ref[...], axis=-1, keepdims=True))
               - jnp.exp(jnp.sum(lq2_ref[...] * lk2_ref[...], axis=-1, keepdims=True)) + lambda_init)
        for c in range(n_c):
            o1 = _softmax_result(acc_ref[2 * c * rows:(2 * c + 1) * rows, :])
            o2 = _softmax_result(acc_ref[(2 * c + 1) * rows:(2 * c + 2) * rows, :])
            o_ref[c * rows:(c + 1) * rows, :] = (
                _rmsnorm(o1 - lam * o2, sg_ref[...], DIFF_SUBLN_EPS) * (1.0 - lambda_init)).astype(o_ref.dtype)


def _diff_key_ranges(qkv, slopes, tq, tk):
    L = qkv.shape[0]
    n_q, n_k = L // tq, L // tk
    sq = _sq_norm_max(qkv, 2 * DIFF_WIDTH, DIFF_HEAD).reshape(2, DIFF_HEADS, 2, DIFF_HEAD)[..., 0]
    norms = jnp.sqrt(sq)
    bound = jnp.max(norms[0] * norms[1], axis=1) * NORM_BOUND_MARGIN
    radius = (SOFTMAX_UNDERFLOW_LOG2 + 2.0 * bound) / (slopes * LOG2E)
    radius = jnp.where(jnp.isfinite(radius), radius, float(L))[:, None]
    q_first = (jnp.arange(n_q, dtype=F32) * tq)[None, :]
    klo = jnp.clip(jnp.floor((q_first - radius) / tk), 0, n_k - 1).astype(jnp.int32)
    khi = jnp.clip(jnp.floor((q_first + (tq - 1) + radius) / tk), 0, n_k - 1).astype(jnp.int32)
    return klo, khi


def _diff_attention(qkv, lq1, lk1, lq2, lk2, subln_g, layer_idx):
    L = qkv.shape[0]
    tq = _tile(L, DIFF_TQ)
    tk = _tile(L, DIFF_TK)
    n_k = L // tk
    lambda_init = 0.8 - 0.6 * math.exp(-0.3 * layer_idx)
    slopes = jnp.asarray([2.0 ** (-8.0 * (i + 1) / DIFF_HEADS) for i in range(DIFF_HEADS)], dtype=F32)
    klo, khi = _diff_key_ranges(qkv, slopes, tq, tk)
    key_tile = lambda h, i, j, klo, khi: jnp.minimum(klo[h, i] + j, khi[h, i])
    vec = lambda: pl.BlockSpec((1, DIFF_HEAD), lambda h, i, j, klo, khi: (0, 0))
    return pl.pallas_call(
        functools.partial(_diff_kernel, n_k=n_k, tq=tq, tk=tk, rows=_tile(tq, ATTN_ROWS), keys=_tile(tk, ATTN_KEYS),
                          lambda_init=lambda_init),
        grid_spec=pltpu.PrefetchScalarGridSpec(
            num_scalar_prefetch=2,
            grid=(DIFF_HEADS, L // tq, n_k),
            in_specs=[
                pl.BlockSpec(memory_space=pltpu.SMEM),
                pl.BlockSpec((tq, DIFF_VHEAD), lambda h, i, j, klo, khi: (i, h)),
                pl.BlockSpec((tk, DIFF_VHEAD), lambda h, i, j, klo, khi: (key_tile(h, i, j, klo, khi), DIFF_HEADS + h)),
                pl.BlockSpec((tk, DIFF_VHEAD),
                             lambda h, i, j, klo, khi: (key_tile(h, i, j, klo, khi), 2 * DIFF_HEADS + h)),
                vec(), vec(), vec(), vec(),
                pl.BlockSpec((1, DIFF_VHEAD), lambda h, i, j, klo, khi: (0, 0)),
            ],
            out_specs=pl.BlockSpec((tq, DIFF_VHEAD), lambda h, i, j, klo, khi: (i, h)),
            scratch_shapes=[
                pltpu.VMEM((2 * tq, DIFF_VHEAD), BF16),
                pltpu.VMEM((2 * tq, LANES), F32),
                pltpu.VMEM((2 * tq, 2 * LANES), F32),
                pltpu.VMEM((tk, 2 * LANES), BF16),
            ],
        ),
        out_shape=jax.ShapeDtypeStruct((L, DIFF_WIDTH), BF16),
        compiler_params=_cparams("parallel", "parallel", "arbitrary"),
        name="diff_attention",
    )(klo, khi, slopes, qkv, qkv, qkv, lq1.reshape(1, -1), lk1.reshape(1, -1), lq2.reshape(1, -1),
      lk2.reshape(1, -1), subln_g.reshape(1, -1))


def _cross_kernel(q_ref, kv_ref, o_ref, *, scale):
    for h in range(X_HEADS):
        sl = slice(h * X_HEAD, (h + 1) * X_HEAD)
        vsl = slice(D_MODEL + h * X_HEAD, D_MODEL + (h + 1) * X_HEAD)
        s = lax.dot_general(q_ref[:, sl], kv_ref[:, sl], (((1,), (1,)), ((), ())),
                            preferred_element_type=F32) * scale
        e = jnp.exp(s - jnp.max(s, axis=-1, keepdims=True))
        p = e / jnp.sum(e, axis=-1, keepdims=True)
        o_ref[:, sl] = jnp.dot(p.astype(BF16), kv_ref[:, vsl], preferred_element_type=F32).astype(o_ref.dtype)


def _cross_attention(q, kv):
    L = q.shape[0]
    n_mem = kv.shape[0]
    tq = _tile(L, 512)
    return pl.pallas_call(
        functools.partial(_cross_kernel, scale=X_HEAD ** -0.5),
        grid=(L // tq,),
        in_specs=[
            pl.BlockSpec((tq, D_MODEL), lambda i: (i, 0)),
            pl.BlockSpec((n_mem, 2 * D_MODEL), lambda i: (0, 0)),
        ],
        out_specs=pl.BlockSpec((tq, D_MODEL), lambda i: (i, 0)),
        out_shape=jax.ShapeDtypeStruct((L, D_MODEL), BF16),
        compiler_params=_cparams("parallel"),
        name="cross_attention",
    )(q, kv)


def _s5_table_kernel(ar_ref, ai_ref, br_ref, bi_ref, o_ref):
    hi = lax.Precision.HIGHEST
    o_ref[0, 0] = (jnp.dot(ar_ref[0, 0], br_ref[0, 0], preferred_element_type=F32, precision=hi)
                   - jnp.dot(ai_ref[0, 0], bi_ref[0, 0], preferred_element_type=F32, precision=hi))


def _s5_chunk_kernel_table(ca_re, ca_im, bb_re, bb_im):
    T = ca_re.shape[0]
    rows = T * S5_GROUP
    stack = lambda a: a.transpose(1, 2, 0, 3, 4).reshape(2, S5_GROUPS, rows, S5_STATE)
    lhs = pl.BlockSpec((1, 1, rows, S5_STATE), lambda d, g: (d, g, 0, 0))
    rhs = pl.BlockSpec((1, 1, S5_STATE, S5_GROUP), lambda d, g: (d, g, 0, 0))
    out = pl.pallas_call(
        _s5_table_kernel,
        grid=(2, S5_GROUPS),
        in_specs=[lhs, lhs, rhs, rhs],
        out_specs=pl.BlockSpec((1, 1, rows, S5_GROUP), lambda d, g: (d, g, 0, 0)),
        out_shape=jax.ShapeDtypeStruct((2, S5_GROUPS, rows, S5_GROUP), F32),
        compiler_params=_cparams("parallel", "parallel"),
        name="s5_kernel_table",
    )(stack(ca_re), stack(ca_im), bb_re, bb_im)
    return out.reshape(2, S5_GROUPS, T, S5_GROUP, S5_GROUP).transpose(2, 0, 1, 3, 4)


def _s5_weights(lam_re, lam_im, log_dt, b_re, b_im, c_re, c_im):
    T = S5_CHUNK
    lr = jnp.minimum(lam_re, S5_LAMBDA_RE_MAX)
    li = lam_im
    dt = jnp.exp(log_dt)[..., None]
    mag = jnp.exp(lr * dt)
    ab_re = mag * jnp.cos(li * dt)
    ab_im = mag * jnp.sin(li * dt)
    nr = ab_re - 1.0
    den = lr * lr + li * li
    f_re = ((nr * lr + ab_im * li) / den)[..., None]
    f_im = ((ab_im * lr - nr * li) / den)[..., None]
    bb_re = f_re * b_re - f_im * b_im
    bb_im = f_re * b_im + f_im * b_re
    k = jnp.arange(T + 1, dtype=F32)[:, None, None, None]
    pmag = jnp.exp(lr * dt * k)
    pw_re = pmag * jnp.cos(li * dt * k)
    pw_im = pmag * jnp.sin(li * dt * k)
    ca_re = c_re * pw_re[:, :, :, None, :] - c_im * pw_im[:, :, :, None, :]
    ca_im = c_re * pw_im[:, :, :, None, :] + c_im * pw_re[:, :, :, None, :]
    nb, ng = S5_LANE_BLOCKS, S5_LANE_GROUPS

    def group_diagonal(x, rows_per_group):
        w = x.shape[-1]
        reps = jnp.tile(x.astype(BF16), (1, 1, 1, ng))
        row_group = (jnp.arange(x.shape[1]) // rows_per_group) % ng
        col_group = jnp.arange(ng * w) // w
        keep = (row_group[:, None] == col_group[None, :])[None, :, None, :]
        return jnp.where(keep, reps, jnp.zeros((), BF16))

    kern = _s5_chunk_kernel_table(ca_re[:T], ca_im[:T], bb_re, bb_im)
    ktab = jnp.concatenate([kern[1:, 1][::-1], (kern[0, 0] + kern[0, 1])[None], kern[1:, 0]], axis=0)
    kblk = ktab.reshape(2 * T - 1, nb, ng, S5_GROUP, S5_GROUP).transpose(0, 1, 2, 4, 3)
    kblk = group_diagonal(kblk.reshape((2 * T - 1) * nb, LANES, 1, S5_GROUP), S5_GROUP)
    kblk = kblk.reshape(2 * T - 1, nb, LANES, LANES)
    idx = jnp.arange(T)[None, :] - jnp.arange(T)[:, None] + (T - 1)
    toep = kblk[idx].transpose(2, 0, 3, 1, 4).reshape(nb, S5_BLOCK_COLS, S5_BLOCK_COLS)

    e_re = jnp.stack([pw_re[:T, 0][::-1], pw_re[:T, 1]], axis=1)[..., None]
    e_im = jnp.stack([pw_im[:T, 0][::-1], pw_im[:T, 1]], axis=1)[..., None]
    zb = jnp.stack([e_re * bb_re - e_im * bb_im, e_re * bb_im + e_im * bb_re], axis=2)
    z_rows = zb.reshape(T, 2, 2, nb, ng, S5_STATE, S5_GROUP).transpose(3, 0, 4, 6, 1, 2, 5)
    w_in = group_diagonal(z_rows.reshape(nb, S5_BLOCK_COLS, 4, S5_STATE), S5_GROUP)
    w_in = w_in.reshape(nb, S5_BLOCK_COLS, 4 * S5_BLOCK_STATE)

    cr = jnp.stack([ca_re[1:, 0], ca_re[1:, 1][::-1]], axis=1)
    ci = jnp.stack([ca_im[1:, 0], ca_im[1:, 1][::-1]], axis=1)
    c_rows = jnp.stack([cr, -ci], axis=2).reshape(T, 2, 2, nb, ng, S5_GROUP, S5_STATE).transpose(3, 1, 2, 4, 6, 0, 5)
    w_out = group_diagonal(c_rows.reshape(nb, 4 * S5_BLOCK_STATE, T, S5_GROUP), S5_STATE)
    w_out = w_out.reshape(nb, 4 * S5_BLOCK_STATE, S5_BLOCK_COLS)

    a4 = jnp.stack([pw_re[T], pw_im[T]], axis=1).reshape(2, 2, nb, ng, S5_STATE)
    a_t = a4.transpose(2, 0, 1, 3, 4).reshape(nb, 1, 4 * S5_BLOCK_STATE)
    return toep, w_in, w_out, a_t


def _s5_gather_chunks(u_ref, lhs_ref, tr):
    for t in range(S5_CHUNK):
        lhs_ref[:, t * LANES:(t + 1) * LANES] = u_ref[pl.ds(t, tr, stride=S5_CHUNK), :].astype(BF16)


def _s5_in_kernel(u_ref, w_ref, z_ref, lhs_ref, *, tr):
    _s5_gather_chunks(u_ref, lhs_ref, tr)
    z_ref[0] = jnp.dot(lhs_ref[...], w_ref[0], preferred_element_type=F32)


def _s5_out_kernel(u_ref, toep_ref, x_ref, w_ref, d_ref, y_ref, lhs_ref, *, tr):
    _s5_gather_chunks(u_ref, lhs_ref, tr)
    y = (jnp.dot(lhs_ref[...], toep_ref[0], preferred_element_type=F32)
         + jnp.dot(x_ref[0].astype(BF16), w_ref[0], preferred_element_type=F32))
    d = d_ref[...]
    for t in range(S5_CHUNK):
        rows = pl.ds(t, tr, stride=S5_CHUNK)
        y_ref[rows, :] = d * u_ref[rows, :] + y[:, t * LANES:(t + 1) * LANES]


def _s5_scan_kernel(z_ref, a_ref, x_ref, *, n_rows):
    n_s = S5_BLOCK_STATE
    a = a_ref[0]
    af_re, af_im, ab_re, ab_im = (a[:, i * n_s:(i + 1) * n_s] for i in range(4))

    def body(n, carry):
        f_re, f_im, b_re, b_im = carry
        rf = pl.ds(n, 1)
        rb = pl.ds(n_rows - 1 - n, 1)
        x_ref[0, rf, 0 * n_s:1 * n_s] = f_re
        x_ref[0, rf, 1 * n_s:2 * n_s] = f_im
        x_ref[0, rb, 2 * n_s:3 * n_s] = b_re
        x_ref[0, rb, 3 * n_s:4 * n_s] = b_im
        zf_re = z_ref[0, rf, 0 * n_s:1 * n_s]
        zf_im = z_ref[0, rf, 1 * n_s:2 * n_s]
        zb_re = z_ref[0, rb, 2 * n_s:3 * n_s]
        zb_im = z_ref[0, rb, 3 * n_s:4 * n_s]
        return (af_re * f_re - af_im * f_im + zf_re, af_re * f_im + af_im * f_re + zf_im,
                ab_re * b_re - ab_im * b_im + zb_re, ab_re * b_im + ab_im * b_re + zb_im)

    zero = jnp.zeros((1, n_s), F32)
    lax.fori_loop(0, n_rows, body, (zero, zero, zero, zero))


def _s5_glu_kernel(y_ref, w_ref, b_ref, o_ref):
    g = _gelu_tanh(y_ref[...])
    gate = jnp.dot(g.astype(BF16), w_ref[...], preferred_element_type=F32) + b_ref[...]
    o_ref[...] = (g * _sigmoid(gate)).astype(o_ref.dtype)


def _s5_mixer(u, s5w, d_skip, glu_w, glu_b):
    L = u.shape[0]
    toep, w_in, w_out, a_t = s5w
    nb = S5_LANE_BLOCKS
    n_rows = L // S5_CHUNK
    n_st = 4 * S5_BLOCK_STATE
    tr = _tile(n_rows, 256)
    tokens = pl.BlockSpec((tr * S5_CHUNK, LANES), lambda c, i: (i, c))
    blk = lambda cols: pl.BlockSpec((1, tr, cols), lambda c, i: (c, i, 0))
    wgt = lambda rows, cols: pl.BlockSpec((1, rows, cols), lambda c, i: (c, 0, 0))
    chunk_lhs = pltpu.VMEM((tr, S5_BLOCK_COLS), BF16)
    z = pl.pallas_call(
        functools.partial(_s5_in_kernel, tr=tr),
        grid=(nb, n_rows // tr),
        in_specs=[tokens, wgt(S5_BLOCK_COLS, n_st)],
        out_specs=blk(n_st),
        out_shape=jax.ShapeDtypeStruct((nb, n_rows, n_st), F32),
        scratch_shapes=[chunk_lhs],
        compiler_params=_cparams("parallel", "parallel"),
        name="s5_chunk_input",
    )(u, w_in)
    x = pl.pallas_call(
        functools.partial(_s5_scan_kernel, n_rows=n_rows),
        grid=(nb,),
        in_specs=[pl.BlockSpec((1, n_rows, n_st), lambda c: (c, 0, 0)),
                  pl.BlockSpec((1, 1, n_st), lambda c: (c, 0, 0))],
        out_specs=pl.BlockSpec((1, n_rows, n_st), lambda c: (c, 0, 0)),
        out_shape=jax.ShapeDtypeStruct((nb, n_rows, n_st), F32),
        compiler_params=_cparams("parallel"),
        name="s5_chunk_scan",
    )(z, a_t)
    y = pl.pallas_call(
        functools.partial(_s5_out_kernel, tr=tr),
        grid=(nb, n_rows // tr),
        in_specs=[tokens, wgt(S5_BLOCK_COLS, S5_BLOCK_COLS), blk(n_st), wgt(n_st, S5_BLOCK_COLS),
                  pl.BlockSpec((1, LANES), lambda c, i: (0, c))],
        out_specs=tokens,
        out_shape=jax.ShapeDtypeStruct((L, S5_WIDTH), F32),
        scratch_shapes=[chunk_lhs],
        compiler_params=_cparams("parallel", "parallel"),
        name="s5_chunk_output",
    )(u, toep, x, w_out, d_skip.reshape(1, -1))
    tm = _tile(L, 512)
    row = lambda: pl.BlockSpec((tm, S5_WIDTH), lambda i: (i, 0))
    vec = lambda: pl.BlockSpec((1, S5_WIDTH), lambda i: (0, 0))
    return pl.pallas_call(
        _s5_glu_kernel,
        grid=(L // tm,),
        in_specs=[row(), pl.BlockSpec((S5_WIDTH, S5_WIDTH), lambda i: (0, 0)), vec()],
        out_specs=row(),
        out_shape=jax.ShapeDtypeStruct((L, S5_WIDTH), BF16),
        compiler_params=_cparams("parallel"),
        name="s5_glu",
    )(y, glu_w, glu_b.reshape(1, -1))


def _rope_tables(L):
    rows = L // GRID_W
    r = jnp.repeat(jnp.arange(rows, dtype=F32), GRID_W)
    c = jnp.tile(jnp.arange(GRID_W, dtype=F32), rows)
    inv = ROPE_THETA ** (-jnp.arange(0, ROPE_AXIS, 2, dtype=F32) / ROPE_AXIS)
    ar = r[:, None] * inv
    ac = c[:, None] * inv
    cos = jnp.concatenate([jnp.cos(ar), jnp.cos(ar), jnp.cos(ac), jnp.cos(ac)], axis=-1)
    sin = jnp.concatenate([-jnp.sin(ar), jnp.sin(ar), -jnp.sin(ac), jnp.sin(ac)], axis=-1)
    return cos, sin


def _encoder(x, mem, p):
    L = x.shape[0]
    cos, sin = _rope_tables(L)
    for l in range(DEPTH):
        x, h = _ffn(x, p['ffn1_norm'][l], p['ffn1_w_gu'], p['ffn1_w_down'], l, next_g=p['mix_norm'][l])
        if l % 2 == 0:
            e = l // 2
            u = _matmul(h, p['even_w_in_u'][e], F32)
            qkv = _matmul(h, p['even_w_in_qkv'][e], BF16, col_scale=p['diff_q_scale'])
            ya = _s5_mixer(u, p['s5'][e], p['s5_d'][e], p['s5_glu_w'][e], p['s5_glu_b'][e])
            yb = _diff_attention(qkv, p['diff_lambda_q1'][e], p['diff_lambda_k1'][e], p['diff_lambda_q2'][e],
                                 p['diff_lambda_k2'][e], p['diff_subln'][e], l)
            mixed, w_out = jnp.concatenate([ya, yb], axis=-1), p['even_w_out'][e]
        else:
            o = l // 2
            qkv = _odd_proj(h, p['odd_w_in'][o], p['gqa_q_norm'][o], p['gqa_k_norm'][o], cos, sin)
            mixed, w_out = _gqa_attention(qkv), p['odd_w_out'][o]
        x, h = _matmul_residual(mixed, w_out, x, next_g=p['cross_norm'][l])
        q = _matmul(h, p['cross_w_q'][l], BF16)
        kv = _norm_matmul(mem, p['mem_norm'][l], p['cross_w_kv'][l], BF16)
        x = _matmul_residual(_cross_attention(q, kv), p['cross_w_o'][l], x)
        x = _ffn(x, p['ffn2_norm'][l], p['ffn2_w_gu'], p['ffn2_w_down'], l,
                 final_g=p['final_norm'] if l == DEPTH - 1 else None)
    return x


def kernel(x_prompt, x_sample, mem_prompt, mem_sample, ffn1_norm, ffn1_w_gu, ffn1_w_down, mix_norm, even_w_in, even_w_out, s5_lambda_re, s5_lambda_im, s5_log_dt, s5_b_re, s5_b_im, s5_c_re, s5_c_im, s5_d, s5_glu_w, s5_glu_b, diff_lambda_q1, diff_lambda_k1, diff_lambda_q2, diff_lambda_k2, diff_subln, odd_w_in, odd_w_out, gqa_q_norm, gqa_k_norm, cross_norm, mem_norm, cross_w_q, cross_w_kv, cross_w_o, ffn2_norm, ffn2_w_gu, ffn2_w_down, final_norm):
    bf = lambda w: w.astype(BF16)
    diff_q_scale = jnp.concatenate([jnp.full((DIFF_WIDTH,), DIFF_HEAD ** -0.5 * LOG2E, F32),
                                    jnp.ones((2 * DIFF_WIDTH,), F32)])
    p = dict(
        ffn1_norm=ffn1_norm, ffn1_w_gu=bf(ffn1_w_gu), ffn1_w_down=bf(ffn1_w_down), mix_norm=mix_norm,
        even_w_in_u=bf(even_w_in[:, :, :S5_WIDTH]), even_w_in_qkv=bf(even_w_in[:, :, S5_WIDTH:]),
        diff_q_scale=diff_q_scale,
        even_w_out=bf(even_w_out),
        s5=[_s5_weights(s5_lambda_re[e], s5_lambda_im[e], s5_log_dt[e], s5_b_re[e], s5_b_im[e], s5_c_re[e], s5_c_im[e])
            for e in range(s5_lambda_re.shape[0])],
        s5_d=s5_d, s5_glu_w=bf(s5_glu_w), s5_glu_b=s5_glu_b,
        diff_lambda_q1=diff_lambda_q1, diff_lambda_k1=diff_lambda_k1, diff_lambda_q2=diff_lambda_q2,
        diff_lambda_k2=diff_lambda_k2, diff_subln=diff_subln,
        odd_w_in=bf(odd_w_in), odd_w_out=bf(odd_w_out), gqa_q_norm=gqa_q_norm, gqa_k_norm=gqa_k_norm,
        cross_norm=cross_norm, mem_norm=mem_norm, cross_w_q=bf(cross_w_q), cross_w_kv=bf(cross_w_kv),
        cross_w_o=bf(cross_w_o), ffn2_norm=ffn2_norm, ffn2_w_gu=bf(ffn2_w_gu), ffn2_w_down=bf(ffn2_w_down),
        final_norm=final_norm)
    outs = []
    for x, mem in ((x_prompt, mem_prompt), (x_sample, mem_sample)):
        outs.append(jnp.stack([_encoder(x[b], mem[b], p) for b in range(x.shape[0])]))
    return tuple(outs)
```
